```python
import math
import jax, jax.numpy as jnp
from jax import lax
import numpy as np

D_MODEL = 1024
BATCH = 8
SEQ = 8192
DEPTH = 2

GRID_W = 64
CTX_LEN = 256
MIX_WIDTH = D_MODEL
F_WIDTH = D_MODEL // 4
F_GROUPS = 4
F_GROUP_DIM = F_WIDTH // F_GROUPS
ATT_WIDTH = D_MODEL // 2
ATT_HEADS = 4
V_DIM = ATT_WIDTH // ATT_HEADS
QK_DIM = V_DIM // 2
ATT_SCALE = QK_DIM ** -0.5
Q_BLOCK = 128
ROPE_HALF = QK_DIM // 2
ROPE_FREQS = ROPE_HALF // 2
ROPE_BASE = 10000.0
LRU_WIDTH = D_MODEL // 4
LRU_BLOCKS = 4
LRU_BLOCK_DIM = LRU_WIDTH // LRU_BLOCKS
LRU_C = 8.0
CONV_W = 4
CONV_LEFT = (CONV_W - 1) // 2
Q_OFF = F_WIDTH
K_OFF = Q_OFF + ATT_HEADS * 2 * QK_DIM
V_OFF = K_OFF + ATT_HEADS * 2 * QK_DIM
Y_OFF = V_OFF + ATT_WIDTH
R_OFF = Y_OFF + LRU_WIDTH
IN_WIDTH = R_OFF + LRU_WIDTH
N_GROUPS = 4
EXPERTS_PER_GROUP = 8
N_EXPERTS = N_GROUPS * EXPERTS_PER_GROUP
TOP_K = 2
D_EXPERT = D_MODEL // 2
MOE_BLOCK = 256
N_MOD = 6
EPS = 1e-6

kernel_name = 'hybrid_fourier_diffattn_rglru_hmoe_dit'


def _rms(x, g):
    xf = x.astype(jnp.float32)
    y = xf * lax.rsqrt(jnp.mean(xf * xf, axis=-1, keepdims=True) + EPS)
    return (y * g.astype(jnp.float32)).astype(x.dtype)


def _modulation(cond, w, b):
    m = jax.nn.silu(cond) @ w + b
    return [m[:, None, i * D_MODEL:(i + 1) * D_MODEL] for i in range(N_MOD)]


def _adaln(x, g, shift, scale):
    return _rms(x, g) * (1 + scale) + shift


def _fourier_mix(u):
    B, T, _ = u.shape
    z = u.astype(jnp.float32).reshape(B, T, F_GROUPS, F_GROUP_DIM)
    y = jnp.fft.fft2(z, axes=(1, 3), norm='ortho').real
    return y.reshape(B, T, F_WIDTH).astype(u.dtype)


def _rope_axis(x, ang):
    cos = jnp.cos(ang)[None, :, None, None, :]
    sin = jnp.sin(ang)[None, :, None, None, :]
    x1, x2 = x[..., :ROPE_FREQS], x[..., ROPE_FREQS:]
    return jnp.concatenate([x1 * cos - x2 * sin, x2 * cos + x1 * sin], axis=-1)


def _rope2d(x, ang_row, ang_col):
    y = jnp.concatenate([_rope_axis(x[..., :ROPE_HALF], ang_row),
                         _rope_axis(x[..., ROPE_HALF:], ang_col)], axis=-1)
    return y.astype(x.dtype)


def _diff_attend(q, k, v, lam):
    s = jnp.einsum('bqhmd,bkhmd->bhmqk', q, k).astype(jnp.float32) * ATT_SCALE
    p = jax.nn.softmax(s, axis=-1)
    a = p[:, :, 0] - lam * p[:, :, 1]
    return jnp.einsum('bhqk,bkhd->bqhd', a.astype(v.dtype), v)


def _diff_head_out(o, subln_g, lam_init):
    B, T = o.shape[0], o.shape[1]
    return (_rms(o, subln_g) * (1 - lam_init)).reshape(B, T, ATT_WIDTH)


def _dwconv_centred(x, w, b):
    T = x.shape[1]
    xp = jnp.pad(x, ((0, 0), (CONV_LEFT, CONV_W - 1 - CONV_LEFT), (0, 0)))
    out = b
    for k in range(CONV_W):
        out = out + xp[:, k:k + T] * w[k]
    return out


def _rglru_coeffs(xr, w_a, b_a, w_x, b_x, lam):
    B, T, _ = xr.shape
    xg = xr.reshape(B, T, LRU_BLOCKS, LRU_BLOCK_DIM)
    r = jax.nn.sigmoid(jnp.einsum('btgi,gio->btgo', xg, w_a).reshape(B, T, LRU_WIDTH) + b_a)
    i = jax.nn.sigmoid(jnp.einsum('btgi,gio->btgo', xg, w_x).reshape(B, T, LRU_WIDTH) + b_x)
    log_a = -LRU_C * r.astype(jnp.float32) * jax.nn.softplus(-lam.astype(jnp.float32))
    a = jnp.exp(log_a)
    bt = jnp.sqrt(-jnp.expm1(2.0 * log_a)) * (i * xr).astype(jnp.float32)
    return a, bt


def _scan_combine(left, right):
    a_l, b_l = left
    a_r, b_r = right
    return a_l * a_r, a_r * b_l + b_r


def _linear_scan(a, b, h0, reverse):
    if h0 is not None:
        idx = -1 if reverse else 0
        b = b.at[:, idx].add(a[:, idx] * h0)
    _, h = lax.associative_scan(_scan_combine, (a, b), reverse=reverse, axis=1)
    return h


def _mixer(hx, hc, p, lam_init, need_ctx):
    B, S, _ = hx.shape
    ux = hx @ p['w_in']
    uc = hc @ p['w_in']

    def split_qkv(u):
        T = u.shape[1]
        q = _rms(u[..., Q_OFF:K_OFF].reshape(B, T, ATT_HEADS, 2, QK_DIM), p['q_norm_g'])
        k = _rms(u[..., K_OFF:V_OFF].reshape(B, T, ATT_HEADS, 2, QK_DIM), p['k_norm_g'])
        v = u[..., V_OFF:Y_OFF].reshape(B, T, ATT_HEADS, V_DIM)
        return q, k, v

    qx, kx, vx = split_qkv(ux)
    qc, kc, vc = split_qkv(uc)
    rows_n = S // GRID_W
    row = jnp.repeat(jnp.arange(rows_n, dtype=jnp.float32), GRID_W)
    col = jnp.tile(jnp.arange(GRID_W, dtype=jnp.float32), rows_n)
    freqs = ROPE_BASE ** (-jnp.arange(ROPE_FREQS, dtype=jnp.float32) / ROPE_FREQS)
    ang_r = row[:, None] * freqs
    ang_c = col[:, None] * freqs
    qx = _rope2d(qx, ang_r, ang_c)
    kx = _rope2d(kx, ang_r, ang_c)
    lam = (jnp.exp(jnp.sum(p['lq1'] * p['lk1']).astype(jnp.float32))
           - jnp.exp(jnp.sum(p['lq2'] * p['lk2']).astype(jnp.float32)) + lam_init)
    k_all = jnp.concatenate([kc, kx], axis=1)
    v_all = jnp.concatenate([vc, vx], axis=1)
    nb = S // Q_BLOCK
    qb = qx.reshape(B, nb, Q_BLOCK, ATT_HEADS, 2, QK_DIM).transpose(1, 0, 2, 3, 4, 5)
    ob = lax.map(lambda qq: _diff_attend(qq, k_all, v_all, lam), qb)
    att_x = _diff_head_out(ob.transpose(1, 0, 2, 3, 4).reshape(B, S, ATT_HEADS, V_DIM),
                           p['subln_g'], lam_init)

    def coeffs(xr, d):
        return _rglru_coeffs(xr, p['gate_a_w'][d], p['gate_a_b'][d], p['gate_x_w'][d],
                             p['gate_x_b'][d], p['lru_lambda'][d])

    xr_c = _dwconv_centred(uc[..., R_OFF:], p['conv_w'], p['conv_b'])
    xr_x = _dwconv_centred(ux[..., R_OFF:], p['conv_w'], p['conv_b'])
    a, bt = coeffs(xr_c, 0)
    hc_f = _linear_scan(a, bt, None, False)
    a, bt = coeffs(xr_c, 1)
    hc_b = _linear_scan(a, bt, None, True)
    a, bt = coeffs(xr_x, 0)
    hx_f = _linear_scan(a, bt, hc_f[:, -1], False)
    a, bt = coeffs(xr_x, 1)
    hx_b = _linear_scan(a, bt, hc_b[:, 0], True)
    rec_x = (jax.nn.gelu(ux[..., Y_OFF:R_OFF]).astype(jnp.float32) * (hx_f + hx_b)).astype(hx.dtype)

    y_x = jnp.concatenate([_fourier_mix(ux[..., :F_WIDTH]), att_x, rec_x], axis=-1) @ p['w_out']
    if not need_ctx:
        return y_x, None
    att_c = _diff_head_out(_diff_attend(qc, kc, vc, lam), p['subln_g'], lam_init)
    rec_c = (jax.nn.gelu(uc[..., Y_OFF:R_OFF]).astype(jnp.float32) * (hc_f + hc_b)).astype(hc.dtype)
    y_c = jnp.concatenate([_fourier_mix(uc[..., :F_WIDTH]), att_c, rec_c], axis=-1) @ p['w_out']
    return y_x, y_c


def _hier_moe(xt, w_group, b_group, w_router, b_router, w1, w3, w2):
    N, D = xt.shape
    gp = jax.nn.softmax((xt @ w_group).astype(jnp.float32) + b_group, axis=-1)
    g_idx = jnp.argmax(gp, axis=-1).astype(jnp.int32)
    p_g = jnp.take_along_axis(gp, g_idx[:, None], axis=1)
    el = (xt @ w_router).astype(jnp.float32) + b_router
    cols = g_idx[:, None] * EXPERTS_PER_GROUP + jnp.arange(EXPERTS_PER_GROUP, dtype=jnp.int32)[None]
    el_g = jnp.take_along_axis(el, cols, axis=1)
    top_v, top_i = lax.top_k(el_g, TOP_K)
    wts = jax.nn.softmax(top_v, axis=-1) * p_g
    eid = g_idx[:, None] * EXPERTS_PER_GROUP + top_i.astype(jnp.int32)

    A = N * TOP_K
    eid_f = eid.reshape(A)
    tok_f = jnp.repeat(jnp.arange(N, dtype=jnp.int32), TOP_K)
    w_f = wts.reshape(A)
    order = jnp.argsort(eid_f)
    se, st, sw = eid_f[order], tok_f[order], w_f[order]
    counts = jnp.zeros((N_EXPERTS,), jnp.int32).at[eid_f].add(1)
    starts = jnp.cumsum(counts) - counts
    padded = (counts + MOE_BLOCK - 1) // MOE_BLOCK * MOE_BLOCK
    pends = jnp.cumsum(padded)
    pstarts = pends - padded
    dest = pstarts[se] + jnp.arange(A, dtype=jnp.int32) - starts[se]
    n_blocks = -(-A // MOE_BLOCK) + N_EXPERTS
    P = n_blocks * MOE_BLOCK
    slot_tok = jnp.full((P,), N, jnp.int32).at[dest].set(st)
    slot_w = jnp.zeros((P,), jnp.float32).at[dest].set(sw)
    block_exp = jnp.minimum(jnp.searchsorted(pends, jnp.arange(n_blocks, dtype=jnp.int32) * MOE_BLOCK,
                                             side='right'), N_EXPERTS - 1).astype(jnp.int32)
    xpad = jnp.concatenate([xt, jnp.zeros((1, D), xt.dtype)], axis=0)

    def body(acc, blk):
        e, toks, ws = blk
        xb = xpad[toks]
        hb = jax.nn.silu(xb @ w1[e]) * (xb @ w3[e])
        yb = (hb @ w2[e]) * ws[:, None].astype(xb.dtype)
        return acc.at[toks].add(yb), None

    acc, _ = lax.scan(body, jnp.zeros((N + 1, D), xt.dtype),
                      (block_exp, slot_tok.reshape(n_blocks, MOE_BLOCK), slot_w.reshape(n_blocks, MOE_BLOCK)))
    return acc[:N]


def setup_inputs(seed: int = 0) -> dict:
    key = jax.random.key(seed)
    ks = jax.random.split(key, 32)
    L, D = DEPTH, D_MODEL

    def nrm(k, shape, s):
        return jax.random.normal(k, shape, jnp.float32) * s

    u = jax.random.uniform(ks[22], (L, 2, LRU_WIDTH), jnp.float32, minval=0.9, maxval=0.999)
    s_lru = u ** (1.0 / LRU_C)
    return {
        'x': nrm(ks[0], (BATCH, SEQ, D), 1.0),
        'c': nrm(ks[1], (BATCH, D), 1.0),
        'ctx': nrm(ks[2], (BATCH, CTX_LEN, D), 1.0),
        'c_ctx': nrm(ks[3], (D,), 1.0),
        'w_mod': nrm(ks[4], (L, D, N_MOD * D), 0.5 * D ** -0.5),
        'b_mod': nrm(ks[5], (L, N_MOD * D), 0.01),
        'norm1_g': 1.0 + nrm(ks[6], (L, D), 0.02),
        'norm2_g': 1.0 + nrm(ks[7], (L, D), 0.02),
        'w_in': nrm(ks[8], (L, D, IN_WIDTH), D ** -0.5),
        'q_norm_g': 1.0 + nrm(ks[9], (L, QK_DIM), 0.02),
        'k_norm_g': 1.0 + nrm(ks[10], (L, QK_DIM), 0.02),
        'lambda_q1': nrm(ks[11], (L, QK_DIM), 0.1),
        'lambda_k1': nrm(ks[12], (L, QK_DIM), 0.1),
        'lambda_q2': nrm(ks[13], (L, QK_DIM), 0.1),
        'lambda_k2': nrm(ks[14], (L, QK_DIM), 0.1),
        'subln_g': 1.0 + nrm(ks[15], (L, V_DIM), 0.02),
        'conv_w': nrm(ks[16], (L, CONV_W, LRU_WIDTH), CONV_W ** -0.5),
        'conv_b': nrm(ks[17], (L, LRU_WIDTH), 0.01),
        'gate_a_w': nrm(ks[18], (L, 2, LRU_BLOCKS, LRU_BLOCK_DIM, LRU_BLOCK_DIM), LRU_BLOCK_DIM ** -0.5),
        'gate_a_b': nrm(ks[19], (L, 2, LRU_WIDTH), 0.01),
        'gate_x_w': nrm(ks[20], (L, 2, LRU_BLOCKS, LRU_BLOCK_DIM, LRU_BLOCK_DIM), LRU_BLOCK_DIM ** -0.5),
        'gate_x_b': nrm(ks[21], (L, 2, LRU_WIDTH), 0.01),
        'lru_lambda': jnp.log(s_lru) - jnp.log1p(-s_lru),
        'w_out': nrm(ks[23], (L, MIX_WIDTH, D), MIX_WIDTH ** -0.5),
        'w_group': nrm(ks[24], (L, D, N_GROUPS), D ** -0.5),
        'b_group': nrm(ks[25], (L, N_GROUPS), 0.01),
        'w_router': nrm(ks[26], (L, D, N_EXPERTS), D ** -0.5),
        'b_router': nrm(ks[27], (L, N_EXPERTS), 0.01),
        'w1': nrm(ks[28], (L, N_EXPERTS, D, D_EXPERT), D ** -0.5),
        'w3': nrm(ks[29], (L, N_EXPERTS, D, D_EXPERT), D ** -0.5),
        'w2': nrm(ks[30], (L, N_EXPERTS, D_EXPERT, D), D_EXPERT ** -0.5),
    }


def reference(x, c, ctx, c_ctx, w_mod, b_mod, norm1_g, norm2_g, w_in, q_norm_g, k_norm_g,
              lambda_q1, lambda_k1, lambda_q2, lambda_k2, subln_g, conv_w, conv_b,
              gate_a_w, gate_a_b, gate_x_w, gate_x_b, lru_lambda, w_out,
              w_group, b_group, w_router, b_router, w1, w3, w2):
    B, S, D = x.shape
    xc = ctx
    for l in range(DEPTH):
        last = l == DEPTH - 1
        lam_init = 0.8 - 0.6 * math.exp(-0.3 * l)
        sh1, sc1, g1, sh2, sc2, g2 = _modulation(c, w_mod[l], b_mod[l])
        csh1, csc1, cg1, csh2, csc2, cg2 = _modulation(c_ctx[None], w_mod[l], b_mod[l])
        p = {
            'w_in': w_in[l], 'q_norm_g': q_norm_g[l], 'k_norm_g': k_norm_g[l],
            'lq1': lambda_q1[l], 'lk1': lambda_k1[l], 'lq2': lambda_q2[l], 'lk2': lambda_k2[l],
            'subln_g': subln_g[l], 'conv_w': conv_w[l], 'conv_b': conv_b[l],
            'gate_a_w': gate_a_w[l], 'gate_a_b': gate_a_b[l], 'gate_x_w': gate_x_w[l],
            'gate_x_b': gate_x_b[l], 'lru_lambda': lru_lambda[l], 'w_out': w_out[l],
        }
        y_x, y_c = _mixer(_adaln(x, norm1_g[l], sh1, sc1), _adaln(xc, norm1_g[l], csh1, csc1),
                          p, lam_init, not last)
        x = x + g1 * y_x
        hx2 = _adaln(x, norm2_g[l], sh2, sc2).reshape(B * S, D)
        if last:
            y = _hier_moe(hx2, w_group[l], b_group[l], w_router[l], b_router[l], w1[l], w3[l], w2[l])
            x = x + g2 * y.reshape(B, S, D)
        else:
            xc = xc + cg1 * y_c
            hc2 = _adaln(xc, norm2_g[l], csh2, csc2).reshape(-1, D)
            nc = hc2.shape[0]
            y = _hier_moe(jnp.concatenate([hc2, hx2], axis=0), w_group[l], b_group[l],
                          w_router[l], b_router[l], w1[l], w3[l], w2[l])
            xc = xc + cg2 * y[:nc].reshape(xc.shape)
            x = x + g2 * y[nc:].reshape(B, S, D)
    return x
```

```python
import functools
import math

import jax
import jax.numpy as jnp
from jax import lax
from jax.experimental import pallas as pl
from jax.experimental.pallas import tpu as pltpu

F32 = jnp.float32
BF16 = jnp.bfloat16

GRID_W = 64
F_GROUPS = 4
ATT_HEADS = 4
LRU_BLOCKS = 4
LRU_C = 8.0
CONV_W = 4
CONV_LEFT = (CONV_W - 1) // 2
N_GROUPS = 4
EXPERTS_PER_GROUP = 8
N_EXPERTS = N_GROUPS * EXPERTS_PER_GROUP
TOP_K = 2
MOE_BLOCK = 256
N_MOD = 6
EPS = 1e-6
ROPE_BASE = 10000.0

LANES = 128
SUBLANES = 8
VMEM_LIMIT = 48 * 1024 * 1024
DFT_T1 = 64
ROUTE_PAD = 128


def _cparams(*sem):
    return pltpu.CompilerParams(dimension_semantics=sem, vmem_limit_bytes=VMEM_LIMIT)


def _mod_kernel(c_ref, w_ref, b_ref, o_ref):
    c = c_ref[...]
    s = c * jax.nn.sigmoid(c)
    o_ref[0] = jnp.dot(s, w_ref[0], preferred_element_type=F32,
                       precision=lax.Precision.HIGHEST) + b_ref[0]


def _modulation(c_all, w_mod, b_mod):
    L, D, n6 = w_mod.shape
    R = c_all.shape[0]
    tn = n6 // 4
    return pl.pallas_call(
        _mod_kernel,
        grid=(L, n6 // tn),
        in_specs=[pl.BlockSpec((R, D), lambda l, j: (0, 0)),
                  pl.BlockSpec((1, D, tn), lambda l, j: (l, 0, j)),
                  pl.BlockSpec((1, 1, tn), lambda l, j: (l, 0, j))],
        out_specs=pl.BlockSpec((1, R, tn), lambda l, j: (l, 0, j)),
        out_shape=jax.ShapeDtypeStruct((L, R, n6), F32),
        compiler_params=_cparams("arbitrary", "arbitrary"),
    )(c_all, w_mod, b_mod.reshape(L, 1, n6))


def _rms_mod(x, g, sc, sh):
    ms = jnp.mean(x * x, axis=-1, keepdims=True)
    return (x * lax.rsqrt(ms + EPS)) * g * (1.0 + sc) + sh


def _inproj_kernel(x_ref, sh_ref, sc_ref, g_ref, w_ref, gqk_ref, gmat_ref, cos_ref, sin_ref,
                   uf_ref, q_ref, kt_ref, v_ref, uy_ref, ur_ref, *, dims, use_rope, kb):
    fw, aw, lw, qk_dim = dims
    q_off, k_off, v_off = fw, fw + aw, fw + 2 * aw
    y_off, r_off = v_off + aw, v_off + aw + lw
    h = _rms_mod(x_ref[0], g_ref[...], sc_ref[0], sh_ref[0])
    u = jnp.dot(h.astype(BF16), w_ref[...], preferred_element_type=F32)
    uf_ref[0] = u[:, :fw].astype(BF16)
    qk = u[:, q_off:v_off]
    sq = qk * qk
    hi = sq.astype(BF16)
    lo = (sq - hi.astype(F32)).astype(BF16)
    gm = gmat_ref[...]
    gw = gm.shape[0]
    parts = []
    for s in range(2 * aw // gw):
        sl = slice(s * gw, (s + 1) * gw)
        parts.append(jnp.dot(hi[:, sl], gm, preferred_element_type=F32)
                     + jnp.dot(lo[:, sl], gm, preferred_element_type=F32))
    msq = jnp.concatenate(parts, axis=1) * (1.0 / qk_dim)
    n = qk * lax.rsqrt(msq + EPS) * gqk_ref[...]
    if use_rope:
        reps = 2 * aw // LANES
        cos = jnp.concatenate([cos_ref[...]] * reps, axis=1)
        sin = jnp.concatenate([sin_ref[...]] * reps, axis=1)
        width = n.shape[1]
        half = qk_dim // 4
        lane = lax.broadcasted_iota(jnp.int32, n.shape, 1)
        swapped = jnp.where((lane % (2 * half)) < half,
                            pltpu.roll(n, width - half, 1), pltpu.roll(n, half, 1))
        n = n * cos + swapped * sin
    q_ref[0] = (n[:, :aw] * (qk_dim ** -0.5)).astype(BF16)
    k = n[:, aw:]
    for j in range(k.shape[0] // kb):
        kt_ref[0, j] = k[j * kb:(j + 1) * kb, :].T.astype(BF16)
    v_ref[0] = u[:, v_off:y_off].astype(BF16)
    uy_ref[0] = u[:, y_off:r_off]
    ur_ref[0] = u[:, r_off:]


def _inproj(x, sh, sc, g, w_in, gqk, gmat, cos, sin, *, dims, use_rope, kb, tm):
    B, T, D = x.shape
    fw, aw, lw, _ = dims
    n_in = w_in.shape[1]
    per_b = lambda b, i: (b, 0, 0)
    const = lambda b, i: (0, 0)
    tile = lambda b, i: (b, i, 0)
    return pl.pallas_call(
        functools.partial(_inproj_kernel, dims=dims, use_rope=use_rope, kb=kb),
        grid=(B, T // tm),
        in_specs=[pl.BlockSpec((1, tm, D), tile),
                  pl.BlockSpec((1, 1, D), per_b), pl.BlockSpec((1, 1, D), per_b),
                  pl.BlockSpec((1, D), const),
                  pl.BlockSpec((D, n_in), const),
                  pl.BlockSpec((1, 2 * aw), const),
                  pl.BlockSpec(gmat.shape, const),
                  pl.BlockSpec((tm, LANES), lambda b, i: (i, 0)),
                  pl.BlockSpec((tm, LANES), lambda b, i: (i, 0))],
        out_specs=[pl.BlockSpec((1, tm, fw), tile),
                   pl.BlockSpec((1, tm, aw), tile),
                   pl.BlockSpec((1, tm // kb, aw, kb), lambda b, i: (b, i, 0, 0)),
                   pl.BlockSpec((1, tm, aw), tile),
                   pl.BlockSpec((1, tm, lw), tile),
                   pl.BlockSpec((1, tm, lw), tile)],
        out_shape=[jax.ShapeDtypeStruct((B, T, fw), BF16),
                   jax.ShapeDtypeStruct((B, T, aw), BF16),
                   jax.ShapeDtypeStruct((B, T // kb, aw, kb), BF16),
                   jax.ShapeDtypeStruct((B, T, aw), BF16),
                   jax.ShapeDtypeStruct((B, T, lw), F32),
                   jax.ShapeDtypeStruct((B, T, lw), F32)],
        compiler_params=_cparams("arbitrary", "arbitrary"),
    )(x, sh, sc, g, w_in, gqk, gmat, cos, sin)


def _attn_kernel(lam_ref, q_ref, kt_ref, v_ref, g_ref, o_ref, *, nkb, out_scale):
    q = q_ref[0].astype(F32)
    tq, w = q.shape
    lane = lax.broadcasted_iota(jnp.int32, q.shape, 1)
    qq = jnp.concatenate([jnp.where(lane < w // 2, q, 0.0),
                          jnp.where(lane >= w // 2, q, 0.0)], axis=0).astype(BF16)

    def body(j, carry):
        m, l, acc = carry
        s = jnp.dot(qq, kt_ref[0, j], preferred_element_type=F32)
        m_new = jnp.maximum(m, jnp.max(s, axis=-1, keepdims=True))
        alpha = jnp.exp(m - m_new)
        p = jnp.exp(s - m_new)
        l = alpha * l + jnp.sum(p, axis=-1, keepdims=True)
        acc = alpha * acc + jnp.dot(p.astype(BF16), v_ref[0, j], preferred_element_type=F32)
        return m_new, l, acc

    vd = v_ref.shape[-1]
    init = (jnp.full((2 * tq, 1), -jnp.inf, F32), jnp.zeros((2 * tq, 1), F32),
            jnp.zeros((2 * tq, vd), F32))
    _, l, acc = lax.fori_loop(0, nkb, body, init)
    o = acc / l
    d = o[:tq] - lam_ref[0] * o[tq:]
    ms = jnp.mean(d * d, axis=-1, keepdims=True)
    o_ref[0] = (d * lax.rsqrt(ms + EPS) * g_ref[...] * out_scale).astype(BF16)


def _attention(lam, q, kt, v, subln_g, *, out_scale, tq):
    B, S, aw = q.shape
    nkb, kb = kt.shape[1], kt.shape[3]
    hd = aw // ATT_HEADS
    v4 = v.reshape(B, nkb, kb, aw)
    return pl.pallas_call(
        functools.partial(_attn_kernel, nkb=nkb, out_scale=out_scale),
        grid=(B, ATT_HEADS, S // tq),
        in_specs=[pl.BlockSpec(memory_space=pltpu.SMEM),
                  pl.BlockSpec((1, tq, hd), lambda b, h, i: (b, i, h)),
                  pl.BlockSpec((1, nkb, hd, kb), lambda b, h, i: (b, 0, h, 0)),
                  pl.BlockSpec((1, nkb, kb, hd), lambda b, h, i: (b, 0, 0, h)),
                  pl.BlockSpec((1, hd), lambda b, h, i: (0, 0))],
        out_specs=pl.BlockSpec((1, tq, hd), lambda b, h, i: (b, i, h)),
        out_shape=jax.ShapeDtypeStruct((B, S, aw), BF16),
        compiler_params=_cparams("arbitrary", "arbitrary", "arbitrary"),
    )(lam, q, kt, v4, subln_g)


def _dft1_kernel(m_ref, z_ref, y_ref):
    y_ref[0] = jnp.dot(m_ref[...], z_ref[0], preferred_element_type=F32).astype(BF16)


def _dft2_kernel(y_ref, tab_ref, cc_ref, sc_ref, o_ref, *, scale):
    y = jnp.concatenate([y_ref[0, 0, 0], y_ref[0, 1, 0]], axis=0)
    zr = jnp.dot(tab_ref[0, 0], y, preferred_element_type=F32)
    zi = jnp.dot(tab_ref[0, 1], y, preferred_element_type=F32)
    o = (jnp.dot(zr.astype(BF16), cc_ref[...], preferred_element_type=F32)
         + jnp.dot(zi.astype(BF16), sc_ref[...], preferred_element_type=F32))
    o_ref[0] = (o * scale).astype(BF16)


def _fourier_tables(T, fw):
    t1n, t2n = DFT_T1, T // DFT_T1
    gd = fw // F_GROUPS
    two_pi = 2.0 * math.pi
    k1 = jnp.arange(t1n, dtype=F32)
    a1 = two_pi * jnp.mod(k1[:, None] * k1[None, :], t1n) / t1n
    m1 = jnp.concatenate([jnp.cos(a1), -jnp.sin(a1)], axis=0)
    k2 = jnp.arange(t2n, dtype=F32)
    kk = k1[:, None, None] + t1n * k2[None, :, None]
    ph = two_pi * jnp.mod(kk * k2[None, None, :], T) / T
    cp, sp = jnp.cos(ph), jnp.sin(ph)
    tab = jnp.stack([jnp.concatenate([cp, sp], axis=-1),
                     jnp.concatenate([-sp, cp], axis=-1)], axis=1)
    c = jnp.arange(gd, dtype=F32)
    ac = two_pi * jnp.mod(c[:, None] * c[None, :], gd) / gd
    eye = jnp.eye(F_GROUPS, dtype=F32)
    cc = jnp.kron(eye, jnp.cos(ac))
    sc = jnp.kron(eye, jnp.sin(ac))
    return m1.astype(BF16), tab.astype(BF16), cc.astype(BF16), sc.astype(BF16)


def _fourier_long(uf, tables):
    B, T, W = uf.shape
    m1, tab, cc, sc = tables
    t1n, t2n = DFT_T1, T // DFT_T1
    ncol = t2n * W
    tn = min(ncol, 4096)
    y = pl.pallas_call(
        _dft1_kernel,
        grid=(B, ncol // tn),
        in_specs=[pl.BlockSpec((2 * t1n, t1n), lambda b, j: (0, 0)),
                  pl.BlockSpec((1, t1n, tn), lambda b, j: (b, 0, j))],
        out_specs=pl.BlockSpec((1, 2 * t1n, tn), lambda b, j: (b, 0, j)),
        out_shape=jax.ShapeDtypeStruct((B, 2 * t1n, ncol), BF16),
        compiler_params=_cparams("arbitrary", "arbitrary"),
    )(m1, uf.reshape(B, t1n, ncol))
    y5 = y.reshape(B, 2, t1n, t2n, W)
    scale = 1.0 / math.sqrt(T * (W // F_GROUPS))
    out = pl.pallas_call(
        functools.partial(_dft2_kernel, scale=scale),
        grid=(t1n, B),
        in_specs=[pl.BlockSpec((1, 2, 1, t2n, W), lambda k, b: (b, 0, k, 0, 0)),
                  pl.BlockSpec((1, 2, t2n, 2 * t2n), lambda k, b: (k, 0, 0, 0)),
                  pl.BlockSpec((W, W), lambda k, b: (0, 0)),
                  pl.BlockSpec((W, W), lambda k, b: (0, 0))],
        out_specs=pl.BlockSpec((1, t2n, W), lambda k, b: (b, 0, k)),
        out_shape=jax.ShapeDtypeStruct((B, t2n, t1n * W), BF16),
        compiler_params=_cparams("arbitrary", "arbitrary"),
    )(y5, tab, cc, sc)
    return out.reshape(B, T, W)


def _dft_short_kernel(z_ref, ct_ref, st_ref, cc_ref, sc_ref, o_ref, *, scale):
    z = z_ref[0]
    zc = jnp.dot(z, cc_ref[...], preferred_element_type=F32).astype(BF16)
    zs = jnp.dot(z, sc_ref[...], preferred_element_type=F32).astype(BF16)
    o = (jnp.dot(ct_ref[...], zc, preferred_element_type=F32)
         - jnp.dot(st_ref[...], zs, preferred_element_type=F32))
    o_ref[0] = (o * scale).astype(BF16)


def _fourier_short(uf, cc, sc):
    B, T, W = uf.shape
    t = jnp.arange(T, dtype=F32)
    ang = 2.0 * math.pi * jnp.mod(t[:, None] * t[None, :], T) / T
    ct, st = jnp.cos(ang).astype(BF16), jnp.sin(ang).astype(BF16)
    scale = 1.0 / math.sqrt(T * (W // F_GROUPS))
    return pl.pallas_call(
        functools.partial(_dft_short_kernel, scale=scale),
        grid=(B,),
        in_specs=[pl.BlockSpec((1, T, W), lambda b: (b, 0, 0)),
                  pl.BlockSpec((T, T), lambda b: (0, 0)), pl.BlockSpec((T, T), lambda b: (0, 0)),
                  pl.BlockSpec((W, W), lambda b: (0, 0)), pl.BlockSpec((W, W), lambda b: (0, 0))],
        out_specs=pl.BlockSpec((1, T, W), lambda b: (b, 0, 0)),
        out_shape=jax.ShapeDtypeStruct((B, T, W), BF16),
        compiler_params=_cparams("arbitrary"),
    )(uf, ct, st, cc, sc)


def _affine_scan(a, b, reverse):
    T = a.shape[0]
    row = lax.broadcasted_iota(jnp.int32, a.shape, 0)
    k = 1
    while k < T:
        if reverse:
            a_s, b_s, valid = pltpu.roll(a, T - k, 0), pltpu.roll(b, T - k, 0), row < T - k
        else:
            a_s, b_s, valid = pltpu.roll(a, k, 0), pltpu.roll(b, k, 0), row >= k
        b = a * jnp.where(valid, b_s, 0.0) + b
        a = a * jnp.where(valid, a_s, 1.0)
        k *= 2
    return a, b


def _lru_kernel(*refs, reverse, combine, nt):
    if combine:
        (ur_ref, prev_ref, next_ref, h0_ref, cw_ref, cb_ref, wa_ref, ba_ref, wx_ref, bx_ref,
         nsp_ref, hf_ref, uy_ref, out_ref, hlast_ref, carry_ref) = refs
    else:
        (ur_ref, prev_ref, next_ref, h0_ref, cw_ref, cb_ref, wa_ref, ba_ref, wx_ref, bx_ref,
         nsp_ref, out_ref, hlast_ref, carry_ref) = refs
    i = pl.program_id(1)
    ci = nt - 1 - i if reverse else i

    @pl.when(i == 0)
    def _():
        carry_ref[...] = h0_ref[0]

    u = ur_ref[0]
    tc = u.shape[0]
    prev = jnp.where(ci == 0, 0.0, prev_ref[0])
    nxt = jnp.where(ci == nt - 1, 0.0, next_ref[0])
    ext = jnp.concatenate([prev, u, nxt], axis=0)
    n_ext = ext.shape[0]
    xr = cb_ref[...] + u * cw_ref[CONV_LEFT:CONV_LEFT + 1, :]
    for k in range(CONV_W):
        d = k - CONV_LEFT
        if d != 0:
            shifted = pltpu.roll(ext, (-d) % n_ext, 0)[SUBLANES:SUBLANES + tc]
            xr = xr + shifted * cw_ref[k:k + 1, :]
    xb = xr.astype(BF16)
    r = jax.nn.sigmoid(jnp.dot(xb, wa_ref[...], preferred_element_type=F32) + ba_ref[...])
    g = jax.nn.sigmoid(jnp.dot(xb, wx_ref[...], preferred_element_type=F32) + bx_ref[...])
    log_a = r * nsp_ref[...]
    a = jnp.exp(log_a)
    bt = jnp.sqrt(-jnp.tanh(log_a) * (a * a + 1.0)) * (g * xr)
    a_cum, b_cum = _affine_scan(a, bt, reverse)
    h = a_cum * carry_ref[...] + b_cum
    last = h[0:1] if reverse else h[tc - 1:tc]
    carry_ref[...] = last
    hlast_ref[0] = last
    if combine:
        out_ref[0] = (jax.nn.gelu(uy_ref[0]) * (hf_ref[0] + h)).astype(BF16)
    else:
        out_ref[0] = h


def _lru_scan(ur, h0, p, d, *, reverse, hf=None, uy=None, tc):
    B, T, W = ur.shape
    nt = T // tc
    hb = tc // SUBLANES
    nh = T // SUBLANES
    combine = hf is not None
    cidx = (lambda i: nt - 1 - i) if reverse else (lambda i: i)
    tile = lambda b, i: (b, cidx(i), 0)
    const = lambda b, i: (0, 0)
    in_specs = [pl.BlockSpec((1, tc, W), tile),
                pl.BlockSpec((1, SUBLANES, W), lambda b, i: (b, jnp.maximum(cidx(i) * hb - 1, 0), 0)),
                pl.BlockSpec((1, SUBLANES, W), lambda b, i: (b, jnp.minimum((cidx(i) + 1) * hb, nh - 1), 0)),
                pl.BlockSpec((1, 1, W), lambda b, i: (b, 0, 0)),
                pl.BlockSpec((CONV_W, W), const), pl.BlockSpec((1, W), const),
                pl.BlockSpec((W, W), const), pl.BlockSpec((1, W), const),
                pl.BlockSpec((W, W), const), pl.BlockSpec((1, W), const),
                pl.BlockSpec((1, W), const)]
    args = [ur, ur, ur, h0.reshape(B, 1, W), p['conv_w'], p['conv_b'],
            p['wa'][d], p['ba'][d], p['wx'][d], p['bx'][d], p['nsp'][d]]
    if combine:
        in_specs += [pl.BlockSpec((1, tc, W), tile), pl.BlockSpec((1, tc, W), tile)]
        args += [hf, uy]
    out, hlast = pl.pallas_call(
        functools.partial(_lru_kernel, reverse=reverse, combine=combine, nt=nt),
        grid=(B, nt),
        in_specs=in_specs,
        out_specs=[pl.BlockSpec((1, tc, W), tile), pl.BlockSpec((1, 1, W), lambda b, i: (b, 0, 0))],
        out_shape=[jax.ShapeDtypeStruct((B, T, W), BF16 if combine else F32),
                   jax.ShapeDtypeStruct((B, 1, W), F32)],
        scratch_shapes=[pltpu.VMEM((1, W), F32)],
        compiler_params=_cparams("arbitrary", "arbitrary"),
    )(*args)
    return out, hlast.reshape(B, W)


def _outproj_kernel(f_ref, a_ref, r_ref, x_ref, g1_ref, sh_ref, sc_ref, ng_ref, w_ref, wr_ref, br_ref,
                    x1_ref, h2_ref, lg_ref, *, fw, aw):
    y = (jnp.dot(f_ref[0], w_ref[0:fw, :], preferred_element_type=F32)
         + jnp.dot(a_ref[0], w_ref[fw:fw + aw, :], preferred_element_type=F32)
         + jnp.dot(r_ref[0], w_ref[fw + aw:, :], preferred_element_type=F32))
    x1 = x_ref[0] + g1_ref[0] * y
    x1_ref[0] = x1
    h2 = _rms_mod(x1, ng_ref[...], sc_ref[0], sh_ref[0])
    h2_ref[0] = h2
    lg_ref[0] = jnp.dot(h2.astype(BF16), wr_ref[...], preferred_element_type=F32) + br_ref[...]


def _outproj(four, att, rec, x, g1, sh2, sc2, ng, w_out, w_rt, b_rt, *, tm):
    B, T, D = x.shape
    fw, aw = four.shape[2], att.shape[2]
    tile = lambda b, i: (b, i, 0)
    per_b = lambda b, i: (b, 0, 0)
    const = lambda b, i: (0, 0)
    x1, h2, logits = pl.pallas_call(
        functools.partial(_outproj_kernel, fw=fw, aw=aw),
        grid=(B, T // tm),
        in_specs=[pl.BlockSpec((1, tm, fw), tile), pl.BlockSpec((1, tm, aw), tile),
                  pl.BlockSpec((1, tm, rec.shape[2]), tile), pl.BlockSpec((1, tm, D), tile),
                  pl.BlockSpec((1, 1, D), per_b), pl.BlockSpec((1, 1, D), per_b),
                  pl.BlockSpec((1, 1, D), per_b), pl.BlockSpec((1, D), const),
                  pl.BlockSpec((D, D), const), pl.BlockSpec((D, ROUTE_PAD), const),
                  pl.BlockSpec((1, ROUTE_PAD), const)],
        out_specs=[pl.BlockSpec((1, tm, D), tile), pl.BlockSpec((1, tm, D), tile),
                   pl.BlockSpec((1, tm, ROUTE_PAD), tile)],
        out_shape=[jax.ShapeDtypeStruct((B, T, D), F32),
                   jax.ShapeDtypeStruct((B, T, D), F32),
                   jax.ShapeDtypeStruct((B, T, ROUTE_PAD), F32)],
        compiler_params=_cparams("arbitrary", "arbitrary"),
    )(four, att, rec, x, g1, sh2, sc2, ng, w_out, w_rt, b_rt)
    return x1, h2.reshape(B * T, D), logits.reshape(B * T, ROUTE_PAD)


def _gather_rows_kernel(idx_ref, src_ref, o_ref, sem):
    rows = o_ref.shape[0]

    def issue(r, c):
        tok = idx_ref[0, 0, r]
        pltpu.make_async_copy(src_ref.at[pl.ds(tok, 1)], o_ref.at[pl.ds(r, 1)], sem).start()
        return c

    lax.fori_loop(0, rows, issue, 0, unroll=8)
    pltpu.make_async_copy(src_ref.at[pl.ds(0, rows)], o_ref, sem).wait()


def _gather_rows(src, slot_tok):
    P = slot_tok.shape[0]
    D = src.shape[1]
    R = MOE_BLOCK
    return pl.pallas_call(
        _gather_rows_kernel,
        grid=(P // R,),
        in_specs=[pl.BlockSpec((1, 1, R), lambda i: (i, 0, 0), memory_space=pltpu.SMEM),
                  pl.BlockSpec(memory_space=pl.ANY)],
        out_specs=pl.BlockSpec((R, D), lambda i: (i, 0)),
        out_shape=jax.ShapeDtypeStruct((P, D), F32),
        scratch_shapes=[pltpu.SemaphoreType.DMA(())],
        compiler_params=_cparams("arbitrary"),
    )(slot_tok.reshape(P // R, 1, R), src)


def _expert_kernel(be_ref, nu_ref, x_ref, ws_ref, w13_ref, w2_ref, o_ref):
    i = pl.program_id(0)
    de = w2_ref.shape[1]

    @pl.when(i < nu_ref[0])
    def _():
        h = jnp.dot(x_ref[...].astype(BF16), w13_ref[0], preferred_element_type=F32)
        hb = jax.nn.silu(h[:, :de]) * h[:, de:]
        y = jnp.dot(hb.astype(BF16), w2_ref[0], preferred_element_type=F32)
        o_ref[...] = y * ws_ref[...]

    @pl.when(i >= nu_ref[0])
    def _():
        o_ref[...] = jnp.zeros_like(o_ref)


def _experts(xs, slot_w, block_exp, n_used, w13, w2):
    P, D = xs.shape
    R = MOE_BLOCK
    de = w2.shape[1]
    clamp = lambda i, be, nu: (jnp.minimum(i, nu[0] - 1), 0)
    return pl.pallas_call(
        _expert_kernel,
        grid_spec=pltpu.PrefetchScalarGridSpec(
            num_scalar_prefetch=2,
            grid=(P // R,),
            in_specs=[pl.BlockSpec((R, D), clamp),
                      pl.BlockSpec((R, 1), clamp),
                      pl.BlockSpec((1, D, 2 * de), lambda i, be, nu: (be[i], 0, 0)),
                      pl.BlockSpec((1, de, D), lambda i, be, nu: (be[i], 0, 0))],
            out_specs=pl.BlockSpec((R, D), lambda i, be, nu: (i, 0))),
        out_shape=jax.ShapeDtypeStruct((P, D), F32),
        compiler_params=_cparams("arbitrary"),
    )(block_exp, n_used, xs, slot_w.reshape(P, 1), w13, w2)


def _combine_kernel(pos_ref, x_ref, g_ref, ys_ref, o_ref, buf, sem):
    tm = x_ref.shape[1]

    def issue(t, c):
        for k in range(TOP_K):
            pltpu.make_async_copy(ys_ref.at[pl.ds(pos_ref[0, 0, TOP_K * t + k], 1)],
                                  buf.at[k, pl.ds(t, 1)], sem).start()
        return c

    lax.fori_loop(0, tm, issue, 0, unroll=4)
    for k in range(TOP_K):
        pltpu.make_async_copy(ys_ref.at[pl.ds(0, tm)], buf.at[k], sem).wait()
    y = buf[0]
    for k in range(1, TOP_K):
        y = y + buf[k]
    o_ref[0] = x_ref[0] + g_ref[0] * y


def _combine(x1, g2, ys, pos, *, row_off, tm):
    B, T, D = x1.shape
    nt = T // tm
    off = row_off // tm
    pos3 = pos.reshape(-1, 1, TOP_K * tm)
    return pl.pallas_call(
        _combine_kernel,
        grid=(B, nt),
        in_specs=[pl.BlockSpec((1, 1, TOP_K * tm), lambda b, i: (b * nt + i + off, 0, 0),
                               memory_space=pltpu.SMEM),
                  pl.BlockSpec((1, tm, D), lambda b, i: (b, i, 0)),
                  pl.BlockSpec((1, 1, D), lambda b, i: (b, 0, 0)),
                  pl.BlockSpec(memory_space=pl.ANY)],
        out_specs=pl.BlockSpec((1, tm, D), lambda b, i: (b, i, 0)),
        out_shape=jax.ShapeDtypeStruct((B, T, D), F32),
        scratch_shapes=[pltpu.VMEM((TOP_K, tm, D), F32), pltpu.SemaphoreType.DMA(())],
        compiler_params=_cparams("arbitrary", "arbitrary"),
    )(pos3, x1, g2, ys)


def _route(logits):
    gp = jax.nn.softmax(logits[:, :N_GROUPS], axis=-1)
    g_idx = jnp.argmax(gp, axis=-1).astype(jnp.int32)
    p_g = jnp.take_along_axis(gp, g_idx[:, None], axis=1)
    el = logits[:, N_GROUPS:N_GROUPS + N_EXPERTS]
    cols = g_idx[:, None] * EXPERTS_PER_GROUP + jnp.arange(EXPERTS_PER_GROUP, dtype=jnp.int32)[None]
    el_g = jnp.take_along_axis(el, cols, axis=1)
    top_v, top_i = lax.top_k(el_g, TOP_K)
    wts = jax.nn.softmax(top_v, axis=-1) * p_g
    eid = g_idx[:, None] * EXPERTS_PER_GROUP + top_i.astype(jnp.int32)
    return eid, wts


def _dispatch_plan(eid, wts):
    N = eid.shape[0]
    A = N * TOP_K
    eid_f = eid.reshape(A)
    onehot = (eid_f[:, None] == jnp.arange(N_EXPERTS, dtype=jnp.int32)[None]).astype(jnp.int32)
    csum = jnp.cumsum(onehot, axis=0)
    counts = csum[-1]
    rank = jnp.take_along_axis(csum, eid_f[:, None], axis=1)[:, 0] - 1
    padded = (counts + MOE_BLOCK - 1) // MOE_BLOCK * MOE_BLOCK
    pends = jnp.cumsum(padded)
    pstarts = pends - padded
    dest = pstarts[eid_f] + rank
    n_blocks = -(-A // MOE_BLOCK) + N_EXPERTS
    P = n_blocks * MOE_BLOCK
    tok_f = jnp.repeat(jnp.arange(N, dtype=jnp.int32), TOP_K)
    slot_tok = jnp.zeros((P,), jnp.int32).at[dest].set(tok_f)
    slot_w = jnp.zeros((P,), F32).at[dest].set(wts.reshape(A))
    block_exp = jnp.minimum(
        jnp.searchsorted(pends, jnp.arange(n_blocks, dtype=jnp.int32) * MOE_BLOCK, side='right'),
        N_EXPERTS - 1).astype(jnp.int32)
    n_used = (pends[-1] // MOE_BLOCK).astype(jnp.int32).reshape(1)
    return slot_tok, slot_w, block_exp, n_used, dest.reshape(N, TOP_K)


def _moe(h2, logits, w13, w2):
    eid, wts = _route(logits)
    slot_tok, slot_w, block_exp, n_used, pos = _dispatch_plan(eid, wts)
    xs = _gather_rows(h2, slot_tok)
    ys = _experts(xs, slot_w, block_exp, n_used, w13, w2)
    return ys, pos


def _blockdiag(w):
    G, n, _ = w.shape
    eye = jnp.eye(G, dtype=w.dtype)
    return (eye[:, None, :, None] * w[:, :, None, :]).reshape(G * n, G * n)


def _rope_tables(S, qk_dim):
    half = qk_dim // 2
    nf = half // 2
    rows_n = S // GRID_W
    row = jnp.repeat(jnp.arange(rows_n, dtype=F32), GRID_W)
    col = jnp.tile(jnp.arange(GRID_W, dtype=F32), rows_n)
    freqs = ROPE_BASE ** (-jnp.arange(nf, dtype=F32) / nf)
    ang_r = row[:, None] * freqs
    ang_c = col[:, None] * freqs
    cos = jnp.concatenate([jnp.cos(ang_r)] * 2 + [jnp.cos(ang_c)] * 2, axis=1)
    sin = jnp.concatenate([-jnp.sin(ang_r), jnp.sin(ang_r), -jnp.sin(ang_c), jnp.sin(ang_c)], axis=1)
    reps = LANES // qk_dim
    return jnp.tile(cos, (1, reps)), jnp.tile(sin, (1, reps))


def kernel(x, c, ctx, c_ctx, w_mod, b_mod, norm1_g, norm2_g, w_in, q_norm_g, k_norm_g, lambda_q1, lambda_k1, lambda_q2, lambda_k2, subln_g, conv_w, conv_b, gate_a_w, gate_a_b, gate_x_w, gate_x_b, lru_lambda, w_out, w_group, b_group, w_router, b_router, w1, w3, w2):
    B, S, D = x.shape
    C = ctx.shape[1]
    L = w_mod.shape[0]
    qk_dim = q_norm_g.shape[1]
    fw = lw = D // 4
    aw = D // 2
    dims = (fw, aw, lw, qk_dim)
    kb = min(256, C)
    tm_x, tm_c = min(512, S), min(512, C)
    n_ctx, n_lat = B * C, B * S

    n_rows = -(-(B + 1) // SUBLANES) * SUBLANES
    c_all = jnp.zeros((n_rows, D), F32).at[:B].set(c).at[B].set(c_ctx)
    mod = _modulation(c_all, w_mod, b_mod)

    cos_t, sin_t = _rope_tables(S, qk_dim)
    dummy_tab = jnp.zeros((C, LANES), F32)
    four_tabs = _fourier_tables(S, fw)
    gmat = _blockdiag(jnp.ones((256 // qk_dim, qk_dim, qk_dim), F32)).astype(BF16)

    xc = ctx
    for l in range(L):
        last = l == L - 1
        lam_init = 0.8 - 0.6 * math.exp(-0.3 * l)
        m = [mod[l, :, i * D:(i + 1) * D] for i in range(N_MOD)]
        mx = [a[:B, None, :] for a in m]
        mc = [jnp.broadcast_to(a[B][None, None, :], (B, 1, D)) for a in m]
        w_in_b = w_in[l].astype(BF16)
        w_out_b = w_out[l].astype(BF16)
        gqk = jnp.concatenate([jnp.tile(q_norm_g[l], aw // qk_dim),
                               jnp.tile(k_norm_g[l], aw // qk_dim)])[None, :]
        lam = (jnp.exp(jnp.sum(lambda_q1[l] * lambda_k1[l])) - jnp.exp(jnp.sum(lambda_q2[l] * lambda_k2[l]))
               + lam_init).astype(F32).reshape(1)
        sub_g = subln_g[l][None, :]
        n1 = norm1_g[l][None, :]
        n2 = norm2_g[l][None, :]
        lru_p = {
            'conv_w': conv_w[l], 'conv_b': conv_b[l][None, :],
            'wa': [_blockdiag(gate_a_w[l, d]).astype(BF16) for d in range(2)],
            'wx': [_blockdiag(gate_x_w[l, d]).astype(BF16) for d in range(2)],
            'ba': [gate_a_b[l, d][None, :] for d in range(2)],
            'bx': [gate_x_b[l, d][None, :] for d in range(2)],
            'nsp': [(-LRU_C * jax.nn.softplus(-lru_lambda[l, d]))[None, :] for d in range(2)],
        }
        w_rt = jnp.zeros((D, ROUTE_PAD), F32).at[:, :N_GROUPS].set(w_group[l]) \
            .at[:, N_GROUPS:N_GROUPS + N_EXPERTS].set(w_router[l]).astype(BF16)
        b_rt = jnp.zeros((1, ROUTE_PAD), F32).at[0, :N_GROUPS].set(b_group[l]) \
            .at[0, N_GROUPS:N_GROUPS + N_EXPERTS].set(b_router[l])
        w13 = jnp.concatenate([w1[l], w3[l]], axis=-1).astype(BF16)
        w2_b = w2[l].astype(BF16)

        ufc, qc, ktc, vc, uyc, urc = _inproj(xc, mc[0], mc[1], n1, w_in_b, gqk, gmat, dummy_tab, dummy_tab,
                                             dims=dims, use_rope=False, kb=kb, tm=tm_c)
        ufx, qx, ktx, vx, uyx, urx = _inproj(x, mx[0], mx[1], n1, w_in_b, gqk, gmat, cos_t, sin_t,
                                             dims=dims, use_rope=True, kb=kb, tm=tm_x)
        kt_all = jnp.concatenate([ktc, ktx], axis=1)
        v_all = jnp.concatenate([vc, vx], axis=1)
        att_x = _attention(lam, qx, kt_all, v_all, sub_g, out_scale=1.0 - lam_init, tq=min(256, S))

        zeros_h = jnp.zeros((B, lw), F32)
        tc_c, tc_x = min(512, C), min(512, S)
        hc_f, hc_f_last = _lru_scan(urc, zeros_h, lru_p, 0, reverse=False, tc=tc_c)
        hx_f, _ = _lru_scan(urx, hc_f_last, lru_p, 0, reverse=False, tc=tc_x)
        rec_c, hc_b_first = _lru_scan(urc, zeros_h, lru_p, 1, reverse=True, hf=hc_f, uy=uyc, tc=tc_c)
        rec_x, _ = _lru_scan(urx, hc_b_first, lru_p, 1, reverse=True, hf=hx_f, uy=uyx, tc=tc_x)

        four_x = _fourier_long(ufx, four_tabs)

        if last:
            x1, h2, logits = _outproj(four_x, att_x, rec_x, x, mx[2], mx[3], mx[4], n2, w_out_b, w_rt, b_rt, tm=tm_x)
            ys, pos = _moe(h2, logits, w13, w2_b)
            x = _combine(x1, mx[5], ys, pos, row_off=0, tm=min(256, S))
        else:
            att_c = _attention(lam, qc, ktc, vc, sub_g, out_scale=1.0 - lam_init, tq=min(256, C))
            four_c = _fourier_short(ufc, four_tabs[2], four_tabs[3])
            xc1, h2c, lgc = _outproj(four_c, att_c, rec_c, xc, mc[2], mc[3], mc[4], n2, w_out_b, w_rt, b_rt, tm=tm_c)
            x1, h2x, lgx = _outproj(four_x, att_x, rec_x, x, mx[2], mx[3], mx[4], n2, w_out_b, w_rt, b_rt, tm=tm_x)
            ys, pos = _moe(jnp.concatenate([h2c, h2x], axis=0), jnp.concatenate([lgc, lgx], axis=0), w13, w2_b)
            tmc = min(256, S, C)
            xc = _combine(xc1, mc[5], ys, pos, row_off=0, tm=tmc)
            x = _combine(x1, mx[5], ys, pos, row_off=n_ctx, tm=tmc)
    return x
```

```python
import functools
import math

import jax
import jax.numpy as jnp
from jax import lax
from jax.experimental import pallas as pl
from jax.experimental.pallas import tpu as pltpu

F32 = jnp.float32
BF16 = jnp.bfloat16

GRID_W = 64
F_GROUPS = 4
ATT_HEADS = 4
LRU_BLOCKS = 4
LRU_C = 8.0
CONV_W = 4
CONV_LEFT = (CONV_W - 1) // 2
N_GROUPS = 4
EXPERTS_PER_GROUP = 8
N_EXPERTS = N_GROUPS * EXPERTS_PER_GROUP
TOP_K = 2
MOE_BLOCK = 256
N_MOD = 6
EPS = 1e-6
ROPE_BASE = 10000.0
LOG2E = math.log2(math.e)

LANES = 128
SUBLANES = 8
VMEM_LIMIT = 48 * 1024 * 1024
DFT_T1 = 64
ROUTE_PAD = 128


def _cparams(*sem):
    return pltpu.CompilerParams(dimension_semantics=sem, vmem_limit_bytes=VMEM_LIMIT)


def _mod_kernel(c_ref, w_ref, b_ref, o_ref):
    c = c_ref[...]
    s = c * jax.nn.sigmoid(c)
    o_ref[0] = jnp.dot(s, w_ref[0], preferred_element_type=F32,
                       precision=lax.Precision.HIGHEST) + b_ref[0]


def _modulation(c_all, w_mod, b_mod):
    L, D, n6 = w_mod.shape
    R = c_all.shape[0]
    tn = n6 // 4
    return pl.pallas_call(
        _mod_kernel,
        grid=(L, n6 // tn),
        in_specs=[pl.BlockSpec((R, D), lambda l, j: (0, 0)),
                  pl.BlockSpec((1, D, tn), lambda l, j: (l, 0, j)),
                  pl.BlockSpec((1, 1, tn), lambda l, j: (l, 0, j))],
        out_specs=pl.BlockSpec((1, R, tn), lambda l, j: (l, 0, j)),
        out_shape=jax.ShapeDtypeStruct((L, R, n6), F32),
        compiler_params=_cparams("arbitrary", "arbitrary"),
        name="modulation",
    )(c_all, w_mod, b_mod.reshape(L, 1, n6))


def _rms_mod(x, g, sc, sh):
    ms = jnp.mean(x * x, axis=-1, keepdims=True)
    return (x * lax.rsqrt(ms + EPS)) * g * (1.0 + sc) + sh


def _inproj_kernel(x_ref, sh_ref, sc_ref, g_ref, w_ref, gqk_ref, gmat_ref, cos_ref, sin_ref,
                   uf_ref, q_ref, kt_ref, v_ref, uy_ref, ur_ref, *, dims, use_rope):
    fw, aw, lw, qk_dim = dims
    q_off, k_off, v_off = fw, fw + aw, fw + 2 * aw
    y_off, r_off = v_off + aw, v_off + aw + lw
    h = _rms_mod(x_ref[0], g_ref[...], sc_ref[0], sh_ref[0])
    u = jnp.dot(h.astype(BF16), w_ref[...], preferred_element_type=F32)
    uf_ref[0] = u[:, :fw].astype(BF16)
    qk = u[:, q_off:v_off]
    sq = qk * qk
    hi = sq.astype(BF16)
    lo = (sq - hi.astype(F32)).astype(BF16)
    gm = gmat_ref[...]
    gw = gm.shape[0]
    parts = []
    for s in range(2 * aw // gw):
        sl = slice(s * gw, (s + 1) * gw)
        parts.append(jnp.dot(hi[:, sl], gm, preferred_element_type=F32)
                     + jnp.dot(lo[:, sl], gm, preferred_element_type=F32))
    msq = jnp.concatenate(parts, axis=1) * (1.0 / qk_dim)
    n = qk * lax.rsqrt(msq + EPS) * gqk_ref[...]
    if use_rope:
        reps = 2 * aw // LANES
        cos = jnp.concatenate([cos_ref[...]] * reps, axis=1)
        sin = jnp.concatenate([sin_ref[...]] * reps, axis=1)
        width = n.shape[1]
        half = qk_dim // 4
        lane = lax.broadcasted_iota(jnp.int32, n.shape, 1)
        swapped = jnp.where((lane % (2 * half)) < half,
                            pltpu.roll(n, width - half, 1), pltpu.roll(n, half, 1))
        n = n * cos + swapped * sin
    q_ref[0] = (n[:, :aw] * (qk_dim ** -0.5 * LOG2E)).astype(BF16)
    kt_ref[0, 0] = n[:, aw:].T.astype(BF16)
    v_ref[0] = u[:, v_off:y_off].astype(BF16)
    uy_ref[0] = u[:, y_off:r_off]
    ur_ref[0] = u[:, r_off:]


def _inproj(x, sh, sc, g, w_in, gqk, gmat, cos, sin, *, dims, use_rope, tm, name):
    B, T, D = x.shape
    fw, aw, lw, _ = dims
    n_in = w_in.shape[1]
    per_b = lambda b, i: (b, 0, 0)
    const = lambda b, i: (0, 0)
    tile = lambda b, i: (b, i, 0)
    return pl.pallas_call(
        functools.partial(_inproj_kernel, dims=dims, use_rope=use_rope),
        grid=(B, T // tm),
        in_specs=[pl.BlockSpec((1, tm, D), tile),
                  pl.BlockSpec((1, 1, D), per_b), pl.BlockSpec((1, 1, D), per_b),
                  pl.BlockSpec((1, D), const),
                  pl.BlockSpec((D, n_in), const),
                  pl.BlockSpec((1, 2 * aw), const),
                  pl.BlockSpec(gmat.shape, const),
                  pl.BlockSpec((tm, LANES), lambda b, i: (i, 0)),
                  pl.BlockSpec((tm, LANES), lambda b, i: (i, 0))],
        out_specs=[pl.BlockSpec((1, tm, fw), tile),
                   pl.BlockSpec((1, tm, aw), tile),
                   pl.BlockSpec((1, 1, aw, tm), lambda b, i: (b, i, 0, 0)),
                   pl.BlockSpec((1, tm, aw), tile),
                   pl.BlockSpec((1, tm, lw), tile),
                   pl.BlockSpec((1, tm, lw), tile)],
        out_shape=[jax.ShapeDtypeStruct((B, T, fw), BF16),
                   jax.ShapeDtypeStruct((B, T, aw), BF16),
                   jax.ShapeDtypeStruct((B, T // tm, aw, tm), BF16),
                   jax.ShapeDtypeStruct((B, T, aw), BF16),
                   jax.ShapeDtypeStruct((B, T, lw), F32),
                   jax.ShapeDtypeStruct((B, T, lw), F32)],
        compiler_params=_cparams("arbitrary", "arbitrary"),
        name=name,
    )(x, sh, sc, g, w_in, gqk, gmat, cos, sin)


def _attn_kernel(*refs, n_src, groups, out_scale):
    lam_ref, q_ref = refs[0], refs[1]
    kv_refs = refs[2:2 + 2 * n_src]
    g_ref, o_ref = refs[2 + 2 * n_src], refs[3 + 2 * n_src]
    q = q_ref[0].astype(F32)
    tq, w = q.shape
    lane = lax.broadcasted_iota(jnp.int32, q.shape, 1)
    qq = jnp.concatenate([jnp.where(lane < w // 2, q, 0.0),
                          jnp.where(lane >= w // 2, q, 0.0)], axis=0).astype(BF16)

    def step(kts, v, carry):
        m, l, acc = carry
        s = [jnp.dot(qq, kt, preferred_element_type=F32) for kt in kts]
        s = s[0] if len(s) == 1 else jnp.concatenate(s, axis=1)
        m_new = jnp.maximum(m, jnp.max(s, axis=-1, keepdims=True))
        alpha = jnp.exp2(m - m_new)
        p = jnp.exp2(s - m_new)
        l = alpha * l + jnp.sum(p, axis=-1, keepdims=True)
        acc = alpha * acc + jnp.dot(p.astype(BF16), v, preferred_element_type=F32)
        return m_new, l, acc

    vd = kv_refs[1].shape[-1]
    carry = (jnp.full((2 * tq, 1), -jnp.inf, F32), jnp.zeros((2 * tq, 1), F32),
             jnp.zeros((2 * tq, vd), F32))
    for n in range(n_src):
        kt_ref, v_ref, grp = kv_refs[2 * n], kv_refs[2 * n + 1], groups[n]
        nblk, kb = kt_ref.shape[1], kt_ref.shape[3]
        tk = grp * kb
        if nblk == grp:
            carry = step([kt_ref[0, g] for g in range(grp)], v_ref[0], carry)
        else:
            def body(j, carry, kt_ref=kt_ref, v_ref=v_ref, grp=grp, tk=tk):
                kts = [kt_ref[0, j * grp + g] for g in range(grp)]
                v = v_ref[0, pl.ds(pl.multiple_of(j * tk, tk), tk), :]
                return step(kts, v, carry)
            carry = lax.fori_loop(0, nblk // grp, body, carry)
    _, l, acc = carry
    o = acc / l
    d = o[:tq] - lam_ref[0] * o[tq:]
    ms = jnp.mean(d * d, axis=-1, keepdims=True)
    o_ref[0] = (d * lax.rsqrt(ms + EPS) * g_ref[...] * out_scale).astype(BF16)


def _attention(lam, q, srcs, subln_g, *, out_scale, tq, name):
    B, S, aw = q.shape
    hd = aw // ATT_HEADS
    in_specs = [pl.BlockSpec(memory_space=pltpu.SMEM),
                pl.BlockSpec((1, tq, hd), lambda b, h, i: (b, i, h))]
    args = [lam, q]
    for kt, v, _ in srcs:
        in_specs += [pl.BlockSpec((1, kt.shape[1], hd, kt.shape[3]), lambda b, h, i: (b, 0, h, 0)),
                     pl.BlockSpec((1, v.shape[1], hd), lambda b, h, i: (b, 0, h))]
        args += [kt, v]
    in_specs.append(pl.BlockSpec((1, hd), lambda b, h, i: (0, 0)))
    args.append(subln_g)
    return pl.pallas_call(
        functools.partial(_attn_kernel, n_src=len(srcs), groups=tuple(g for _, _, g in srcs),
                          out_scale=out_scale),
        grid=(B, ATT_HEADS, S // tq),
        in_specs=in_specs,
        out_specs=pl.BlockSpec((1, tq, hd), lambda b, h, i: (b, i, h)),
        out_shape=jax.ShapeDtypeStruct((B, S, aw), BF16),
        compiler_params=_cparams("arbitrary", "arbitrary", "arbitrary"),
        name=name,
    )(*args)


def _dft1_kernel(m_ref, z_ref, y_ref):
    y_ref[0] = jnp.dot(m_ref[...], z_ref[0], preferred_element_type=F32).astype(BF16)


def _dft2_kernel(y_ref, tab_ref, cc_ref, sc_ref, o_ref, *, scale):
    y = jnp.concatenate([y_ref[0, 0, 0], y_ref[0, 1, 0]], axis=0)
    zr = jnp.dot(tab_ref[0, 0], y, preferred_element_type=F32)
    zi = jnp.dot(tab_ref[0, 1], y, preferred_element_type=F32)
    o = (jnp.dot(zr.astype(BF16), cc_ref[...], preferred_element_type=F32)
         + jnp.dot(zi.astype(BF16), sc_ref[...], preferred_element_type=F32))
    o_ref[0] = (o * scale).astype(BF16)


def _fourier_tables(T, fw):
    t1n, t2n = DFT_T1, T // DFT_T1
    gd = fw // F_GROUPS
    two_pi = 2.0 * math.pi
    k1 = jnp.arange(t1n, dtype=F32)
    a1 = two_pi * jnp.mod(k1[:, None] * k1[None, :], t1n) / t1n
    m1 = jnp.concatenate([jnp.cos(a1), -jnp.sin(a1)], axis=0)
    k2 = jnp.arange(t2n, dtype=F32)
    kk = k1[:, None, None] + t1n * k2[None, :, None]
    ph = two_pi * jnp.mod(kk * k2[None, None, :], T) / T
    cp, sp = jnp.cos(ph), jnp.sin(ph)
    tab = jnp.stack([jnp.concatenate([cp, sp], axis=-1),
                     jnp.concatenate([-sp, cp], axis=-1)], axis=1)
    c = jnp.arange(gd, dtype=F32)
    ac = two_pi * jnp.mod(c[:, None] * c[None, :], gd) / gd
    eye = jnp.eye(F_GROUPS, dtype=F32)
    cc = jnp.kron(eye, jnp.cos(ac))
    sc = jnp.kron(eye, jnp.sin(ac))
    return m1.astype(BF16), tab.astype(BF16), cc.astype(BF16), sc.astype(BF16)


def _fourier_long(uf, tables):
    B, T, W = uf.shape
    m1, tab, cc, sc = tables
    t1n, t2n = DFT_T1, T // DFT_T1
    ncol = t2n * W
    tn = min(ncol, 4096)
    y = pl.pallas_call(
        _dft1_kernel,
        grid=(B, ncol // tn),
        in_specs=[pl.BlockSpec((2 * t1n, t1n), lambda b, j: (0, 0)),
                  pl.BlockSpec((1, t1n, tn), lambda b, j: (b, 0, j))],
        out_specs=pl.BlockSpec((1, 2 * t1n, tn), lambda b, j: (b, 0, j)),
        out_shape=jax.ShapeDtypeStruct((B, 2 * t1n, ncol), BF16),
        compiler_params=_cparams("arbitrary", "arbitrary"),
        name="dft_stage1",
    )(m1, uf.reshape(B, t1n, ncol))
    y5 = y.reshape(B, 2, t1n, t2n, W)
    scale = 1.0 / math.sqrt(T * (W // F_GROUPS))
    out = pl.pallas_call(
        functools.partial(_dft2_kernel, scale=scale),
        grid=(t1n, B),
        in_specs=[pl.BlockSpec((1, 2, 1, t2n, W), lambda k, b: (b, 0, k, 0, 0)),
                  pl.BlockSpec((1, 2, t2n, 2 * t2n), lambda k, b: (k, 0, 0, 0)),
                  pl.BlockSpec((W, W), lambda k, b: (0, 0)),
                  pl.BlockSpec((W, W), lambda k, b: (0, 0))],
        out_specs=pl.BlockSpec((1, t2n, W), lambda k, b: (b, 0, k)),
        out_shape=jax.ShapeDtypeStruct((B, t2n, t1n * W), BF16),
        compiler_params=_cparams("arbitrary", "arbitrary"),
        name="dft_stage2",
    )(y5, tab, cc, sc)
    return out.reshape(B, T, W)


def _dft_short_kernel(z_ref, ct_ref, st_ref, cc_ref, sc_ref, o_ref, *, scale):
    z = z_ref[0]
    zc = jnp.dot(z, cc_ref[...], preferred_element_type=F32).astype(BF16)
    zs = jnp.dot(z, sc_ref[...], preferred_element_type=F32).astype(BF16)
    o = (jnp.dot(ct_ref[...], zc, preferred_element_type=F32)
         - jnp.dot(st_ref[...], zs, preferred_element_type=F32))
    o_ref[0] = (o * scale).astype(BF16)


def _fourier_short(uf, cc, sc):
    B, T, W = uf.shape
    t = jnp.arange(T, dtype=F32)
    ang = 2.0 * math.pi * jnp.mod(t[:, None] * t[None, :], T) / T
    ct, st = jnp.cos(ang).astype(BF16), jnp.sin(ang).astype(BF16)
    scale = 1.0 / math.sqrt(T * (W // F_GROUPS))
    return pl.pallas_call(
        functools.partial(_dft_short_kernel, scale=scale),
        grid=(B,),
        in_specs=[pl.BlockSpec((1, T, W), lambda b: (b, 0, 0)),
                  pl.BlockSpec((T, T), lambda b: (0, 0)), pl.BlockSpec((T, T), lambda b: (0, 0)),
                  pl.BlockSpec((W, W), lambda b: (0, 0)), pl.BlockSpec((W, W), lambda b: (0, 0))],
        out_specs=pl.BlockSpec((1, T, W), lambda b: (b, 0, 0)),
        out_shape=jax.ShapeDtypeStruct((B, T, W), BF16),
        compiler_params=_cparams("arbitrary"),
        name="dft_short",
    )(uf, ct, st, cc, sc)


def _affine_scan(a, b, reverse):
    T = a.shape[0]
    row = lax.broadcasted_iota(jnp.int32, a.shape, 0)
    k = 1
    while k < T:
        if reverse:
            a_s, b_s, valid = pltpu.roll(a, T - k, 0), pltpu.roll(b, T - k, 0), row < T - k
        else:
            a_s, b_s, valid = pltpu.roll(a, k, 0), pltpu.roll(b, k, 0), row >= k
        b = a * jnp.where(valid, b_s, 0.0) + b
        a = a * jnp.where(valid, a_s, 1.0)
        k *= 2
    return a, b


def _lru_kernel(*refs, reverse, combine, nt):
    if combine:
        (ur_ref, prev_ref, next_ref, h0_ref, cw_ref, cb_ref, wa_ref, ba_ref, wx_ref, bx_ref,
         nsp_ref, hf_ref, uy_ref, out_ref, hlast_ref, carry_ref) = refs
    else:
        (ur_ref, prev_ref, next_ref, h0_ref, cw_ref, cb_ref, wa_ref, ba_ref, wx_ref, bx_ref,
         nsp_ref, out_ref, hlast_ref, carry_ref) = refs
    i = pl.program_id(1)
    ci = nt - 1 - i if reverse else i

    @pl.when(i == 0)
    def _():
        carry_ref[...] = h0_ref[0]

    u = ur_ref[0]
    tc = u.shape[0]
    prev = jnp.where(ci == 0, 0.0, prev_ref[0])
    nxt = jnp.where(ci == nt - 1, 0.0, next_ref[0])
    ext = jnp.concatenate([prev, u, nxt], axis=0)
    n_ext = ext.shape[0]
    xr = cb_ref[...] + u * cw_ref[CONV_LEFT:CONV_LEFT + 1, :]
    for k in range(CONV_W):
        d = k - CONV_LEFT
        if d != 0:
            shifted = pltpu.roll(ext, (-d) % n_ext, 0)[SUBLANES:SUBLANES + tc]
            xr = xr + shifted * cw_ref[k:k + 1, :]
    xb = xr.astype(BF16)
    r = jax.nn.sigmoid(jnp.dot(xb, wa_ref[...], preferred_element_type=F32) + ba_ref[...])
    g = jax.nn.sigmoid(jnp.dot(xb, wx_ref[...], preferred_element_type=F32) + bx_ref[...])
    log_a = r * nsp_ref[...]
    a = jnp.exp(log_a)
    bt = jnp.sqrt(-jnp.tanh(log_a) * (a * a + 1.0)) * (g * xr)
    a_cum, b_cum = _affine_scan(a, bt, reverse)
    h = a_cum * carry_ref[...] + b_cum
    last = h[0:1] if reverse else h[tc - 1:tc]
    carry_ref[...] = last
    hlast_ref[0] = last
    if combine:
        out_ref[0] = (jax.nn.gelu(uy_ref[0]) * (hf_ref[0] + h)).astype(BF16)
    else:
        out_ref[0] = h


def _lru_scan(ur, h0, p, d, *, reverse, hf=None, uy=None, tc):
    B, T, W = ur.shape
    nt = T // tc
    hb = tc // SUBLANES
    nh = T // SUBLANES
    combine = hf is not None
    cidx = (lambda i: nt - 1 - i) if reverse else (lambda i: i)
    tile = lambda b, i: (b, cidx(i), 0)
    const = lambda b, i: (0, 0)
    in_specs = [pl.BlockSpec((1, tc, W), tile),
                pl.BlockSpec((1, SUBLANES, W), lambda b, i: (b, jnp.maximum(cidx(i) * hb - 1, 0), 0)),
                pl.BlockSpec((1, SUBLANES, W), lambda b, i: (b, jnp.minimum((cidx(i) + 1) * hb, nh - 1), 0)),
                pl.BlockSpec((1, 1, W), lambda b, i: (b, 0, 0)),
                pl.BlockSpec((CONV_W, W), const), pl.BlockSpec((1, W), const),
                pl.BlockSpec((W, W), const), pl.BlockSpec((1, W), const),
                pl.BlockSpec((W, W), const), pl.BlockSpec((1, W), const),
                pl.BlockSpec((1, W), const)]
    args = [ur, ur, ur, h0.reshape(B, 1, W), p['conv_w'], p['conv_b'],
            p['wa'][d], p['ba'][d], p['wx'][d], p['bx'][d], p['nsp'][d]]
    if combine:
        in_specs += [pl.BlockSpec((1, tc, W), tile), pl.BlockSpec((1, tc, W), tile)]
        args += [hf, uy]
    out, hlast = pl.pallas_call(
        functools.partial(_lru_kernel, reverse=reverse, combine=combine, nt=nt),
        grid=(B, nt),
        in_specs=in_specs,
        out_specs=[pl.BlockSpec((1, tc, W), tile), pl.BlockSpec((1, 1, W), lambda b, i: (b, 0, 0))],
        out_shape=[jax.ShapeDtypeStruct((B, T, W), BF16 if combine else F32),
                   jax.ShapeDtypeStruct((B, 1, W), F32)],
        scratch_shapes=[pltpu.VMEM((1, W), F32)],
        compiler_params=_cparams("arbitrary", "arbitrary"),
        name=("lru_bwd" if reverse else "lru_fwd"),
    )(*args)
    return out, hlast.reshape(B, W)


def _outproj_kernel(f_ref, a_ref, r_ref, x_ref, g1_ref, sh_ref, sc_ref, ng_ref, w_ref, wr_ref, br_ref,
                    x1_ref, h2_ref, lg_ref, *, fw, aw):
    y = (jnp.dot(f_ref[0], w_ref[0:fw, :], preferred_element_type=F32)
         + jnp.dot(a_ref[0], w_ref[fw:fw + aw, :], preferred_element_type=F32)
         + jnp.dot(r_ref[0], w_ref[fw + aw:, :], preferred_element_type=F32))
    x1 = x_ref[0] + g1_ref[0] * y
    x1_ref[0] = x1
    h2 = _rms_mod(x1, ng_ref[...], sc_ref[0], sh_ref[0])
    h2_ref[0] = h2
    lg_ref[0] = jnp.dot(h2.astype(BF16), wr_ref[...], preferred_element_type=F32) + br_ref[...]


def _outproj(four, att, rec, x, g1, sh2, sc2, ng, w_out, w_rt, b_rt, *, tm):
    B, T, D = x.shape
    fw, aw = four.shape[2], att.shape[2]
    tile = lambda b, i: (b, i, 0)
    per_b = lambda b, i: (b, 0, 0)
    const = lambda b, i: (0, 0)
    x1, h2, logits = pl.pallas_call(
        functools.partial(_outproj_kernel, fw=fw, aw=aw),
        grid=(B, T // tm),
        in_specs=[pl.BlockSpec((1, tm, fw), tile), pl.BlockSpec((1, tm, aw), tile),
                  pl.BlockSpec((1, tm, rec.shape[2]), tile), pl.BlockSpec((1, tm, D), tile),
                  pl.BlockSpec((1, 1, D), per_b), pl.BlockSpec((1, 1, D), per_b),
                  pl.BlockSpec((1, 1, D), per_b), pl.BlockSpec((1, D), const),
                  pl.BlockSpec((D, D), const), pl.BlockSpec((D, ROUTE_PAD), const),
                  pl.BlockSpec((1, ROUTE_PAD), const)],
        out_specs=[pl.BlockSpec((1, tm, D), tile), pl.BlockSpec((1, tm, D), tile),
                   pl.BlockSpec((1, tm, ROUTE_PAD), tile)],
        out_shape=[jax.ShapeDtypeStruct((B, T, D), F32),
                   jax.ShapeDtypeStruct((B, T, D), F32),
                   jax.ShapeDtypeStruct((B, T, ROUTE_PAD), F32)],
        compiler_params=_cparams("arbitrary", "arbitrary"),
        name="outproj",
    )(four, att, rec, x, g1, sh2, sc2, ng, w_out, w_rt, b_rt)
    return x1, h2.reshape(B * T, D), logits.reshape(B * T, ROUTE_PAD)


def _gather_rows_kernel(idx_ref, src_ref, o_ref, sem):
    rows = o_ref.shape[0]

    def issue(r, c):
        tok = idx_ref[0, 0, r]
        pltpu.make_async_copy(src_ref.at[pl.ds(tok, 1)], o_ref.at[pl.ds(r, 1)], sem).start()
        return c

    lax.fori_loop(0, rows, issue, 0, unroll=8)
    pltpu.make_async_copy(src_ref.at[pl.ds(0, rows)], o_ref, sem).wait()


def _gather_rows(src, slot_tok):
    P = slot_tok.shape[0]
    D = src.shape[1]
    R = MOE_BLOCK
    return pl.pallas_call(
        _gather_rows_kernel,
        grid=(P // R,),
        in_specs=[pl.BlockSpec((1, 1, R), lambda i: (i, 0, 0), memory_space=pltpu.SMEM),
                  pl.BlockSpec(memory_space=pl.ANY)],
        out_specs=pl.BlockSpec((R, D), lambda i: (i, 0)),
        out_shape=jax.ShapeDtypeStruct((P, D), F32),
        scratch_shapes=[pltpu.SemaphoreType.DMA(())],
        compiler_params=_cparams("arbitrary"),
        name="moe_gather",
    )(slot_tok.reshape(P // R, 1, R), src)


def _expert_kernel(be_ref, nu_ref, x_ref, ws_ref, w13_ref, w2_ref, o_ref):
    i = pl.program_id(0)
    de = w2_ref.shape[1]

    @pl.when(i < nu_ref[0])
    def _():
        h = jnp.dot(x_ref[...].astype(BF16), w13_ref[0], preferred_element_type=F32)
        hb = jax.nn.silu(h[:, :de]) * h[:, de:]
        y = jnp.dot(hb.astype(BF16), w2_ref[0], preferred_element_type=F32)
        o_ref[...] = y * ws_ref[...]

    @pl.when(i >= nu_ref[0])
    def _():
        o_ref[...] = jnp.zeros_like(o_ref)


def _experts(xs, slot_w, block_exp, n_used, w13, w2):
    P, D = xs.shape
    R = MOE_BLOCK
    de = w2.shape[1]
    clamp = lambda i, be, nu: (jnp.minimum(i, nu[0] - 1), 0)
    return pl.pallas_call(
        _expert_kernel,
        grid_spec=pltpu.PrefetchScalarGridSpec(
            num_scalar_prefetch=2,
            grid=(P // R,),
            in_specs=[pl.BlockSpec((R, D), clamp),
                      pl.BlockSpec((R, 1), clamp),
                      pl.BlockSpec((1, D, 2 * de), lambda i, be, nu: (be[i], 0, 0)),
                      pl.BlockSpec((1, de, D), lambda i, be, nu: (be[i], 0, 0))],
            out_specs=pl.BlockSpec((R, D), lambda i, be, nu: (i, 0))),
        out_shape=jax.ShapeDtypeStruct((P, D), F32),
        compiler_params=_cparams("arbitrary"),
        name="moe_experts",
    )(block_exp, n_used, xs, slot_w.reshape(P, 1), w13, w2)


def _combine_kernel(pos_ref, x_ref, g_ref, ys_ref, o_ref, buf, sem):
    tm = x_ref.shape[1]

    def issue(t, c):
        for k in range(TOP_K):
            pltpu.make_async_copy(ys_ref.at[pl.ds(pos_ref[0, 0, TOP_K * t + k], 1)],
                                  buf.at[k, pl.ds(t, 1)], sem).start()
        return c

    lax.fori_loop(0, tm, issue, 0, unroll=4)
    for k in range(TOP_K):
        pltpu.make_async_copy(ys_ref.at[pl.ds(0, tm)], buf.at[k], sem).wait()
    y = buf[0]
    for k in range(1, TOP_K):
        y = y + buf[k]
    o_ref[0] = x_ref[0] + g_ref[0] * y


def _combine(x1, g2, ys, pos, *, row_off, tm):
    B, T, D = x1.shape
    nt = T // tm
    off = row_off // tm
    pos3 = pos.reshape(-1, 1, TOP_K * tm)
    return pl.pallas_call(
        _combine_kernel,
        grid=(B, nt),
        in_specs=[pl.BlockSpec((1, 1, TOP_K * tm), lambda b, i: (b * nt + i + off, 0, 0),
                               memory_space=pltpu.SMEM),
                  pl.BlockSpec((1, tm, D), lambda b, i: (b, i, 0)),
                  pl.BlockSpec((1, 1, D), lambda b, i: (b, 0, 0)),
                  pl.BlockSpec(memory_space=pl.ANY)],
        out_specs=pl.BlockSpec((1, tm, D), lambda b, i: (b, i, 0)),
        out_shape=jax.ShapeDtypeStruct((B, T, D), F32),
        scratch_shapes=[pltpu.VMEM((TOP_K, tm, D), F32), pltpu.SemaphoreType.DMA(())],
        compiler_params=_cparams("arbitrary", "arbitrary"),
        name="moe_combine",
    )(pos3, x1, g2, ys)


def _route(logits):
    gp = jax.nn.softmax(logits[:, :N_GROUPS], axis=-1)
    g_idx = jnp.argmax(gp, axis=-1).astype(jnp.int32)
    p_g = jnp.take_along_axis(gp, g_idx[:, None], axis=1)
    el = logits[:, N_GROUPS:N_GROUPS + N_EXPERTS]
    cols = g_idx[:, None] * EXPERTS_PER_GROUP + jnp.arange(EXPERTS_PER_GROUP, dtype=jnp.int32)[None]
    el_g = jnp.take_along_axis(el, cols, axis=1)
    top_v, top_i = lax.top_k(el_g, TOP_K)
    wts = jax.nn.softmax(top_v, axis=-1) * p_g
    eid = g_idx[:, None] * EXPERTS_PER_GROUP + top_i.astype(jnp.int32)
    return eid, wts


def _dispatch_plan(eid, wts):
    N = eid.shape[0]
    A = N * TOP_K
    eid_f = eid.reshape(A)
    onehot = (eid_f[:, None] == jnp.arange(N_EXPERTS, dtype=jnp.int32)[None]).astype(jnp.int32)
    csum = jnp.cumsum(onehot, axis=0)
    counts = csum[-1]
    rank = jnp.take_along_axis(csum, eid_f[:, None], axis=1)[:, 0] - 1
    padded = (counts + MOE_BLOCK - 1) // MOE_BLOCK * MOE_BLOCK
    pends = jnp.cumsum(padded)
    pstarts = pends - padded
    dest = pstarts[eid_f] + rank
    n_blocks = -(-A // MOE_BLOCK) + N_EXPERTS
    P = n_blocks * MOE_BLOCK
    tok_f = jnp.repeat(jnp.arange(N, dtype=jnp.int32), TOP_K)
    slot_tok = jnp.zeros((P,), jnp.int32).at[dest].set(tok_f)
    slot_w = jnp.zeros((P,), F32).at[dest].set(wts.reshape(A))
    block_exp = jnp.minimum(
        jnp.searchsorted(pends, jnp.arange(n_blocks, dtype=jnp.int32) * MOE_BLOCK, side='right'),
        N_EXPERTS - 1).astype(jnp.int32)
    n_used = (pends[-1] // MOE_BLOCK).astype(jnp.int32).reshape(1)
    return slot_tok, slot_w, block_exp, n_used, dest.reshape(N, TOP_K)


def _moe(h2, logits, w13, w2):
    eid, wts = _route(logits)
    slot_tok, slot_w, block_exp, n_used, pos = _dispatch_plan(eid, wts)
    xs = _gather_rows(h2, slot_tok)
    ys = _experts(xs, slot_w, block_exp, n_used, w13, w2)
    return ys, pos


def _blockdiag(w):
    G, n, _ = w.shape
    eye = jnp.eye(G, dtype=w.dtype)
    return (eye[:, None, :, None] * w[:, :, None, :]).reshape(G * n, G * n)


def _rope_tables(S, qk_dim):
    half = qk_dim // 2
    nf = half // 2
    rows_n = S // GRID_W
    row = jnp.repeat(jnp.arange(rows_n, dtype=F32), GRID_W)
    col = jnp.tile(jnp.arange(GRID_W, dtype=F32), rows_n)
    freqs = ROPE_BASE ** (-jnp.arange(nf, dtype=F32) / nf)
    ang_r = row[:, None] * freqs
    ang_c = col[:, None] * freqs
    cos = jnp.concatenate([jnp.cos(ang_r)] * 2 + [jnp.cos(ang_c)] * 2, axis=1)
    sin = jnp.concatenate([-jnp.sin(ang_r), jnp.sin(ang_r), -jnp.sin(ang_c), jnp.sin(ang_c)], axis=1)
    reps = LANES // qk_dim
    return jnp.tile(cos, (1, reps)), jnp.tile(sin, (1, reps))


def kernel(x, c, ctx, c_ctx, w_mod, b_mod, norm1_g, norm2_g, w_in, q_norm_g, k_norm_g, lambda_q1, lambda_k1, lambda_q2, lambda_k2, subln_g, conv_w, conv_b, gate_a_w, gate_a_b, gate_x_w, gate_x_b, lru_lambda, w_out, w_group, b_group, w_router, b_router, w1, w3, w2):
    B, S, D = x.shape
    C = ctx.shape[1]
    L = w_mod.shape[0]
    qk_dim = q_norm_g.shape[1]
    fw = lw = D // 4
    aw = D // 2
    dims = (fw, aw, lw, qk_dim)
    tm_x, tm_c = min(512, S), min(512, C)
    n_ctx, n_lat = B * C, B * S

    n_rows = -(-(B + 1) // SUBLANES) * SUBLANES
    c_all = jnp.zeros((n_rows, D), F32).at[:B].set(c).at[B].set(c_ctx)
    mod = _modulation(c_all, w_mod, b_mod)

    cos_t, sin_t = _rope_tables(S, qk_dim)
    dummy_tab = jnp.zeros((C, LANES), F32)
    four_tabs = _fourier_tables(S, fw)
    gmat = _blockdiag(jnp.ones((256 // qk_dim, qk_dim, qk_dim), F32)).astype(BF16)

    xc = ctx
    for l in range(L):
        last = l == L - 1
        lam_init = 0.8 - 0.6 * math.exp(-0.3 * l)
        m = [mod[l, :, i * D:(i + 1) * D] for i in range(N_MOD)]
        mx = [a[:B, None, :] for a in m]
        mc = [jnp.broadcast_to(a[B][None, None, :], (B, 1, D)) for a in m]
        w_in_b = w_in[l].astype(BF16)
        w_out_b = w_out[l].astype(BF16)
        gqk = jnp.concatenate([jnp.tile(q_norm_g[l], aw // qk_dim),
                               jnp.tile(k_norm_g[l], aw // qk_dim)])[None, :]
        lam = (jnp.exp(jnp.sum(lambda_q1[l] * lambda_k1[l])) - jnp.exp(jnp.sum(lambda_q2[l] * lambda_k2[l]))
               + lam_init).astype(F32).reshape(1)
        sub_g = subln_g[l][None, :]
        n1 = norm1_g[l][None, :]
        n2 = norm2_g[l][None, :]
        lru_p = {
            'conv_w': conv_w[l], 'conv_b': conv_b[l][None, :],
            'wa': [_blockdiag(gate_a_w[l, d]).astype(BF16) for d in range(2)],
            'wx': [_blockdiag(gate_x_w[l, d]).astype(BF16) for d in range(2)],
            'ba': [gate_a_b[l, d][None, :] for d in range(2)],
            'bx': [gate_x_b[l, d][None, :] for d in range(2)],
            'nsp': [(-LRU_C * jax.nn.softplus(-lru_lambda[l, d]))[None, :] for d in range(2)],
        }
        w_rt = jnp.zeros((D, ROUTE_PAD), F32).at[:, :N_GROUPS].set(w_group[l]) \
            .at[:, N_GROUPS:N_GROUPS + N_EXPERTS].set(w_router[l]).astype(BF16)
        b_rt = jnp.zeros((1, ROUTE_PAD), F32).at[0, :N_GROUPS].set(b_group[l]) \
            .at[0, N_GROUPS:N_GROUPS + N_EXPERTS].set(b_router[l])
        w13 = jnp.concatenate([w1[l], w3[l]], axis=-1).astype(BF16)
        w2_b = w2[l].astype(BF16)

        ufc, qc, ktc, vc, uyc, urc = _inproj(xc, mc[0], mc[1], n1, w_in_b, gqk, gmat, dummy_tab, dummy_tab,
                                             dims=dims, use_rope=False, tm=tm_c, name="inproj_ctx")
        ufx, qx, ktx, vx, uyx, urx = _inproj(x, mx[0], mx[1], n1, w_in_b, gqk, gmat, cos_t, sin_t,
                                             dims=dims, use_rope=True, tm=tm_x, name="inproj_lat")
        grp_x = min(2, ktx.shape[1])
        att_x = _attention(lam, qx, [(ktc, vc, ktc.shape[1]), (ktx, vx, grp_x)], sub_g,
                           out_scale=1.0 - lam_init, tq=min(256, S), name="attn_lat")

        zeros_h = jnp.zeros((B, lw), F32)
        tc_c, tc_x = min(512, C), min(512, S)
        hc_f, hc_f_last = _lru_scan(urc, zeros_h, lru_p, 0, reverse=False, tc=tc_c)
        hx_f, _ = _lru_scan(urx, hc_f_last, lru_p, 0, reverse=False, tc=tc_x)
        rec_c, hc_b_first = _lru_scan(urc, zeros_h, lru_p, 1, reverse=True, hf=hc_f, uy=uyc, tc=tc_c)
        rec_x, _ = _lru_scan(urx, hc_b_first, lru_p, 1, reverse=True, hf=hx_f, uy=uyx, tc=tc_x)

        four_x = _fourier_long(ufx, four_tabs)

        if last:
            x1, h2, logits = _outproj(four_x, att_x, rec_x, x, mx[2], mx[3], mx[4], n2, w_out_b, w_rt, b_rt, tm=tm_x)
            ys, pos = _moe(h2, logits, w13, w2_b)
            x = _combine(x1, mx[5], ys, pos, row_off=0, tm=min(256, S))
        else:
            att_c = _attention(lam, qc, [(ktc, vc, ktc.shape[1])], sub_g,
                               out_scale=1.0 - lam_init, tq=min(256, C), name="attn_ctx")
            four_c = _fourier_short(ufc, four_tabs[2], four_tabs[3])
            xc1, h2c, lgc = _outproj(four_c, att_c, rec_c, xc, mc[2], mc[3], mc[4], n2, w_out_b, w_rt, b_rt, tm=tm_c)
            x1, h2x, lgx = _outproj(four_x, att_x, rec_x, x, mx[2], mx[3], mx[4], n2, w_out_b, w_rt, b_rt, tm=tm_x)
            ys, pos = _moe(jnp.concatenate([h2c, h2x], axis=0), jnp.concatenate([lgc, lgx], axis=0), w13, w2_b)
            tmc = min(256, S, C)
            xc = _combine(xc1, mc[5], ys, pos, row_off=0, tm=tmc)
            x = _combine(x1, mx[5], ys, pos, row_off=n_ctx, tm=tmc)
    return x
```

```python
import functools
import math

import jax
import jax.numpy as jnp
import numpy as np
from jax import lax
from jax.experimental import pallas as pl
from jax.experimental.pallas import tpu as pltpu

F32 = jnp.float32
BF16 = jnp.bfloat16

GRID_W = 64
F_GROUPS = 4
ATT_HEADS = 4
LRU_BLOCKS = 4
LRU_C = 8.0
CONV_W = 4
CONV_LEFT = (CONV_W - 1) // 2
N_GROUPS = 4
EXPERTS_PER_GROUP = 8
N_EXPERTS = N_GROUPS * EXPERTS_PER_GROUP
TOP_K = 2
MOE_BLOCK = 256
N_MOD = 6
EPS = 1e-6
ROPE_BASE = 10000.0
LOG2E = math.log2(math.e)

LANES = 128
SUBLANES = 8
VMEM_LIMIT = 48 * 1024 * 1024
DFT_T1 = 64
DFT_K1_PER_STEP = 8
ROUTE_PAD = 128


def _cparams(*sem):
    return pltpu.CompilerParams(dimension_semantics=sem, vmem_limit_bytes=VMEM_LIMIT)


def _mod_kernel(c_ref, w_ref, b_ref, o_ref):
    c = c_ref[...]
    s = c * jax.nn.sigmoid(c)
    o_ref[0] = jnp.dot(s, w_ref[0], preferred_element_type=F32,
                       precision=lax.Precision.HIGHEST) + b_ref[0]


def _modulation(c_all, w_mod, b_mod):
    L, D, n6 = w_mod.shape
    R = c_all.shape[0]
    tn = n6 // 4
    return pl.pallas_call(
        _mod_kernel,
        grid=(L, n6 // tn),
        in_specs=[pl.BlockSpec((R, D), lambda l, j: (0, 0)),
                  pl.BlockSpec((1, D, tn), lambda l, j: (l, 0, j)),
                  pl.BlockSpec((1, 1, tn), lambda l, j: (l, 0, j))],
        out_specs=pl.BlockSpec((1, R, tn), lambda l, j: (l, 0, j)),
        out_shape=jax.ShapeDtypeStruct((L, R, n6), F32),
        compiler_params=_cparams("arbitrary", "arbitrary"),
        name="modulation",
    )(c_all, w_mod, b_mod.reshape(L, 1, n6))


def _rms_mod(x, g, sc, sh):
    ms = jnp.mean(x * x, axis=-1, keepdims=True)
    return (x * lax.rsqrt(ms + EPS)) * g * (1.0 + sc) + sh


def _inproj_kernel(x_ref, sh_ref, sc_ref, g_ref, w_ref, gqk_ref, gmat_ref, cos_ref, sin_ref,
                   uf_ref, q_ref, kt_ref, v_ref, uy_ref, ur_ref, *, dims, use_rope):
    fw, aw, lw, qk_dim = dims
    q_off, k_off, v_off = fw, fw + aw, fw + 2 * aw
    y_off, r_off = v_off + aw, v_off + aw + lw
    h = _rms_mod(x_ref[0], g_ref[...], sc_ref[0], sh_ref[0])
    u = jnp.dot(h.astype(BF16), w_ref[...], preferred_element_type=F32)
    uf_ref[0] = u[:, :fw].astype(BF16)
    qk = u[:, q_off:v_off]
    sq = qk * qk
    hi = sq.astype(BF16)
    lo = (sq - hi.astype(F32)).astype(BF16)
    gm = gmat_ref[...]
    gw = gm.shape[0]
    parts = []
    for s in range(2 * aw // gw):
        sl = slice(s * gw, (s + 1) * gw)
        parts.append(jnp.dot(hi[:, sl], gm, preferred_element_type=F32)
                     + jnp.dot(lo[:, sl], gm, preferred_element_type=F32))
    msq = jnp.concatenate(parts, axis=1) * (1.0 / qk_dim)
    n = qk * lax.rsqrt(msq + EPS) * gqk_ref[...]
    if use_rope:
        reps = 2 * aw // LANES
        cos = jnp.concatenate([cos_ref[...]] * reps, axis=1)
        sin = jnp.concatenate([sin_ref[...]] * reps, axis=1)
        width = n.shape[1]
        half = qk_dim // 4
        lane = lax.broadcasted_iota(jnp.int32, n.shape, 1)
        swapped = jnp.where((lane % (2 * half)) < half,
                            pltpu.roll(n, width - half, 1), pltpu.roll(n, half, 1))
        n = n * cos + swapped * sin
    q_ref[0] = (n[:, :aw] * (qk_dim ** -0.5 * LOG2E)).astype(BF16)
    kt_ref[0, 0] = n[:, aw:].T.astype(BF16)
    v_ref[0] = u[:, v_off:y_off].astype(BF16)
    uy_ref[0] = u[:, y_off:r_off]
    ur_ref[0] = u[:, r_off:]


def _inproj(x, sh, sc, g, w_in, gqk, gmat, cos, sin, *, dims, use_rope, tm, name):
    B, T, D = x.shape
    fw, aw, lw, _ = dims
    n_in = w_in.shape[1]
    per_b = lambda b, i: (b, 0, 0)
    const = lambda b, i: (0, 0)
    tile = lambda b, i: (b, i, 0)
    return pl.pallas_call(
        functools.partial(_inproj_kernel, dims=dims, use_rope=use_rope),
        grid=(B, T // tm),
        in_specs=[pl.BlockSpec((1, tm, D), tile),
                  pl.BlockSpec((1, 1, D), per_b), pl.BlockSpec((1, 1, D), per_b),
                  pl.BlockSpec((1, D), const),
                  pl.BlockSpec((D, n_in), const),
                  pl.BlockSpec((1, 2 * aw), const),
                  pl.BlockSpec(gmat.shape, const),
                  pl.BlockSpec((tm, LANES), lambda b, i: (i, 0)),
                  pl.BlockSpec((tm, LANES), lambda b, i: (i, 0))],
        out_specs=[pl.BlockSpec((1, tm, fw), tile),
                   pl.BlockSpec((1, tm, aw), tile),
                   pl.BlockSpec((1, 1, aw, tm), lambda b, i: (b, i, 0, 0)),
                   pl.BlockSpec((1, tm, aw), tile),
                   pl.BlockSpec((1, tm, lw), tile),
                   pl.BlockSpec((1, tm, lw), tile)],
        out_shape=[jax.ShapeDtypeStruct((B, T, fw), BF16),
                   jax.ShapeDtypeStruct((B, T, aw), BF16),
                   jax.ShapeDtypeStruct((B, T // tm, aw, tm), BF16),
                   jax.ShapeDtypeStruct((B, T, aw), BF16),
                   jax.ShapeDtypeStruct((B, T, lw), F32),
                   jax.ShapeDtypeStruct((B, T, lw), F32)],
        compiler_params=_cparams("arbitrary", "arbitrary"),
        name=name,
    )(x, sh, sc, g, w_in, gqk, gmat, cos, sin)


def _attn_kernel(*refs, n_src, groups, out_scale):
    lam_ref, q_ref = refs[0], refs[1]
    kv_refs = refs[2:2 + 2 * n_src]
    g_ref, o_ref = refs[2 + 2 * n_src], refs[3 + 2 * n_src]
    q = q_ref[0].astype(F32)
    tq, w = q.shape
    lane = lax.broadcasted_iota(jnp.int32, q.shape, 1)
    qq = jnp.concatenate([jnp.where(lane < w // 2, q, 0.0),
                          jnp.where(lane >= w // 2, q, 0.0)], axis=0).astype(BF16)

    def step(kt, v, carry):
        m, l, acc = carry
        s = jnp.dot(qq, kt, preferred_element_type=F32)
        m_new = jnp.maximum(m, jnp.max(s, axis=-1, keepdims=True))
        alpha = jnp.exp2(m - m_new)
        p = jnp.exp2(s - m_new)
        l = alpha * l + jnp.sum(p, axis=-1, keepdims=True)
        acc = alpha * acc + jnp.dot(p.astype(BF16), v, preferred_element_type=F32)
        return m_new, l, acc

    vd = kv_refs[1].shape[-1]
    carry = (jnp.full((2 * tq, 1), -jnp.inf, F32), jnp.zeros((2 * tq, 1), F32),
             jnp.zeros((2 * tq, vd), F32))
    for n in range(n_src):
        kt_ref, v_ref, grp = kv_refs[2 * n], kv_refs[2 * n + 1], groups[n]
        nblk, kb = kt_ref.shape[1], kt_ref.shape[3]
        if nblk == grp:
            for g in range(grp):
                carry = step(kt_ref[0, g], v_ref[0, g * kb:(g + 1) * kb, :], carry)
        else:
            def body(j, carry, kt_ref=kt_ref, v_ref=v_ref, grp=grp, kb=kb):
                for g in range(grp):
                    blk = j * grp + g
                    carry = step(kt_ref[0, blk], v_ref[0, pl.ds(pl.multiple_of(blk * kb, kb), kb), :], carry)
                return carry
            carry = lax.fori_loop(0, nblk // grp, body, carry)
    _, l, acc = carry
    o = acc / l
    d = o[:tq] - lam_ref[0] * o[tq:]
    ms = jnp.mean(d * d, axis=-1, keepdims=True)
    o_ref[0] = (d * lax.rsqrt(ms + EPS) * g_ref[...] * out_scale).astype(BF16)


def _attention(lam, q, srcs, subln_g, *, out_scale, tq, name):
    B, S, aw = q.shape
    hd = aw // ATT_HEADS
    in_specs = [pl.BlockSpec(memory_space=pltpu.SMEM),
                pl.BlockSpec((1, tq, hd), lambda b, h, i: (b, i, h))]
    args = [lam, q]
    for kt, v, _ in srcs:
        in_specs += [pl.BlockSpec((1, kt.shape[1], hd, kt.shape[3]), lambda b, h, i: (b, 0, h, 0)),
                     pl.BlockSpec((1, v.shape[1], hd), lambda b, h, i: (b, 0, h))]
        args += [kt, v]
    in_specs.append(pl.BlockSpec((1, hd), lambda b, h, i: (0, 0)))
    args.append(subln_g)
    return pl.pallas_call(
        functools.partial(_attn_kernel, n_src=len(srcs), groups=tuple(g for _, _, g in srcs),
                          out_scale=out_scale),
        grid=(B, ATT_HEADS, S // tq),
        in_specs=in_specs,
        out_specs=pl.BlockSpec((1, tq, hd), lambda b, h, i: (b, i, h)),
        out_shape=jax.ShapeDtypeStruct((B, S, aw), BF16),
        compiler_params=_cparams("arbitrary", "arbitrary", "arbitrary"),
        name=name,
    )(*args)


def _dft1_kernel(m_ref, z_ref, y_ref):
    y_ref[0] = jnp.dot(m_ref[...], z_ref[0], preferred_element_type=F32).astype(BF16)


def _dft2_kernel(y_ref, tab_ref, cc_ref, sc_ref, o_ref, *, scale):
    w = cc_ref.shape[0]
    for i in range(tab_ref.shape[0]):
        y = jnp.concatenate([y_ref[0, 0, i], y_ref[0, 1, i]], axis=0)
        zr = jnp.dot(tab_ref[i, 0], y, preferred_element_type=F32)
        zi = jnp.dot(tab_ref[i, 1], y, preferred_element_type=F32)
        o = (jnp.dot(zr.astype(BF16), cc_ref[...], preferred_element_type=F32)
             + jnp.dot(zi.astype(BF16), sc_ref[...], preferred_element_type=F32))
        o_ref[0, :, i * w:(i + 1) * w] = (o * scale).astype(BF16)


@functools.lru_cache(maxsize=None)
def _fourier_tables(T, fw):
    t1n, t2n = DFT_T1, T // DFT_T1
    gd = fw // F_GROUPS
    two_pi = 2.0 * np.pi
    k1 = np.arange(t1n, dtype=np.int64)
    a1 = two_pi * ((k1[:, None] * k1[None, :]) % t1n) / t1n
    m1 = np.concatenate([np.cos(a1), -np.sin(a1)], axis=0)
    k2 = np.arange(t2n, dtype=np.int64)
    kk = k1[:, None, None] + t1n * k2[None, :, None]
    ph = two_pi * ((kk * k2[None, None, :]) % T) / T
    cp, sp = np.cos(ph), np.sin(ph)
    tab = np.stack([np.concatenate([cp, sp], axis=-1),
                    np.concatenate([-sp, cp], axis=-1)], axis=1)
    c = np.arange(gd, dtype=np.int64)
    ac = two_pi * ((c[:, None] * c[None, :]) % gd) / gd
    eye = np.eye(F_GROUPS)
    cc, sc = np.kron(eye, np.cos(ac)), np.kron(eye, np.sin(ac))
    return tuple(np.asarray(t, np.float32) for t in (m1, tab, cc, sc))


def _fourier_long(uf, tables):
    B, T, W = uf.shape
    m1, tab, cc, sc = (jnp.asarray(t, BF16) for t in tables)
    t1n, t2n = DFT_T1, T // DFT_T1
    ncol = t2n * W
    tn = min(ncol, 4096)
    y = pl.pallas_call(
        _dft1_kernel,
        grid=(B, ncol // tn),
        in_specs=[pl.BlockSpec((2 * t1n, t1n), lambda b, j: (0, 0)),
                  pl.BlockSpec((1, t1n, tn), lambda b, j: (b, 0, j))],
        out_specs=pl.BlockSpec((1, 2 * t1n, tn), lambda b, j: (b, 0, j)),
        out_shape=jax.ShapeDtypeStruct((B, 2 * t1n, ncol), BF16),
        compiler_params=_cparams("arbitrary", "arbitrary"),
        name="dft_stage1",
    )(m1, uf.reshape(B, t1n, ncol))
    y5 = y.reshape(B, 2, t1n, t2n, W)
    scale = 1.0 / math.sqrt(T * (W // F_GROUPS))
    out = pl.pallas_call(
        functools.partial(_dft2_kernel, scale=scale),
        grid=(t1n // DFT_K1_PER_STEP, B),
        in_specs=[pl.BlockSpec((1, 2, DFT_K1_PER_STEP, t2n, W), lambda k, b: (b, 0, k, 0, 0)),
                  pl.BlockSpec((DFT_K1_PER_STEP, 2, t2n, 2 * t2n), lambda k, b: (k, 0, 0, 0)),
                  pl.BlockSpec((W, W), lambda k, b: (0, 0)),
                  pl.BlockSpec((W, W), lambda k, b: (0, 0))],
        out_specs=pl.BlockSpec((1, t2n, DFT_K1_PER_STEP * W), lambda k, b: (b, 0, k)),
        out_shape=jax.ShapeDtypeStruct((B, t2n, t1n * W), BF16),
        compiler_params=_cparams("arbitrary", "arbitrary"),
        name="dft_stage2",
    )(y5, tab, cc, sc)
    return out.reshape(B, T, W)


def _dft_short_kernel(z_ref, ct_ref, st_ref, cc_ref, sc_ref, o_ref, *, scale):
    z = z_ref[0]
    zc = jnp.dot(z, cc_ref[...], preferred_element_type=F32).astype(BF16)
    zs = jnp.dot(z, sc_ref[...], preferred_element_type=F32).astype(BF16)
    o = (jnp.dot(ct_ref[...], zc, preferred_element_type=F32)
         - jnp.dot(st_ref[...], zs, preferred_element_type=F32))
    o_ref[0] = (o * scale).astype(BF16)


def _fourier_short(uf, cc, sc):
    B, T, W = uf.shape
    t = np.arange(T, dtype=np.int64)
    ang = 2.0 * np.pi * ((t[:, None] * t[None, :]) % T) / T
    ct, st = jnp.asarray(np.cos(ang), BF16), jnp.asarray(np.sin(ang), BF16)
    cc, sc = jnp.asarray(cc, BF16), jnp.asarray(sc, BF16)
    scale = 1.0 / math.sqrt(T * (W // F_GROUPS))
    return pl.pallas_call(
        functools.partial(_dft_short_kernel, scale=scale),
        grid=(B,),
        in_specs=[pl.BlockSpec((1, T, W), lambda b: (b, 0, 0)),
                  pl.BlockSpec((T, T), lambda b: (0, 0)), pl.BlockSpec((T, T), lambda b: (0, 0)),
                  pl.BlockSpec((W, W), lambda b: (0, 0)), pl.BlockSpec((W, W), lambda b: (0, 0))],
        out_specs=pl.BlockSpec((1, T, W), lambda b: (b, 0, 0)),
        out_shape=jax.ShapeDtypeStruct((B, T, W), BF16),
        compiler_params=_cparams("arbitrary"),
        name="dft_short",
    )(uf, ct, st, cc, sc)


def _affine_scan(a, b, reverse):
    T = a.shape[0]
    row = lax.broadcasted_iota(jnp.int32, a.shape, 0)
    k = 1
    while k < T:
        if reverse:
            a_s, b_s, valid = pltpu.roll(a, T - k, 0), pltpu.roll(b, T - k, 0), row < T - k
        else:
            a_s, b_s, valid = pltpu.roll(a, k, 0), pltpu.roll(b, k, 0), row >= k
        b = a * jnp.where(valid, b_s, 0.0) + b
        a = a * jnp.where(valid, a_s, 1.0)
        k *= 2
    return a, b


def _lru_kernel(*refs, reverse, combine, nt):
    if combine:
        (ur_ref, prev_ref, next_ref, h0_ref, cw_ref, cb_ref, wa_ref, ba_ref, wx_ref, bx_ref,
         nsp_ref, hf_ref, uy_ref, out_ref, hlast_ref, carry_ref) = refs
    else:
        (ur_ref, prev_ref, next_ref, h0_ref, cw_ref, cb_ref, wa_ref, ba_ref, wx_ref, bx_ref,
         nsp_ref, out_ref, hlast_ref, carry_ref) = refs
    i = pl.program_id(1)
    ci = nt - 1 - i if reverse else i

    @pl.when(i == 0)
    def _():
        carry_ref[...] = h0_ref[0]

    u = ur_ref[0]
    tc = u.shape[0]
    prev = jnp.where(ci == 0, 0.0, prev_ref[0])
    nxt = jnp.where(ci == nt - 1, 0.0, next_ref[0])
    ext = jnp.concatenate([prev, u, nxt], axis=0)
    n_ext = ext.shape[0]
    xr = cb_ref[...] + u * cw_ref[CONV_LEFT:CONV_LEFT + 1, :]
    for k in range(CONV_W):
        d = k - CONV_LEFT
        if d != 0:
            shifted = pltpu.roll(ext, (-d) % n_ext, 0)[SUBLANES:SUBLANES + tc]
            xr = xr + shifted * cw_ref[k:k + 1, :]
    xb = xr.astype(BF16)
    r = jax.nn.sigmoid(jnp.dot(xb, wa_ref[...], preferred_element_type=F32) + ba_ref[...])
    g = jax.nn.sigmoid(jnp.dot(xb, wx_ref[...], preferred_element_type=F32) + bx_ref[...])
    log_a = r * nsp_ref[...]
    a = jnp.exp(log_a)
    bt = jnp.sqrt(-jnp.tanh(log_a) * (a * a + 1.0)) * (g * xr)
    a_cum, b_cum = _affine_scan(a, bt, reverse)
    h = a_cum * carry_ref[...] + b_cum
    last = h[0:1] if reverse else h[tc - 1:tc]
    carry_ref[...] = last
    hlast_ref[0] = last
    if combine:
        out_ref[0] = (jax.nn.gelu(uy_ref[0]) * (hf_ref[0] + h)).astype(BF16)
    else:
        out_ref[0] = h


def _lru_scan(ur, h0, p, d, *, reverse, hf=None, uy=None, tc):
    B, T, W = ur.shape
    nt = T // tc
    hb = tc // SUBLANES
    nh = T // SUBLANES
    combine = hf is not None
    cidx = (lambda i: nt - 1 - i) if reverse else (lambda i: i)
    tile = lambda b, i: (b, cidx(i), 0)
    const = lambda b, i: (0, 0)
    in_specs = [pl.BlockSpec((1, tc, W), tile),
                pl.BlockSpec((1, SUBLANES, W), lambda b, i: (b, jnp.maximum(cidx(i) * hb - 1, 0), 0)),
                pl.BlockSpec((1, SUBLANES, W), lambda b, i: (b, jnp.minimum((cidx(i) + 1) * hb, nh - 1), 0)),
                pl.BlockSpec((1, 1, W), lambda b, i: (b, 0, 0)),
                pl.BlockSpec((CONV_W, W), const), pl.BlockSpec((1, W), const),
                pl.BlockSpec((W, W), const), pl.BlockSpec((1, W), const),
                pl.BlockSpec((W, W), const), pl.BlockSpec((1, W), const),
                pl.BlockSpec((1, W), const)]
    args = [ur, ur, ur, h0.reshape(B, 1, W), p['conv_w'], p['conv_b'],
            p['wa'][d], p['ba'][d], p['wx'][d], p['bx'][d], p['nsp'][d]]
    if combine:
        in_specs += [pl.BlockSpec((1, tc, W), tile), pl.BlockSpec((1, tc, W), tile)]
        args += [hf, uy]
    out, hlast = pl.pallas_call(
        functools.partial(_lru_kernel, reverse=reverse, combine=combine, nt=nt),
        grid=(B, nt),
        in_specs=in_specs,
        out_specs=[pl.BlockSpec((1, tc, W), tile), pl.BlockSpec((1, 1, W), lambda b, i: (b, 0, 0))],
        out_shape=[jax.ShapeDtypeStruct((B, T, W), BF16 if combine else F32),
                   jax.ShapeDtypeStruct((B, 1, W), F32)],
        scratch_shapes=[pltpu.VMEM((1, W), F32)],
        compiler_params=_cparams("arbitrary", "arbitrary"),
        name=("lru_bwd" if reverse else "lru_fwd"),
    )(*args)
    return out, hlast.reshape(B, W)


def _store_token_tiles(ref, val, lead=()):
    rows, width = val.shape
    nsub = width // LANES
    for j in range(nsub):
        ref[lead + (pl.ds(j, rows, stride=nsub), slice(None))] = val[:, j * LANES:(j + 1) * LANES]


def _load_token_tiles(ref, rows, nsub, lead=()):
    return jnp.concatenate([ref[lead + (pl.ds(j, rows, stride=nsub), slice(None))] for j in range(nsub)], axis=1)


def _outproj_kernel(f_ref, a_ref, r_ref, x_ref, g1_ref, sh_ref, sc_ref, ng_ref, w_ref, wr_ref, br_ref,
                    x1_ref, h2_ref, lg_ref, *, fw, aw):
    y = (jnp.dot(f_ref[0], w_ref[0:fw, :], preferred_element_type=F32)
         + jnp.dot(a_ref[0], w_ref[fw:fw + aw, :], preferred_element_type=F32)
         + jnp.dot(r_ref[0], w_ref[fw + aw:, :], preferred_element_type=F32))
    x1 = x_ref[0] + g1_ref[0] * y
    x1_ref[0] = x1
    h2 = _rms_mod(x1, ng_ref[...], sc_ref[0], sh_ref[0])
    _store_token_tiles(h2_ref, h2, lead=(0,))
    lg_ref[0] = jnp.dot(h2.astype(BF16), wr_ref[...], preferred_element_type=F32) + br_ref[...]


def _outproj(four, att, rec, x, g1, sh2, sc2, ng, w_out, w_rt, b_rt, *, tm):
    B, T, D = x.shape
    fw, aw = four.shape[2], att.shape[2]
    nsub = D // LANES
    tile = lambda b, i: (b, i, 0)
    per_b = lambda b, i: (b, 0, 0)
    const = lambda b, i: (0, 0)
    x1, h2, logits = pl.pallas_call(
        functools.partial(_outproj_kernel, fw=fw, aw=aw),
        grid=(B, T // tm),
        in_specs=[pl.BlockSpec((1, tm, fw), tile), pl.BlockSpec((1, tm, aw), tile),
                  pl.BlockSpec((1, tm, rec.shape[2]), tile), pl.BlockSpec((1, tm, D), tile),
                  pl.BlockSpec((1, 1, D), per_b), pl.BlockSpec((1, 1, D), per_b),
                  pl.BlockSpec((1, 1, D), per_b), pl.BlockSpec((1, D), const),
                  pl.BlockSpec((D, D), const), pl.BlockSpec((D, ROUTE_PAD), const),
                  pl.BlockSpec((1, ROUTE_PAD), const)],
        out_specs=[pl.BlockSpec((1, tm, D), tile), pl.BlockSpec((1, tm * nsub, LANES), tile),
                   pl.BlockSpec((1, tm, ROUTE_PAD), tile)],
        out_shape=[jax.ShapeDtypeStruct((B, T, D), F32),
                   jax.ShapeDtypeStruct((B, T * nsub, LANES), F32),
                   jax.ShapeDtypeStruct((B, T, ROUTE_PAD), F32)],
        compiler_params=_cparams("arbitrary", "arbitrary"),
        name="outproj",
    )(four, att, rec, x, g1, sh2, sc2, ng, w_out, w_rt, b_rt)
    return x1, h2.reshape(B * T * nsub, LANES), logits.reshape(B * T, ROUTE_PAD)


def _dispatch_kernel(pos_ref, h_ref, xs_in_ref, xs_ref, sem, *, nsub):
    del xs_in_ref
    tm = h_ref.shape[0] // nsub

    def issue(t, c):
        src = h_ref.at[pl.ds(pl.multiple_of(t * nsub, nsub), nsub)]
        for k in range(TOP_K):
            dst = pl.multiple_of(pos_ref[0, 0, TOP_K * t + k] * nsub, nsub)
            pltpu.make_async_copy(src, xs_ref.at[pl.ds(dst, nsub)], sem).start()
        return c

    lax.fori_loop(0, tm, issue, 0, unroll=4)
    for k in range(TOP_K):
        pltpu.make_async_copy(h_ref, xs_ref.at[pl.ds(0, tm * nsub)], sem).wait()


def _dispatch(h2, pos, xs, *, row_off, tm, nsub):
    N = h2.shape[0] // nsub
    off = row_off // tm
    return pl.pallas_call(
        functools.partial(_dispatch_kernel, nsub=nsub),
        grid=(N // tm,),
        in_specs=[pl.BlockSpec((1, 1, TOP_K * tm), lambda i: (i + off, 0, 0), memory_space=pltpu.SMEM),
                  pl.BlockSpec((tm * nsub, LANES), lambda i: (i, 0)),
                  pl.BlockSpec(memory_space=pl.ANY)],
        out_specs=pl.BlockSpec(memory_space=pl.ANY),
        out_shape=jax.ShapeDtypeStruct(xs.shape, xs.dtype),
        input_output_aliases={2: 0},
        scratch_shapes=[pltpu.SemaphoreType.DMA(())],
        compiler_params=_cparams("arbitrary"),
        name="moe_dispatch",
    )(pos.reshape(-1, 1, TOP_K * tm), h2, xs)


def _expert_kernel(be_ref, nu_ref, x_ref, w13_ref, w2_ref, o_ref):
    i = pl.program_id(0)
    de, nsub = w2_ref.shape[1], w2_ref.shape[2] // LANES

    @pl.when(i < nu_ref[0])
    def _():
        x = _load_token_tiles(x_ref, MOE_BLOCK, nsub)
        h = jnp.dot(x.astype(BF16), w13_ref[0], preferred_element_type=F32)
        hb = jax.nn.silu(h[:, :de]) * h[:, de:]
        _store_token_tiles(o_ref, jnp.dot(hb.astype(BF16), w2_ref[0], preferred_element_type=F32))

    @pl.when(i >= nu_ref[0])
    def _():
        o_ref[...] = jnp.zeros_like(o_ref)


def _experts(xs, block_exp, n_used, w13, w2):
    de, D = w2.shape[1], w2.shape[2]
    nsub = D // LANES
    R = MOE_BLOCK
    P = xs.shape[0] // nsub
    clamp = lambda i, be, nu: (jnp.minimum(i, nu[0] - 1), 0)
    return pl.pallas_call(
        _expert_kernel,
        grid_spec=pltpu.PrefetchScalarGridSpec(
            num_scalar_prefetch=2,
            grid=(P // R,),
            in_specs=[pl.BlockSpec((R * nsub, LANES), clamp),
                      pl.BlockSpec((1, D, 2 * de), lambda i, be, nu: (be[i], 0, 0)),
                      pl.BlockSpec((1, de, D), lambda i, be, nu: (be[i], 0, 0))],
            out_specs=pl.BlockSpec((R * nsub, LANES), lambda i, be, nu: (i, 0))),
        out_shape=jax.ShapeDtypeStruct((P * nsub, LANES), F32),
        compiler_params=_cparams("arbitrary"),
        name="moe_experts",
    )(block_exp, n_used, xs, w13, w2)


def _combine_kernel(pos_ref, x_ref, g_ref, w_ref, ys_ref, o_ref, buf, sem):
    tm = x_ref.shape[1]
    nsub = x_ref.shape[2] // LANES

    def issue(t, c):
        for k in range(TOP_K):
            src = pl.multiple_of(pos_ref[0, 0, TOP_K * t + k] * nsub, nsub)
            pltpu.make_async_copy(ys_ref.at[pl.ds(src, nsub)],
                                  buf.at[k, pl.ds(pl.multiple_of(t * nsub, nsub), nsub)], sem).start()
        return c

    lax.fori_loop(0, tm, issue, 0, unroll=4)
    for k in range(TOP_K):
        pltpu.make_async_copy(ys_ref.at[pl.ds(0, tm * nsub)], buf.at[k], sem).wait()
    w = w_ref[0]
    y = w[:, 0:1] * _load_token_tiles(buf, tm, nsub, lead=(0,))
    for k in range(1, TOP_K):
        y = y + w[:, k:k + 1] * _load_token_tiles(buf, tm, nsub, lead=(k,))
    o_ref[0] = x_ref[0] + g_ref[0] * y


def _combine(x1, g2, ys, pos, wts, *, row_off, tm):
    B, T, D = x1.shape
    nt = T // tm
    off = row_off // tm
    pos3 = pos.reshape(-1, 1, TOP_K * tm)
    wts3 = wts.reshape(-1, tm, TOP_K)
    return pl.pallas_call(
        _combine_kernel,
        grid=(B, nt),
        in_specs=[pl.BlockSpec((1, 1, TOP_K * tm), lambda b, i: (b * nt + i + off, 0, 0),
                               memory_space=pltpu.SMEM),
                  pl.BlockSpec((1, tm, D), lambda b, i: (b, i, 0)),
                  pl.BlockSpec((1, 1, D), lambda b, i: (b, 0, 0)),
                  pl.BlockSpec((1, tm, TOP_K), lambda b, i: (b * nt + i + off, 0, 0)),
                  pl.BlockSpec(memory_space=pl.ANY)],
        out_specs=pl.BlockSpec((1, tm, D), lambda b, i: (b, i, 0)),
        out_shape=jax.ShapeDtypeStruct((B, T, D), F32),
        scratch_shapes=[pltpu.VMEM((TOP_K, tm * (D // LANES), LANES), F32), pltpu.SemaphoreType.DMA(())],
        compiler_params=_cparams("arbitrary", "arbitrary"),
        name="moe_combine",
    )(pos3, x1, g2, wts3, ys)


def _route(logits):
    gp = jax.nn.softmax(logits[:, :N_GROUPS], axis=-1)
    g_idx = jnp.argmax(gp, axis=-1).astype(jnp.int32)
    p_g = jnp.take_along_axis(gp, g_idx[:, None], axis=1)
    el = logits[:, N_GROUPS:N_GROUPS + N_EXPERTS]
    cols = g_idx[:, None] * EXPERTS_PER_GROUP + jnp.arange(EXPERTS_PER_GROUP, dtype=jnp.int32)[None]
    el_g = jnp.take_along_axis(el, cols, axis=1)
    top_v, top_i = lax.top_k(el_g, TOP_K)
    wts = jax.nn.softmax(top_v, axis=-1) * p_g
    eid = g_idx[:, None] * EXPERTS_PER_GROUP + top_i.astype(jnp.int32)
    return eid, wts


def _dispatch_plan(eid):
    N = eid.shape[0]
    A = N * TOP_K
    eid_f = eid.reshape(A)
    onehot = (eid_f[:, None] == jnp.arange(N_EXPERTS, dtype=jnp.int32)[None]).astype(jnp.int32)
    csum = jnp.cumsum(onehot, axis=0)
    counts = csum[-1]
    rank = jnp.take_along_axis(csum, eid_f[:, None], axis=1)[:, 0] - 1
    padded = (counts + MOE_BLOCK - 1) // MOE_BLOCK * MOE_BLOCK
    pends = jnp.cumsum(padded)
    pstarts = pends - padded
    dest = pstarts[eid_f] + rank
    n_blocks = -(-A // MOE_BLOCK) + N_EXPERTS
    block_exp = jnp.minimum(
        jnp.searchsorted(pends, jnp.arange(n_blocks, dtype=jnp.int32) * MOE_BLOCK, side='right'),
        N_EXPERTS - 1).astype(jnp.int32)
    n_used = (pends[-1] // MOE_BLOCK).astype(jnp.int32).reshape(1)
    return block_exp, n_used, dest.reshape(N, TOP_K), n_blocks * MOE_BLOCK


def _moe(h2_parts, logits, w13, w2):
    eid, wts = _route(logits)
    block_exp, n_used, pos, n_slots = _dispatch_plan(eid)
    nsub = w2.shape[2] // LANES
    xs = jnp.zeros((n_slots * nsub, LANES), F32)
    row_off = 0
    for h2 in h2_parts:
        n_tok = h2.shape[0] // nsub
        xs = _dispatch(h2, pos, xs, row_off=row_off, tm=math.gcd(MOE_BLOCK, n_tok), nsub=nsub)
        row_off += n_tok
    ys = _experts(xs, block_exp, n_used, w13, w2)
    return ys, pos, wts


def _blockdiag(w):
    G, n, _ = w.shape
    eye = jnp.eye(G, dtype=w.dtype)
    return (eye[:, None, :, None] * w[:, :, None, :]).reshape(G * n, G * n)


@functools.lru_cache(maxsize=None)
def _rope_tables(S, qk_dim):
    half = qk_dim // 2
    nf = half // 2
    rows_n = S // GRID_W
    row = np.repeat(np.arange(rows_n, dtype=np.float32), GRID_W)
    col = np.tile(np.arange(GRID_W, dtype=np.float32), rows_n)
    freqs = (np.float32(ROPE_BASE) ** (-np.arange(nf, dtype=np.float32) / np.float32(nf))).astype(np.float32)
    ang_r = (row[:, None] * freqs).astype(np.float64)
    ang_c = (col[:, None] * freqs).astype(np.float64)
    cos = np.concatenate([np.cos(ang_r)] * 2 + [np.cos(ang_c)] * 2, axis=1)
    sin = np.concatenate([-np.sin(ang_r), np.sin(ang_r), -np.sin(ang_c), np.sin(ang_c)], axis=1)
    reps = LANES // qk_dim
    return np.tile(cos, (1, reps)).astype(np.float32), np.tile(sin, (1, reps)).astype(np.float32)


def kernel(x, c, ctx, c_ctx, w_mod, b_mod, norm1_g, norm2_g, w_in, q_norm_g, k_norm_g, lambda_q1, lambda_k1, lambda_q2, lambda_k2, subln_g, conv_w, conv_b, gate_a_w, gate_a_b, gate_x_w, gate_x_b, lru_lambda, w_out, w_group, b_group, w_router, b_router, w1, w3, w2):
    B, S, D = x.shape
    C = ctx.shape[1]
    L = w_mod.shape[0]
    qk_dim = q_norm_g.shape[1]
    fw = lw = D // 4
    aw = D // 2
    dims = (fw, aw, lw, qk_dim)
    tm_x, tm_c = min(512, S), min(512, C)
    n_ctx, n_lat = B * C, B * S

    n_rows = -(-(B + 1) // SUBLANES) * SUBLANES
    c_all = jnp.concatenate([c, c_ctx[None, :], jnp.zeros((n_rows - B - 1, D), F32)], axis=0)
    mod = _modulation(c_all, w_mod, b_mod)

    cos_t, sin_t = (jnp.asarray(t) for t in _rope_tables(S, qk_dim))
    dummy_tab = jnp.zeros((C, LANES), F32)
    four_tabs = _fourier_tables(S, fw)
    gmat = _blockdiag(jnp.ones((256 // qk_dim, qk_dim, qk_dim), F32)).astype(BF16)

    xc = ctx
    for l in range(L):
        last = l == L - 1
        lam_init = 0.8 - 0.6 * math.exp(-0.3 * l)
        m = [mod[l, :, i * D:(i + 1) * D] for i in range(N_MOD)]
        mx = [a[:B, None, :] for a in m]
        mc = [jnp.broadcast_to(a[B][None, None, :], (B, 1, D)) for a in m]
        w_in_b = w_in[l].astype(BF16)
        w_out_b = w_out[l].astype(BF16)
        gqk = jnp.concatenate([jnp.tile(q_norm_g[l], aw // qk_dim),
                               jnp.tile(k_norm_g[l], aw // qk_dim)])[None, :]
        lam = (jnp.exp(jnp.sum(lambda_q1[l] * lambda_k1[l])) - jnp.exp(jnp.sum(lambda_q2[l] * lambda_k2[l]))
               + lam_init).astype(F32).reshape(1)
        sub_g = subln_g[l][None, :]
        n1 = norm1_g[l][None, :]
        n2 = norm2_g[l][None, :]
        lru_p = {
            'conv_w': conv_w[l], 'conv_b': conv_b[l][None, :],
            'wa': [_blockdiag(gate_a_w[l, d]).astype(BF16) for d in range(2)],
            'wx': [_blockdiag(gate_x_w[l, d]).astype(BF16) for d in range(2)],
            'ba': [gate_a_b[l, d][None, :] for d in range(2)],
            'bx': [gate_x_b[l, d][None, :] for d in range(2)],
            'nsp': [(-LRU_C * jax.nn.softplus(-lru_lambda[l, d]))[None, :] for d in range(2)],
        }
        pad = ROUTE_PAD - N_GROUPS - N_EXPERTS
        w_rt = jnp.concatenate([w_group[l], w_router[l], jnp.zeros((D, pad), F32)], axis=1).astype(BF16)
        b_rt = jnp.concatenate([b_group[l], b_router[l], jnp.zeros((pad,), F32)])[None, :]
        w13 = jnp.concatenate([w1[l], w3[l]], axis=-1).astype(BF16)
        w2_b = w2[l].astype(BF16)

        ufc, qc, ktc, vc, uyc, urc = _inproj(xc, mc[0], mc[1], n1, w_in_b, gqk, gmat, dummy_tab, dummy_tab,
                                             dims=dims, use_rope=False, tm=tm_c, name="inproj_ctx")
        ufx, qx, ktx, vx, uyx, urx = _inproj(x, mx[0], mx[1], n1, w_in_b, gqk, gmat, cos_t, sin_t,
                                             dims=dims, use_rope=True, tm=tm_x, name="inproj_lat")
        grp_x = min(4, ktx.shape[1])
        att_x = _attention(lam, qx, [(ktc, vc, ktc.shape[1]), (ktx, vx, grp_x)], sub_g,
                           out_scale=1.0 - lam_init, tq=min(256, S), name="attn_lat")

        zeros_h = jnp.zeros((B, lw), F32)
        tc_c, tc_x = min(512, C), min(512, S)
        hc_f, hc_f_last = _lru_scan(urc, zeros_h, lru_p, 0, reverse=False, tc=tc_c)
        hx_f, _ = _lru_scan(urx, hc_f_last, lru_p, 0, reverse=False, tc=tc_x)
        rec_c, hc_b_first = _lru_scan(urc, zeros_h, lru_p, 1, reverse=True, hf=hc_f, uy=uyc, tc=tc_c)
        rec_x, _ = _lru_scan(urx, hc_b_first, lru_p, 1, reverse=True, hf=hx_f, uy=uyx, tc=tc_x)

        four_x = _fourier_long(ufx, four_tabs)

        if last:
            x1, h2, logits = _outproj(four_x, att_x, rec_x, x, mx[2], mx[3], mx[4], n2, w_out_b, w_rt, b_rt, tm=tm_x)
            ys, pos, wts = _moe([h2], logits, w13, w2_b)
            x = _combine(x1, mx[5], ys, pos, wts, row_off=0, tm=min(256, S))
        else:
            att_c = _attention(lam, qc, [(ktc, vc, ktc.shape[1])], sub_g,
                               out_scale=1.0 - lam_init, tq=min(256, C), name="attn_ctx")
            four_c = _fourier_short(ufc, four_tabs[2], four_tabs[3])
            xc1, h2c, lgc = _outproj(four_c, att_c, rec_c, xc, mc[2], mc[3], mc[4], n2, w_out_b, w_rt, b_rt, tm=tm_c)
            x1, h2x, lgx = _outproj(four_x, att_x, rec_x, x, mx[2], mx[3], mx[4], n2, w_out_b, w_rt, b_rt, tm=tm_x)
            ys, pos, wts = _moe([h2c, h2x], jnp.concatenate([lgc, lgx], axis=0), w13, w2_b)
            tmc = min(256, S, C)
            xc = _combine(xc1, mc[5], ys, pos, wts, row_off=0, tm=tmc)
            x = _combine(x1, mx[5], ys, pos, wts, row_off=n_ctx, tm=tmc)
    return x
```

```python
import functools
import math

import jax
import jax.numpy as jnp
import numpy as np
from jax import lax
from jax.experimental import pallas as pl
from jax.experimental.pallas import tpu as pltpu

F32 = jnp.float32
BF16 = jnp.bfloat16

GRID_W = 64
F_GROUPS = 4
ATT_HEADS = 4
LRU_BLOCKS = 4
LRU_C = 8.0
CONV_W = 4
CONV_LEFT = (CONV_W - 1) // 2
N_GROUPS = 4
EXPERTS_PER_GROUP = 8
N_EXPERTS = N_GROUPS * EXPERTS_PER_GROUP
TOP_K = 2
MOE_BLOCK = 512
MOE_COPY_TOKENS = 256
N_MOD = 6
EPS = 1e-6
ROPE_BASE = 10000.0
LOG2E = math.log2(math.e)
ATT_BOUND_MARGIN = 1.02
ATT_BOUND_MAX_SPAN = 100.0

LANES = 128
SUBLANES = 8
VMEM_LIMIT = 48 * 1024 * 1024
DFT_T1 = 64
DFT_K1_PER_STEP = 8
ROUTE_PAD = 128


def _cparams(*sem):
    return pltpu.CompilerParams(dimension_semantics=sem, vmem_limit_bytes=VMEM_LIMIT)


def _mod_kernel(c_ref, w_ref, b_ref, o_ref):
    c = c_ref[...]
    s = c * jax.nn.sigmoid(c)
    o_ref[0] = jnp.dot(s, w_ref[0], preferred_element_type=F32,
                       precision=lax.Precision.HIGHEST) + b_ref[0]


def _modulation(c_all, w_mod, b_mod):
    L, D, n6 = w_mod.shape
    R = c_all.shape[0]
    tn = n6 // 4
    return pl.pallas_call(
        _mod_kernel,
        grid=(L, n6 // tn),
        in_specs=[pl.BlockSpec((R, D), lambda l, j: (0, 0)),
                  pl.BlockSpec((1, D, tn), lambda l, j: (l, 0, j)),
                  pl.BlockSpec((1, 1, tn), lambda l, j: (l, 0, j))],
        out_specs=pl.BlockSpec((1, R, tn), lambda l, j: (l, 0, j)),
        out_shape=jax.ShapeDtypeStruct((L, R, n6), F32),
        compiler_params=_cparams("arbitrary", "arbitrary"),
        name="modulation",
    )(c_all, w_mod, b_mod.reshape(L, 1, n6))


def _rms_mod(x, g, sc, sh):
    ms = jnp.mean(x * x, axis=-1, keepdims=True)
    return (x * lax.rsqrt(ms + EPS)) * g * (1.0 + sc) + sh


def _inproj_kernel(x_ref, sh_ref, sc_ref, g_ref, w_ref, gqk_ref, gmat_ref, cos_ref, sin_ref,
                   uf_ref, q_ref, kt_ref, v_ref, uy_ref, ur_ref, *, dims, use_rope):
    fw, aw, lw, qk_dim = dims
    q_off, k_off, v_off = fw, fw + aw, fw + 2 * aw
    y_off, r_off = v_off + aw, v_off + aw + lw
    h = _rms_mod(x_ref[0], g_ref[...], sc_ref[0], sh_ref[0])
    u = jnp.dot(h.astype(BF16), w_ref[...], preferred_element_type=F32)
    uf_ref[0] = u[:, :fw].astype(BF16)
    qk = u[:, q_off:v_off]
    sq = qk * qk
    hi = sq.astype(BF16)
    lo = (sq - hi.astype(F32)).astype(BF16)
    gm = gmat_ref[...]
    gw = gm.shape[0]
    parts = []
    for s in range(2 * aw // gw):
        sl = slice(s * gw, (s + 1) * gw)
        parts.append(jnp.dot(hi[:, sl], gm, preferred_element_type=F32)
                     + jnp.dot(lo[:, sl], gm, preferred_element_type=F32))
    msq = jnp.concatenate(parts, axis=1) * (1.0 / qk_dim)
    n = qk * lax.rsqrt(msq + EPS) * gqk_ref[...]
    if use_rope:
        reps = 2 * aw // LANES
        cos = jnp.concatenate([cos_ref[...]] * reps, axis=1)
        sin = jnp.concatenate([sin_ref[...]] * reps, axis=1)
        width = n.shape[1]
        half = qk_dim // 4
        lane = lax.broadcasted_iota(jnp.int32, n.shape, 1)
        swapped = jnp.where((lane % (2 * half)) < half,
                            pltpu.roll(n, width - half, 1), pltpu.roll(n, half, 1))
        n = n * cos + swapped * sin
    q_ref[0] = (n[:, :aw] * (qk_dim ** -0.5 * LOG2E)).astype(BF16)
    kt_ref[0, 0] = n[:, aw:].T.astype(BF16)
    v_ref[0] = u[:, v_off:y_off].astype(BF16)
    uy_ref[0] = u[:, y_off:r_off]
    ur_ref[0] = u[:, r_off:]


def _inproj(x, sh, sc, g, w_in, gqk, gmat, cos, sin, *, dims, use_rope, tm, name):
    B, T, D = x.shape
    fw, aw, lw, _ = dims
    n_in = w_in.shape[1]
    per_b = lambda b, i: (b, 0, 0)
    const = lambda b, i: (0, 0)
    tile = lambda b, i: (b, i, 0)
    return pl.pallas_call(
        functools.partial(_inproj_kernel, dims=dims, use_rope=use_rope),
        grid=(B, T // tm),
        in_specs=[pl.BlockSpec((1, tm, D), tile),
                  pl.BlockSpec((1, 1, D), per_b), pl.BlockSpec((1, 1, D), per_b),
                  pl.BlockSpec((1, D), const),
                  pl.BlockSpec((D, n_in), const),
                  pl.BlockSpec((1, 2 * aw), const),
                  pl.BlockSpec(gmat.shape, const),
                  pl.BlockSpec((tm, LANES), lambda b, i: (i, 0)),
                  pl.BlockSpec((tm, LANES), lambda b, i: (i, 0))],
        out_specs=[pl.BlockSpec((1, tm, fw), tile),
                   pl.BlockSpec((1, tm, aw), tile),
                   pl.BlockSpec((1, 1, aw, tm), lambda b, i: (b, i, 0, 0)),
                   pl.BlockSpec((1, tm, aw), tile),
                   pl.BlockSpec((1, tm, lw), tile),
                   pl.BlockSpec((1, tm, lw), tile)],
        out_shape=[jax.ShapeDtypeStruct((B, T, fw), BF16),
                   jax.ShapeDtypeStruct((B, T, aw), BF16),
                   jax.ShapeDtypeStruct((B, T // tm, aw, tm), BF16),
                   jax.ShapeDtypeStruct((B, T, aw), BF16),
                   jax.ShapeDtypeStruct((B, T, lw), F32),
                   jax.ShapeDtypeStruct((B, T, lw), F32)],
        compiler_params=_cparams("arbitrary", "arbitrary"),
        name=name,
    )(x, sh, sc, g, w_in, gqk, gmat, cos, sin)


def _attn_kernel(*refs, n_src, groups, out_scale, online_max):
    sc_ref, q_ref = refs[0], refs[1]
    kv_refs = refs[2:2 + 2 * n_src]
    g_ref, o_ref = refs[2 + 2 * n_src], refs[3 + 2 * n_src]
    q = q_ref[0].astype(F32)
    tq, w = q.shape
    lane = lax.broadcasted_iota(jnp.int32, q.shape, 1)
    qq = jnp.concatenate([jnp.where(lane < w // 2, q, 0.0),
                          jnp.where(lane >= w // 2, q, 0.0)], axis=0).astype(BF16)
    vd = kv_refs[1].shape[-1]

    if online_max:
        def step(kt, v, carry):
            m, l, acc = carry
            s = jnp.dot(qq, kt, preferred_element_type=F32)
            m_new = jnp.maximum(m, jnp.max(s, axis=-1, keepdims=True))
            alpha = jnp.exp2(m - m_new)
            p = jnp.exp2(s - m_new)
            l = alpha * l + jnp.sum(p, axis=-1, keepdims=True)
            acc = alpha * acc + jnp.dot(p.astype(BF16), v, preferred_element_type=F32)
            return m_new, l, acc

        carry = (jnp.full((2 * tq, 1), -jnp.inf, F32), jnp.zeros((2 * tq, 1), F32),
                 jnp.zeros((2 * tq, vd), F32))
    else:
        bound = sc_ref[1]

        def step(kt, v, carry):
            l_part, acc = carry
            s = jnp.dot(qq, kt, preferred_element_type=F32)
            p = jnp.exp2(s - bound)
            for c in range(s.shape[1] // LANES):
                l_part = l_part + p[:, c * LANES:(c + 1) * LANES]
            acc = acc + jnp.dot(p.astype(BF16), v, preferred_element_type=F32)
            return l_part, acc

        carry = (jnp.zeros((2 * tq, LANES), F32), jnp.zeros((2 * tq, vd), F32))

    for n in range(n_src):
        kt_ref, v_ref, grp = kv_refs[2 * n], kv_refs[2 * n + 1], groups[n]
        nblk, kb = kt_ref.shape[1], kt_ref.shape[3]
        if nblk == grp:
            for g in range(grp):
                carry = step(kt_ref[0, g], v_ref[0, g * kb:(g + 1) * kb, :], carry)
        else:
            def body(j, carry, kt_ref=kt_ref, v_ref=v_ref, grp=grp, kb=kb):
                for g in range(grp):
                    blk = j * grp + g
                    carry = step(kt_ref[0, blk], v_ref[0, pl.ds(pl.multiple_of(blk * kb, kb), kb), :], carry)
                return carry
            carry = lax.fori_loop(0, nblk // grp, body, carry)
    if online_max:
        _, l, acc = carry
    else:
        l_part, acc = carry
        l = jnp.sum(l_part, axis=-1, keepdims=True)
    o = acc / l
    d = o[:tq] - sc_ref[0] * o[tq:]
    ms = jnp.mean(d * d, axis=-1, keepdims=True)
    o_ref[0] = (d * lax.rsqrt(ms + EPS) * g_ref[...] * out_scale).astype(BF16)


def _attention(scalars, q, srcs, subln_g, *, out_scale, tq, online_max, name):
    B, S, aw = q.shape
    hd = aw // ATT_HEADS
    in_specs = [pl.BlockSpec(memory_space=pltpu.SMEM),
                pl.BlockSpec((1, tq, hd), lambda b, h, i: (b, i, h))]
    args = [scalars, q]
    for kt, v, _ in srcs:
        in_specs += [pl.BlockSpec((1, kt.shape[1], hd, kt.shape[3]), lambda b, h, i: (b, 0, h, 0)),
                     pl.BlockSpec((1, v.shape[1], hd), lambda b, h, i: (b, 0, h))]
        args += [kt, v]
    in_specs.append(pl.BlockSpec((1, hd), lambda b, h, i: (0, 0)))
    args.append(subln_g)
    return pl.pallas_call(
        functools.partial(_attn_kernel, n_src=len(srcs), groups=tuple(g for _, _, g in srcs),
                          out_scale=out_scale, online_max=online_max),
        grid=(B, ATT_HEADS, S // tq),
        in_specs=in_specs,
        out_specs=pl.BlockSpec((1, tq, hd), lambda b, h, i: (b, i, h)),
        out_shape=jax.ShapeDtypeStruct((B, S, aw), BF16),
        compiler_params=_cparams("arbitrary", "arbitrary", "arbitrary"),
        name=name,
    )(*args)


def _dft1_kernel(m_ref, z_ref, y_ref):
    y_ref[0] = jnp.dot(m_ref[...], z_ref[0], preferred_element_type=F32).astype(BF16)


def _dft2_kernel(y_ref, tab_ref, cc_ref, sc_ref, o_ref, *, scale):
    w = cc_ref.shape[0]
    for i in range(tab_ref.shape[0]):
        y = jnp.concatenate([y_ref[0, 0, i], y_ref[0, 1, i]], axis=0)
        zr = jnp.dot(tab_ref[i, 0], y, preferred_element_type=F32)
        zi = jnp.dot(tab_ref[i, 1], y, preferred_element_type=F32)
        o = (jnp.dot(zr.astype(BF16), cc_ref[...], preferred_element_type=F32)
             + jnp.dot(zi.astype(BF16), sc_ref[...], preferred_element_type=F32))
        o_ref[0, :, i * w:(i + 1) * w] = (o * scale).astype(BF16)


@functools.lru_cache(maxsize=None)
def _fourier_tables(T, fw):
    t1n, t2n = DFT_T1, T // DFT_T1
    gd = fw // F_GROUPS
    two_pi = 2.0 * np.pi
    k1 = np.arange(t1n, dtype=np.int64)
    a1 = two_pi * ((k1[:, None] * k1[None, :]) % t1n) / t1n
    m1 = np.concatenate([np.cos(a1), -np.sin(a1)], axis=0)
    k2 = np.arange(t2n, dtype=np.int64)
    kk = k1[:, None, None] + t1n * k2[None, :, None]
    ph = two_pi * ((kk * k2[None, None, :]) % T) / T
    cp, sp = np.cos(ph), np.sin(ph)
    tab = np.stack([np.concatenate([cp, sp], axis=-1),
                    np.concatenate([-sp, cp], axis=-1)], axis=1)
    c = np.arange(gd, dtype=np.int64)
    ac = two_pi * ((c[:, None] * c[None, :]) % gd) / gd
    eye = np.eye(F_GROUPS)
    cc, sc = np.kron(eye, np.cos(ac)), np.kron(eye, np.sin(ac))
    return tuple(np.asarray(t, np.float32) for t in (m1, tab, cc, sc))


def _fourier_long(uf, tables):
    B, T, W = uf.shape
    m1, tab, cc, sc = (jnp.asarray(t, BF16) for t in tables)
    t1n, t2n = DFT_T1, T // DFT_T1
    ncol = t2n * W
    tn = min(ncol, 4096)
    y = pl.pallas_call(
        _dft1_kernel,
        grid=(B, ncol // tn),
        in_specs=[pl.BlockSpec((2 * t1n, t1n), lambda b, j: (0, 0)),
                  pl.BlockSpec((1, t1n, tn), lambda b, j: (b, 0, j))],
        out_specs=pl.BlockSpec((1, 2 * t1n, tn), lambda b, j: (b, 0, j)),
        out_shape=jax.ShapeDtypeStruct((B, 2 * t1n, ncol), BF16),
        compiler_params=_cparams("arbitrary", "arbitrary"),
        name="dft_stage1",
    )(m1, uf.reshape(B, t1n, ncol))
    y5 = y.reshape(B, 2, t1n, t2n, W)
    scale = 1.0 / math.sqrt(T * (W // F_GROUPS))
    out = pl.pallas_call(
        functools.partial(_dft2_kernel, scale=scale),
        grid=(t1n // DFT_K1_PER_STEP, B),
        in_specs=[pl.BlockSpec((1, 2, DFT_K1_PER_STEP, t2n, W), lambda k, b: (b, 0, k, 0, 0)),
                  pl.BlockSpec((DFT_K1_PER_STEP, 2, t2n, 2 * t2n), lambda k, b: (k, 0, 0, 0)),
                  pl.BlockSpec((W, W), lambda k, b: (0, 0)),
                  pl.BlockSpec((W, W), lambda k, b: (0, 0))],
        out_specs=pl.BlockSpec((1, t2n, DFT_K1_PER_STEP * W), lambda k, b: (b, 0, k)),
        out_shape=jax.ShapeDtypeStruct((B, t2n, t1n * W), BF16),
        compiler_params=_cparams("arbitrary", "arbitrary"),
        name="dft_stage2",
    )(y5, tab, cc, sc)
    return out.reshape(B, T, W)


def _dft_short_kernel(z_ref, ct_ref, st_ref, cc_ref, sc_ref, o_ref, *, scale):
    z = z_ref[0]
    zc = jnp.dot(z, cc_ref[...], preferred_element_type=F32).astype(BF16)
    zs = jnp.dot(z, sc_ref[...], preferred_element_type=F32).astype(BF16)
    o = (jnp.dot(ct_ref[...], zc, preferred_element_type=F32)
         - jnp.dot(st_ref[...], zs, preferred_element_type=F32))
    o_ref[0] = (o * scale).astype(BF16)


def _fourier_short(uf, cc, sc):
    B, T, W = uf.shape
    t = np.arange(T, dtype=np.int64)
    ang = 2.0 * np.pi * ((t[:, None] * t[None, :]) % T) / T
    ct, st = jnp.asarray(np.cos(ang), BF16), jnp.asarray(np.sin(ang), BF16)
    cc, sc = jnp.asarray(cc, BF16), jnp.asarray(sc, BF16)
    scale = 1.0 / math.sqrt(T * (W // F_GROUPS))
    return pl.pallas_call(
        functools.partial(_dft_short_kernel, scale=scale),
        grid=(B,),
        in_specs=[pl.BlockSpec((1, T, W), lambda b: (b, 0, 0)),
                  pl.BlockSpec((T, T), lambda b: (0, 0)), pl.BlockSpec((T, T), lambda b: (0, 0)),
                  pl.BlockSpec((W, W), lambda b: (0, 0)), pl.BlockSpec((W, W), lambda b: (0, 0))],
        out_specs=pl.BlockSpec((1, T, W), lambda b: (b, 0, 0)),
        out_shape=jax.ShapeDtypeStruct((B, T, W), BF16),
        compiler_params=_cparams("arbitrary"),
        name="dft_short",
    )(uf, ct, st, cc, sc)


def _affine_scan(a, b, reverse):
    T = a.shape[0]
    row = lax.broadcasted_iota(jnp.int32, a.shape, 0)
    k = 1
    while k < T:
        if reverse:
            a_s, b_s, valid = pltpu.roll(a, T - k, 0), pltpu.roll(b, T - k, 0), row < T - k
        else:
            a_s, b_s, valid = pltpu.roll(a, k, 0), pltpu.roll(b, k, 0), row >= k
        b = a * jnp.where(valid, b_s, 0.0) + b
        a = a * jnp.where(valid, a_s, 1.0)
        k *= 2
    return a, b


def _lru_kernel(*refs, reverse, combine, nt):
    if combine:
        (ur_ref, prev_ref, next_ref, h0_ref, cw_ref, cb_ref, wa_ref, ba_ref, wx_ref, bx_ref,
         nsp_ref, hf_ref, uy_ref, out_ref, hlast_ref, carry_ref) = refs
    else:
        (ur_ref, prev_ref, next_ref, h0_ref, cw_ref, cb_ref, wa_ref, ba_ref, wx_ref, bx_ref,
         nsp_ref, out_ref, hlast_ref, carry_ref) = refs
    i = pl.program_id(1)
    ci = nt - 1 - i if reverse else i

    @pl.when(i == 0)
    def _():
        carry_ref[...] = h0_ref[0]

    u = ur_ref[0]
    tc = u.shape[0]
    prev = jnp.where(ci == 0, 0.0, prev_ref[0])
    nxt = jnp.where(ci == nt - 1, 0.0, next_ref[0])
    ext = jnp.concatenate([prev, u, nxt], axis=0)
    n_ext = ext.shape[0]
    xr = cb_ref[...] + u * cw_ref[CONV_LEFT:CONV_LEFT + 1, :]
    for k in range(CONV_W):
        d = k - CONV_LEFT
        if d != 0:
            shifted = pltpu.roll(ext, (-d) % n_ext, 0)[SUBLANES:SUBLANES + tc]
            xr = xr + shifted * cw_ref[k:k + 1, :]
    xb = xr.astype(BF16)
    r = jax.nn.sigmoid(jnp.dot(xb, wa_ref[...], preferred_element_type=F32) + ba_ref[...])
    g = jax.nn.sigmoid(jnp.dot(xb, wx_ref[...], preferred_element_type=F32) + bx_ref[...])
    log_a = r * nsp_ref[...]
    a = jnp.exp(log_a)
    bt = jnp.sqrt(-jnp.tanh(log_a) * (a * a + 1.0)) * (g * xr)
    a_cum, b_cum = _affine_scan(a, bt, reverse)
    h = a_cum * carry_ref[...] + b_cum
    last = h[0:1] if reverse else h[tc - 1:tc]
    carry_ref[...] = last
    hlast_ref[0] = last
    if combine:
        out_ref[0] = (jax.nn.gelu(uy_ref[0]) * (hf_ref[0] + h)).astype(BF16)
    else:
        out_ref[0] = h


def _lru_scan(ur, h0, p, d, *, reverse, hf=None, uy=None, tc):
    B, T, W = ur.shape
    nt = T // tc
    hb = tc // SUBLANES
    nh = T // SUBLANES
    combine = hf is not None
    cidx = (lambda i: nt - 1 - i) if reverse else (lambda i: i)
    tile = lambda b, i: (b, cidx(i), 0)
    const = lambda b, i: (0, 0)
    in_specs = [pl.BlockSpec((1, tc, W), tile),
                pl.BlockSpec((1, SUBLANES, W), lambda b, i: (b, jnp.maximum(cidx(i) * hb - 1, 0), 0)),
                pl.BlockSpec((1, SUBLANES, W), lambda b, i: (b, jnp.minimum((cidx(i) + 1) * hb, nh - 1), 0)),
                pl.BlockSpec((1, 1, W), lambda b, i: (b, 0, 0)),
                pl.BlockSpec((CONV_W, W), const), pl.BlockSpec((1, W), const),
                pl.BlockSpec((W, W), const), pl.BlockSpec((1, W), const),
                pl.BlockSpec((W, W), const), pl.BlockSpec((1, W), const),
                pl.BlockSpec((1, W), const)]
    args = [ur, ur, ur, h0.reshape(B, 1, W), p['conv_w'], p['conv_b'],
            p['wa'][d], p['ba'][d], p['wx'][d], p['bx'][d], p['nsp'][d]]
    if combine:
        in_specs += [pl.BlockSpec((1, tc, W), tile), pl.BlockSpec((1, tc, W), tile)]
        args += [hf, uy]
    out, hlast = pl.pallas_call(
        functools.partial(_lru_kernel, reverse=reverse, combine=combine, nt=nt),
        grid=(B, nt),
        in_specs=in_specs,
        out_specs=[pl.BlockSpec((1, tc, W), tile), pl.BlockSpec((1, 1, W), lambda b, i: (b, 0, 0))],
        out_shape=[jax.ShapeDtypeStruct((B, T, W), BF16 if combine else F32),
                   jax.ShapeDtypeStruct((B, 1, W), F32)],
        scratch_shapes=[pltpu.VMEM((1, W), F32)],
        compiler_params=_cparams("arbitrary", "arbitrary"),
        name=("lru_bwd" if reverse else "lru_fwd"),
    )(*args)
    return out, hlast.reshape(B, W)


def _store_token_tiles(ref, val, lead=()):
    rows, width = val.shape
    nsub = width // LANES
    for j in range(nsub):
        ref[lead + (pl.ds(j, rows, stride=nsub), slice(None))] = val[:, j * LANES:(j + 1) * LANES]


def _load_token_tiles(ref, rows, nsub, lead=()):
    return jnp.concatenate([ref[lead + (pl.ds(j, rows, stride=nsub), slice(None))] for j in range(nsub)], axis=1)


def _outproj_kernel(f_ref, a_ref, r_ref, x_ref, g1_ref, sh_ref, sc_ref, ng_ref, w_ref, wr_ref, br_ref,
                    x1_ref, h2_ref, lg_ref, *, fw, aw):
    y = (jnp.dot(f_ref[0], w_ref[0:fw, :], preferred_element_type=F32)
         + jnp.dot(a_ref[0], w_ref[fw:fw + aw, :], preferred_element_type=F32)
         + jnp.dot(r_ref[0], w_ref[fw + aw:, :], preferred_element_type=F32))
    x1 = x_ref[0] + g1_ref[0] * y
    x1_ref[0] = x1
    h2 = _rms_mod(x1, ng_ref[...], sc_ref[0], sh_ref[0])
    _store_token_tiles(h2_ref, h2, lead=(0,))
    lg_ref[0] = jnp.dot(h2.astype(BF16), wr_ref[...], preferred_element_type=F32) + br_ref[...]


def _outproj(four, att, rec, x, g1, sh2, sc2, ng, w_out, w_rt, b_rt, *, tm):
    B, T, D = x.shape
    fw, aw = four.shape[2], att.shape[2]
    nsub = D // LANES
    tile = lambda b, i: (b, i, 0)
    per_b = lambda b, i: (b, 0, 0)
    const = lambda b, i: (0, 0)
    x1, h2, logits = pl.pallas_call(
        functools.partial(_outproj_kernel, fw=fw, aw=aw),
        grid=(B, T // tm),
        in_specs=[pl.BlockSpec((1, tm, fw), tile), pl.BlockSpec((1, tm, aw), tile),
                  pl.BlockSpec((1, tm, rec.shape[2]), tile), pl.BlockSpec((1, tm, D), tile),
                  pl.BlockSpec((1, 1, D), per_b), pl.BlockSpec((1, 1, D), per_b),
                  pl.BlockSpec((1, 1, D), per_b), pl.BlockSpec((1, D), const),
                  pl.BlockSpec((D, D), const), pl.BlockSpec((D, ROUTE_PAD), const),
                  pl.BlockSpec((1, ROUTE_PAD), const)],
        out_specs=[pl.BlockSpec((1, tm, D), tile), pl.BlockSpec((1, tm * nsub, LANES), tile),
                   pl.BlockSpec((1, tm, ROUTE_PAD), tile)],
        out_shape=[jax.ShapeDtypeStruct((B, T, D), F32),
                   jax.ShapeDtypeStruct((B, T * nsub, LANES), F32),
                   jax.ShapeDtypeStruct((B, T, ROUTE_PAD), F32)],
        compiler_params=_cparams("arbitrary", "arbitrary"),
        name="outproj",
    )(four, att, rec, x, g1, sh2, sc2, ng, w_out, w_rt, b_rt)
    return x1, h2.reshape(B * T * nsub, LANES), logits.reshape(B * T, ROUTE_PAD)


def _dispatch_kernel(pos_ref, h_ref, xs_in_ref, xs_ref, sem, *, nsub):
    del xs_in_ref
    tm = h_ref.shape[0] // nsub

    def issue(t, c):
        src = h_ref.at[pl.ds(pl.multiple_of(t * nsub, nsub), nsub)]
        for k in range(TOP_K):
            dst = pl.multiple_of(pos_ref[0, 0, TOP_K * t + k] * nsub, nsub)
            pltpu.make_async_copy(src, xs_ref.at[pl.ds(dst, nsub)], sem).start(priority=k % 2)
        return c

    lax.fori_loop(0, tm, issue, 0, unroll=4)
    for k in range(TOP_K):
        pltpu.make_async_copy(h_ref, xs_ref.at[pl.ds(0, tm * nsub)], sem).wait()


def _dispatch(h2, pos, xs, *, row_off, tm, nsub):
    N = h2.shape[0] // nsub
    off = row_off // tm
    return pl.pallas_call(
        functools.partial(_dispatch_kernel, nsub=nsub),
        grid=(N // tm,),
        in_specs=[pl.BlockSpec((1, 1, TOP_K * tm), lambda i: (i + off, 0, 0), memory_space=pltpu.SMEM),
                  pl.BlockSpec((tm * nsub, LANES), lambda i: (i, 0)),
                  pl.BlockSpec(memory_space=pl.ANY)],
        out_specs=pl.BlockSpec(memory_space=pl.ANY),
        out_shape=jax.ShapeDtypeStruct(xs.shape, xs.dtype),
        input_output_aliases={2: 0},
        scratch_shapes=[pltpu.SemaphoreType.DMA(())],
        compiler_params=_cparams("arbitrary"),
        name="moe_dispatch",
    )(pos.reshape(-1, 1, TOP_K * tm), h2, xs)


def _expert_kernel(be_ref, nu_ref, x_ref, w13_ref, w2_ref, o_ref):
    i = pl.program_id(0)
    de, nsub = w2_ref.shape[1], w2_ref.shape[2] // LANES

    @pl.when(i < nu_ref[0])
    def _():
        x = _load_token_tiles(x_ref, MOE_BLOCK, nsub)
        h = jnp.dot(x.astype(BF16), w13_ref[0], preferred_element_type=F32)
        hb = jax.nn.silu(h[:, :de]) * h[:, de:]
        _store_token_tiles(o_ref, jnp.dot(hb.astype(BF16), w2_ref[0], preferred_element_type=F32))

    @pl.when(i >= nu_ref[0])
    def _():
        o_ref[...] = jnp.zeros_like(o_ref)


def _experts(xs, block_exp, n_used, w13, w2):
    de, D = w2.shape[1], w2.shape[2]
    nsub = D // LANES
    R = MOE_BLOCK
    P = xs.shape[0] // nsub
    clamp = lambda i, be, nu: (jnp.minimum(i, nu[0] - 1), 0)
    return pl.pallas_call(
        _expert_kernel,
        grid_spec=pltpu.PrefetchScalarGridSpec(
            num_scalar_prefetch=2,
            grid=(P // R,),
            in_specs=[pl.BlockSpec((R * nsub, LANES), clamp),
                      pl.BlockSpec((1, D, 2 * de), lambda i, be, nu: (be[i], 0, 0)),
                      pl.BlockSpec((1, de, D), lambda i, be, nu: (be[i], 0, 0))],
            out_specs=pl.BlockSpec((R * nsub, LANES), lambda i, be, nu: (i, 0))),
        out_shape=jax.ShapeDtypeStruct((P * nsub, LANES), F32),
        compiler_params=_cparams("arbitrary"),
        name="moe_experts",
    )(block_exp, n_used, xs, w13, w2)


def _combine_kernel(pos_ref, x_ref, g_ref, w_ref, ys_ref, o_ref, buf, sem):
    tm = x_ref.shape[1]
    nsub = x_ref.shape[2] // LANES

    def issue(t, c):
        for k in range(TOP_K):
            src = pl.multiple_of(pos_ref[0, 0, TOP_K * t + k] * nsub, nsub)
            pltpu.make_async_copy(ys_ref.at[pl.ds(src, nsub)],
                                  buf.at[k, pl.ds(pl.multiple_of(t * nsub, nsub), nsub)], sem).start(priority=k % 2)
        return c

    lax.fori_loop(0, tm, issue, 0, unroll=4)
    for k in range(TOP_K):
        pltpu.make_async_copy(ys_ref.at[pl.ds(0, tm * nsub)], buf.at[k], sem).wait()
    w = w_ref[0]
    y = w[:, 0:1] * _load_token_tiles(buf, tm, nsub, lead=(0,))
    for k in range(1, TOP_K):
        y = y + w[:, k:k + 1] * _load_token_tiles(buf, tm, nsub, lead=(k,))
    o_ref[0] = x_ref[0] + g_ref[0] * y


def _combine(x1, g2, ys, pos, wts, *, row_off, tm):
    B, T, D = x1.shape
    nt = T // tm
    off = row_off // tm
    pos3 = pos.reshape(-1, 1, TOP_K * tm)
    wts3 = wts.reshape(-1, tm, TOP_K)
    return pl.pallas_call(
        _combine_kernel,
        grid=(B, nt),
        in_specs=[pl.BlockSpec((1, 1, TOP_K * tm), lambda b, i: (b * nt + i + off, 0, 0),
                               memory_space=pltpu.SMEM),
                  pl.BlockSpec((1, tm, D), lambda b, i: (b, i, 0)),
                  pl.BlockSpec((1, 1, D), lambda b, i: (b, 0, 0)),
                  pl.BlockSpec((1, tm, TOP_K), lambda b, i: (b * nt + i + off, 0, 0)),
                  pl.BlockSpec(memory_space=pl.ANY)],
        out_specs=pl.BlockSpec((1, tm, D), lambda b, i: (b, i, 0)),
        out_shape=jax.ShapeDtypeStruct((B, T, D), F32),
        scratch_shapes=[pltpu.VMEM((TOP_K, tm * (D // LANES), LANES), F32), pltpu.SemaphoreType.DMA(())],
        compiler_params=_cparams("arbitrary", "arbitrary"),
        name="moe_combine",
    )(pos3, x1, g2, wts3, ys)


def _route(logits):
    gp = jax.nn.softmax(logits[:, :N_GROUPS], axis=-1)
    g_idx = jnp.argmax(gp, axis=-1).astype(jnp.int32)
    p_g = jnp.take_along_axis(gp, g_idx[:, None], axis=1)
    el = logits[:, N_GROUPS:N_GROUPS + N_EXPERTS]
    cols = g_idx[:, None] * EXPERTS_PER_GROUP + jnp.arange(EXPERTS_PER_GROUP, dtype=jnp.int32)[None]
    el_g = jnp.take_along_axis(el, cols, axis=1)
    top_v, top_i = lax.top_k(el_g, TOP_K)
    wts = jax.nn.softmax(top_v, axis=-1) * p_g
    eid = g_idx[:, None] * EXPERTS_PER_GROUP + top_i.astype(jnp.int32)
    return eid, wts


def _dispatch_plan(eid):
    N = eid.shape[0]
    A = N * TOP_K
    eid_f = eid.reshape(A)
    onehot = (eid_f[:, None] == jnp.arange(N_EXPERTS, dtype=jnp.int32)[None]).astype(jnp.int32)
    csum = jnp.cumsum(onehot, axis=0)
    counts = csum[-1]
    rank = jnp.take_along_axis(csum, eid_f[:, None], axis=1)[:, 0] - 1
    padded = (counts + MOE_BLOCK - 1) // MOE_BLOCK * MOE_BLOCK
    pends = jnp.cumsum(padded)
    pstarts = pends - padded
    dest = pstarts[eid_f] + rank
    n_blocks = -(-A // MOE_BLOCK) + N_EXPERTS
    block_exp = jnp.minimum(
        jnp.searchsorted(pends, jnp.arange(n_blocks, dtype=jnp.int32) * MOE_BLOCK, side='right'),
        N_EXPERTS - 1).astype(jnp.int32)
    n_used = (pends[-1] // MOE_BLOCK).astype(jnp.int32).reshape(1)
    return block_exp, n_used, dest.reshape(N, TOP_K), n_blocks * MOE_BLOCK


def _moe(h2_parts, logits, w13, w2):
    eid, wts = _route(logits)
    block_exp, n_used, pos, n_slots = _dispatch_plan(eid)
    nsub = w2.shape[2] // LANES
    xs = jnp.zeros((n_slots * nsub, LANES), F32)
    row_off = 0
    for h2 in h2_parts:
        n_tok = h2.shape[0] // nsub
        xs = _dispatch(h2, pos, xs, row_off=row_off, tm=math.gcd(MOE_COPY_TOKENS, n_tok), nsub=nsub)
        row_off += n_tok
    ys = _experts(xs, block_exp, n_used, w13, w2)
    return ys, pos, wts


def _blockdiag(w):
    G, n, _ = w.shape
    eye = jnp.eye(G, dtype=w.dtype)
    return (eye[:, None, :, None] * w[:, :, None, :]).reshape(G * n, G * n)


@functools.lru_cache(maxsize=None)
def _rope_tables(S, qk_dim):
    half = qk_dim // 2
    nf = half // 2
    rows_n = S // GRID_W
    row = np.repeat(np.arange(rows_n, dtype=np.float32), GRID_W)
    col = np.tile(np.arange(GRID_W, dtype=np.float32), rows_n)
    freqs = (np.float32(ROPE_BASE) ** (-np.arange(nf, dtype=np.float32) / np.float32(nf))).astype(np.float32)
    ang_r = (row[:, None] * freqs).astype(np.float64)
    ang_c = (col[:, None] * freqs).astype(np.float64)
    cos = np.concatenate([np.cos(ang_r)] * 2 + [np.cos(ang_c)] * 2, axis=1)
    sin = np.concatenate([-np.sin(ang_r), np.sin(ang_r), -np.sin(ang_c), np.sin(ang_c)], axis=1)
    reps = LANES // qk_dim
    return np.tile(cos, (1, reps)).astype(np.float32), np.tile(sin, (1, reps)).astype(np.float32)


def kernel(x, c, ctx, c_ctx, w_mod, b_mod, norm1_g, norm2_g, w_in, q_norm_g, k_norm_g, lambda_q1, lambda_k1, lambda_q2, lambda_k2, subln_g, conv_w, conv_b, gate_a_w, gate_a_b, gate_x_w, gate_x_b, lru_lambda, w_out, w_group, b_group, w_router, b_router, w1, w3, w2):
    B, S, D = x.shape
    C = ctx.shape[1]
    L = w_mod.shape[0]
    qk_dim = q_norm_g.shape[1]
    fw = lw = D // 4
    aw = D // 2
    dims = (fw, aw, lw, qk_dim)
    tm_x, tm_c = min(512, S), min(512, C)
    n_ctx, n_lat = B * C, B * S

    n_rows = -(-(B + 1) // SUBLANES) * SUBLANES
    c_all = jnp.concatenate([c, c_ctx[None, :], jnp.zeros((n_rows - B - 1, D), F32)], axis=0)
    mod = _modulation(c_all, w_mod, b_mod)

    cos_t, sin_t = (jnp.asarray(t) for t in _rope_tables(S, qk_dim))
    dummy_tab = jnp.zeros((C, LANES), F32)
    four_tabs = _fourier_tables(S, fw)
    gmat = _blockdiag(jnp.ones((256 // qk_dim, qk_dim, qk_dim), F32)).astype(BF16)

    xc = ctx
    for l in range(L):
        last = l == L - 1
        lam_init = 0.8 - 0.6 * math.exp(-0.3 * l)
        m = [mod[l, :, i * D:(i + 1) * D] for i in range(N_MOD)]
        mx = [a[:B, None, :] for a in m]
        mc = [jnp.broadcast_to(a[B][None, None, :], (B, 1, D)) for a in m]
        w_in_b = w_in[l].astype(BF16)
        w_out_b = w_out[l].astype(BF16)
        gqk = jnp.concatenate([jnp.tile(q_norm_g[l], aw // qk_dim),
                               jnp.tile(k_norm_g[l], aw // qk_dim)])[None, :]
        lam = (jnp.exp(jnp.sum(lambda_q1[l] * lambda_k1[l])) - jnp.exp(jnp.sum(lambda_q2[l] * lambda_k2[l]))
               + lam_init).astype(F32)
        s_bound = (ATT_BOUND_MARGIN * qk_dim ** 0.5 * LOG2E
                   * jnp.max(jnp.abs(q_norm_g[l])) * jnp.max(jnp.abs(k_norm_g[l]))).astype(F32)
        att_sc = jnp.stack([lam, s_bound])
        sub_g = subln_g[l][None, :]
        n1 = norm1_g[l][None, :]
        n2 = norm2_g[l][None, :]
        lru_p = {
            'conv_w': conv_w[l], 'conv_b': conv_b[l][None, :],
            'wa': [_blockdiag(gate_a_w[l, d]).astype(BF16) for d in range(2)],
            'wx': [_blockdiag(gate_x_w[l, d]).astype(BF16) for d in range(2)],
            'ba': [gate_a_b[l, d][None, :] for d in range(2)],
            'bx': [gate_x_b[l, d][None, :] for d in range(2)],
            'nsp': [(-LRU_C * jax.nn.softplus(-lru_lambda[l, d]))[None, :] for d in range(2)],
        }
        pad = ROUTE_PAD - N_GROUPS - N_EXPERTS
        w_rt = jnp.concatenate([w_group[l], w_router[l], jnp.zeros((D, pad), F32)], axis=1).astype(BF16)
        b_rt = jnp.concatenate([b_group[l], b_router[l], jnp.zeros((pad,), F32)])[None, :]
        w13 = jnp.concatenate([w1[l], w3[l]], axis=-1).astype(BF16)
        w2_b = w2[l].astype(BF16)

        ufc, qc, ktc, vc, uyc, urc = _inproj(xc, mc[0], mc[1], n1, w_in_b, gqk, gmat, dummy_tab, dummy_tab,
                                             dims=dims, use_rope=False, tm=tm_c, name="inproj_ctx")
        ufx, qx, ktx, vx, uyx, urx = _inproj(x, mx[0], mx[1], n1, w_in_b, gqk, gmat, cos_t, sin_t,
                                             dims=dims, use_rope=True, tm=tm_x, name="inproj_lat")
        grp_x = min(4, ktx.shape[1])
        att_srcs = [(ktc, vc, ktc.shape[1]), (ktx, vx, grp_x)]
        att_fn = functools.partial(_attention, att_sc, qx, att_srcs, sub_g, out_scale=1.0 - lam_init,
                                   tq=min(256, S))
        att_x = lax.cond(2.0 * s_bound <= ATT_BOUND_MAX_SPAN,
                         lambda: att_fn(online_max=False, name="attn_lat_bound"),
                         lambda: att_fn(online_max=True, name="attn_lat_online"))

        zeros_h = jnp.zeros((B, lw), F32)
        tc_c, tc_x = min(512, C), min(512, S)
        hc_f, hc_f_last = _lru_scan(urc, zeros_h, lru_p, 0, reverse=False, tc=tc_c)
        hx_f, _ = _lru_scan(urx, hc_f_last, lru_p, 0, reverse=False, tc=tc_x)
        rec_c, hc_b_first = _lru_scan(urc, zeros_h, lru_p, 1, reverse=True, hf=hc_f, uy=uyc, tc=tc_c)
        rec_x, _ = _lru_scan(urx, hc_b_first, lru_p, 1, reverse=True, hf=hx_f, uy=uyx, tc=tc_x)

        four_x = _fourier_long(ufx, four_tabs)

        if last:
            x1, h2, logits = _outproj(four_x, att_x, rec_x, x, mx[2], mx[3], mx[4], n2, w_out_b, w_rt, b_rt, tm=tm_x)
            ys, pos, wts = _moe([h2], logits, w13, w2_b)
            x = _combine(x1, mx[5], ys, pos, wts, row_off=0, tm=min(MOE_COPY_TOKENS, S))
        else:
            att_c = _attention(att_sc, qc, [(ktc, vc, ktc.shape[1])], sub_g, out_scale=1.0 - lam_init,
                               tq=min(256, C), online_max=True, name="attn_ctx")
            four_c = _fourier_short(ufc, four_tabs[2], four_tabs[3])
            xc1, h2c, lgc = _outproj(four_c, att_c, rec_c, xc, mc[2], mc[3], mc[4], n2, w_out_b, w_rt, b_rt, tm=tm_c)
            x1, h2x, lgx = _outproj(four_x, att_x, rec_x, x, mx[2], mx[3], mx[4], n2, w_out_b, w_rt, b_rt, tm=tm_x)
            ys, pos, wts = _moe([h2c, h2x], jnp.concatenate([lgc, lgx], axis=0), w13, w2_b)
            tmc = min(MOE_COPY_TOKENS, S, C)
            xc = _combine(xc1, mc[5], ys, pos, wts, row_off=0, tm=tmc)
            x = _combine(x1, mx[5], ys, pos, wts, row_off=n_ctx, tm=tmc)
    return x
```

```python
import functools
import math

import jax
import jax.numpy as jnp
import numpy as np
from jax import lax
from jax.experimental import pallas as pl
from jax.experimental.pallas import tpu as pltpu

F32 = jnp.float32
BF16 = jnp.bfloat16

GRID_W = 64
F_GROUPS = 4
ATT_HEADS = 4
LRU_BLOCKS = 4
LRU_C = 8.0
CONV_W = 4
CONV_LEFT = (CONV_W - 1) // 2
N_GROUPS = 4
EXPERTS_PER_GROUP = 8
N_EXPERTS = N_GROUPS * EXPERTS_PER_GROUP
TOP_K = 2
MOE_BLOCK = 512
MOE_COPY_TOKENS = 256
ROUTE_TOKENS = 512
N_MOD = 6
EPS = 1e-6
ROPE_BASE = 10000.0
LOG2E = math.log2(math.e)
ATT_BOUND_MARGIN = 1.02
ATT_BOUND_MAX_SPAN = 100.0

LANES = 128
SUBLANES = 8
VMEM_LIMIT = 48 * 1024 * 1024
DFT_T1 = 64
DFT_K1_PER_STEP = 8
ROUTE_PAD = 128
ATT_TQ = 512
ATT_BLOCKS_PER_BODY = 16


def _cparams(*sem):
    return pltpu.CompilerParams(dimension_semantics=sem, vmem_limit_bytes=VMEM_LIMIT)


def _mod_kernel(c_ref, w_ref, b_ref, o_ref):
    c = c_ref[...]
    s = c * jax.nn.sigmoid(c)
    o_ref[0] = jnp.dot(s, w_ref[0], preferred_element_type=F32,
                       precision=lax.Precision.HIGHEST) + b_ref[0]


def _modulation(c_all, w_mod, b_mod):
    L, D, n6 = w_mod.shape
    R = c_all.shape[0]
    tn = n6 // 4
    return pl.pallas_call(
        _mod_kernel,
        grid=(L, n6 // tn),
        in_specs=[pl.BlockSpec((R, D), lambda l, j: (0, 0)),
                  pl.BlockSpec((1, D, tn), lambda l, j: (l, 0, j)),
                  pl.BlockSpec((1, 1, tn), lambda l, j: (l, 0, j))],
        out_specs=pl.BlockSpec((1, R, tn), lambda l, j: (l, 0, j)),
        out_shape=jax.ShapeDtypeStruct((L, R, n6), F32),
        compiler_params=_cparams("arbitrary", "arbitrary"),
        name="modulation",
    )(c_all, w_mod, b_mod.reshape(L, 1, n6))


def _rms_mod(x, g, sc, sh):
    ms = jnp.mean(x * x, axis=-1, keepdims=True)
    return (x * lax.rsqrt(ms + EPS)) * g * (1.0 + sc) + sh


def _inproj_kernel(x_ref, sh_ref, sc_ref, g_ref, w_ref, gqk_ref, gmat_ref, cos_ref, sin_ref,
                   uf_ref, q_ref, kt_ref, v_ref, uy_ref, ur_ref, *, dims, use_rope):
    fw, aw, lw, qk_dim = dims
    q_off, k_off, v_off = fw, fw + aw, fw + 2 * aw
    y_off, r_off = v_off + aw, v_off + aw + lw
    h = _rms_mod(x_ref[0], g_ref[...], sc_ref[0], sh_ref[0])
    u = jnp.dot(h.astype(BF16), w_ref[...], preferred_element_type=F32)
    uf_ref[0] = u[:, :fw].astype(BF16)
    qk = u[:, q_off:v_off]
    sq = qk * qk
    hi = sq.astype(BF16)
    lo = (sq - hi.astype(F32)).astype(BF16)
    gm = gmat_ref[...]
    gw = gm.shape[0]
    parts = []
    for s in range(2 * aw // gw):
        sl = slice(s * gw, (s + 1) * gw)
        parts.append(jnp.dot(hi[:, sl], gm, preferred_element_type=F32)
                     + jnp.dot(lo[:, sl], gm, preferred_element_type=F32))
    msq = jnp.concatenate(parts, axis=1) * (1.0 / qk_dim)
    n = qk * lax.rsqrt(msq + EPS) * gqk_ref[...]
    if use_rope:
        reps = 2 * aw // LANES
        cos = jnp.concatenate([cos_ref[...]] * reps, axis=1)
        sin = jnp.concatenate([sin_ref[...]] * reps, axis=1)
        width = n.shape[1]
        half = qk_dim // 4
        lane = lax.broadcasted_iota(jnp.int32, n.shape, 1)
        swapped = jnp.where((lane % (2 * half)) < half,
                            pltpu.roll(n, width - half, 1), pltpu.roll(n, half, 1))
        n = n * cos + swapped * sin
    q_ref[0] = (n[:, :aw] * (qk_dim ** -0.5 * LOG2E)).astype(BF16)
    kt_ref[0, 0] = n[:, aw:].T.astype(BF16)
    v_ref[0] = u[:, v_off:y_off].astype(BF16)
    uy_ref[0] = u[:, y_off:r_off]
    ur_ref[0] = u[:, r_off:]


def _inproj(x, sh, sc, g, w_in, gqk, gmat, cos, sin, *, dims, use_rope, tm, name):
    B, T, D = x.shape
    fw, aw, lw, _ = dims
    n_in = w_in.shape[1]
    per_b = lambda b, i: (b, 0, 0)
    const = lambda b, i: (0, 0)
    tile = lambda b, i: (b, i, 0)
    return pl.pallas_call(
        functools.partial(_inproj_kernel, dims=dims, use_rope=use_rope),
        grid=(B, T // tm),
        in_specs=[pl.BlockSpec((1, tm, D), tile),
                  pl.BlockSpec((1, 1, D), per_b), pl.BlockSpec((1, 1, D), per_b),
                  pl.BlockSpec((1, D), const),
                  pl.BlockSpec((D, n_in), const),
                  pl.BlockSpec((1, 2 * aw), const),
                  pl.BlockSpec(gmat.shape, const),
                  pl.BlockSpec((tm, LANES), lambda b, i: (i, 0)),
                  pl.BlockSpec((tm, LANES), lambda b, i: (i, 0))],
        out_specs=[pl.BlockSpec((1, tm, fw), tile),
                   pl.BlockSpec((1, tm, aw), tile),
                   pl.BlockSpec((1, 1, aw, tm), lambda b, i: (b, i, 0, 0)),
                   pl.BlockSpec((1, tm, aw), tile),
                   pl.BlockSpec((1, tm, lw), tile),
                   pl.BlockSpec((1, tm, lw), tile)],
        out_shape=[jax.ShapeDtypeStruct((B, T, fw), BF16),
                   jax.ShapeDtypeStruct((B, T, aw), BF16),
                   jax.ShapeDtypeStruct((B, T // tm, aw, tm), BF16),
                   jax.ShapeDtypeStruct((B, T, aw), BF16),
                   jax.ShapeDtypeStruct((B, T, lw), F32),
                   jax.ShapeDtypeStruct((B, T, lw), F32)],
        compiler_params=_cparams("arbitrary", "arbitrary"),
        name=name,
    )(x, sh, sc, g, w_in, gqk, gmat, cos, sin)


def _attn_kernel(*refs, n_src, groups, out_scale, online_max):
    sc_ref, q_ref = refs[0], refs[1]
    kv_refs = refs[2:2 + 2 * n_src]
    g_ref, o_ref = refs[2 + 2 * n_src], refs[3 + 2 * n_src]
    q = q_ref[0].astype(F32)
    tq, w = q.shape
    lane = lax.broadcasted_iota(jnp.int32, q.shape, 1)
    qq = jnp.concatenate([jnp.where(lane < w // 2, q, 0.0),
                          jnp.where(lane >= w // 2, q, 0.0)], axis=0).astype(BF16)
    vd = kv_refs[1].shape[-1]

    if online_max:
        def step(kt, v, carry):
            m, l, acc = carry
            s = jnp.dot(qq, kt, preferred_element_type=F32)
            m_new = jnp.maximum(m, jnp.max(s, axis=-1, keepdims=True))
            alpha = jnp.exp2(m - m_new)
            p = jnp.exp2(s - m_new)
            l = alpha * l + jnp.sum(p, axis=-1, keepdims=True)
            acc = alpha * acc + jnp.dot(p.astype(BF16), v, preferred_element_type=F32)
            return m_new, l, acc

        carry = (jnp.full((2 * tq, 1), -jnp.inf, F32), jnp.zeros((2 * tq, 1), F32),
                 jnp.zeros((2 * tq, vd), F32))
    else:
        bound = sc_ref[1]

        def step(kt, v, carry):
            l_part, acc = carry
            s = jnp.dot(qq, kt, preferred_element_type=F32)
            p = jnp.exp2(s - bound)
            for c in range(s.shape[1] // LANES):
                l_part = l_part + p[:, c * LANES:(c + 1) * LANES]
            acc = acc + jnp.dot(p.astype(BF16), v, preferred_element_type=F32)
            return l_part, acc

        carry = (jnp.zeros((2 * tq, LANES), F32), jnp.zeros((2 * tq, vd), F32))

    for n in range(n_src):
        kt_ref, v_ref, grp = kv_refs[2 * n], kv_refs[2 * n + 1], groups[n]
        nblk, kb = kt_ref.shape[1], kt_ref.shape[3]
        if nblk == grp:
            for g in range(grp):
                carry = step(kt_ref[0, g], v_ref[0, g * kb:(g + 1) * kb, :], carry)
        else:
            def body(j, carry, kt_ref=kt_ref, v_ref=v_ref, grp=grp, kb=kb):
                for g in range(grp):
                    blk = j * grp + g
                    carry = step(kt_ref[0, blk], v_ref[0, pl.ds(pl.multiple_of(blk * kb, kb), kb), :], carry)
                return carry
            carry = lax.fori_loop(0, nblk // grp, body, carry)
    if online_max:
        _, l, acc = carry
    else:
        l_part, acc = carry
        l = jnp.sum(l_part, axis=-1, keepdims=True)
    o = acc / l
    d = o[:tq] - sc_ref[0] * o[tq:]
    ms = jnp.mean(d * d, axis=-1, keepdims=True)
    o_ref[0] = (d * lax.rsqrt(ms + EPS) * g_ref[...] * out_scale).astype(BF16)


def _attention(scalars, q, srcs, subln_g, *, out_scale, tq, online_max, name):
    B, S, aw = q.shape
    hd = aw // ATT_HEADS
    in_specs = [pl.BlockSpec(memory_space=pltpu.SMEM),
                pl.BlockSpec((1, tq, hd), lambda b, h, i: (b, i, h))]
    args = [scalars, q]
    for kt, v, _ in srcs:
        in_specs += [pl.BlockSpec((1, kt.shape[1], hd, kt.shape[3]), lambda b, h, i: (b, 0, h, 0)),
                     pl.BlockSpec((1, v.shape[1], hd), lambda b, h, i: (b, 0, h))]
        args += [kt, v]
    in_specs.append(pl.BlockSpec((1, hd), lambda b, h, i: (0, 0)))
    args.append(subln_g)
    return pl.pallas_call(
        functools.partial(_attn_kernel, n_src=len(srcs), groups=tuple(g for _, _, g in srcs),
                          out_scale=out_scale, online_max=online_max),
        grid=(B, ATT_HEADS, S // tq),
        in_specs=in_specs,
        out_specs=pl.BlockSpec((1, tq, hd), lambda b, h, i: (b, i, h)),
        out_shape=jax.ShapeDtypeStruct((B, S, aw), BF16),
        compiler_params=_cparams("arbitrary", "arbitrary", "arbitrary"),
        name=name,
    )(*args)


def _dft1_kernel(m_ref, z_ref, y_ref):
    y_ref[0] = jnp.dot(m_ref[...], z_ref[0], preferred_element_type=F32).astype(BF16)


def _dft2_kernel(y_ref, tab_ref, cc_ref, sc_ref, o_ref, *, scale):
    w = cc_ref.shape[0]
    for i in range(tab_ref.shape[0]):
        y = jnp.concatenate([y_ref[0, 0, i], y_ref[0, 1, i]], axis=0)
        zr = jnp.dot(tab_ref[i, 0], y, preferred_element_type=F32)
        zi = jnp.dot(tab_ref[i, 1], y, preferred_element_type=F32)
        o = (jnp.dot(zr.astype(BF16), cc_ref[...], preferred_element_type=F32)
             + jnp.dot(zi.astype(BF16), sc_ref[...], preferred_element_type=F32))
        o_ref[0, :, i * w:(i + 1) * w] = (o * scale).astype(BF16)


@functools.lru_cache(maxsize=None)
def _fourier_tables(T, fw):
    t1n, t2n = DFT_T1, T // DFT_T1
    gd = fw // F_GROUPS
    two_pi = 2.0 * np.pi
    k1 = np.arange(t1n, dtype=np.int64)
    a1 = two_pi * ((k1[:, None] * k1[None, :]) % t1n) / t1n
    m1 = np.concatenate([np.cos(a1), -np.sin(a1)], axis=0)
    k2 = np.arange(t2n, dtype=np.int64)
    kk = k1[:, None, None] + t1n * k2[None, :, None]
    ph = two_pi * ((kk * k2[None, None, :]) % T) / T
    cp, sp = np.cos(ph), np.sin(ph)
    tab = np.stack([np.concatenate([cp, sp], axis=-1),
                    np.concatenate([-sp, cp], axis=-1)], axis=1)
    c = np.arange(gd, dtype=np.int64)
    ac = two_pi * ((c[:, None] * c[None, :]) % gd) / gd
    eye = np.eye(F_GROUPS)
    cc, sc = np.kron(eye, np.cos(ac)), np.kron(eye, np.sin(ac))
    return tuple(np.asarray(t, np.float32) for t in (m1, tab, cc, sc))


def _fourier_long(uf, tables):
    B, T, W = uf.shape
    m1, tab, cc, sc = (jnp.asarray(t, BF16) for t in tables)
    t1n, t2n = DFT_T1, T // DFT_T1
    ncol = t2n * W
    tn = min(ncol, 4096)
    y = pl.pallas_call(
        _dft1_kernel,
        grid=(B, ncol // tn),
        in_specs=[pl.BlockSpec((2 * t1n, t1n), lambda b, j: (0, 0)),
                  pl.BlockSpec((1, t1n, tn), lambda b, j: (b, 0, j))],
        out_specs=pl.BlockSpec((1, 2 * t1n, tn), lambda b, j: (b, 0, j)),
        out_shape=jax.ShapeDtypeStruct((B, 2 * t1n, ncol), BF16),
        compiler_params=_cparams("arbitrary", "arbitrary"),
        name="dft_stage1",
    )(m1, uf.reshape(B, t1n, ncol))
    y5 = y.reshape(B, 2, t1n, t2n, W)
    scale = 1.0 / math.sqrt(T * (W // F_GROUPS))
    out = pl.pallas_call(
        functools.partial(_dft2_kernel, scale=scale),
        grid=(t1n // DFT_K1_PER_STEP, B),
        in_specs=[pl.BlockSpec((1, 2, DFT_K1_PER_STEP, t2n, W), lambda k, b: (b, 0, k, 0, 0)),
                  pl.BlockSpec((DFT_K1_PER_STEP, 2, t2n, 2 * t2n), lambda k, b: (k, 0, 0, 0)),
                  pl.BlockSpec((W, W), lambda k, b: (0, 0)),
                  pl.BlockSpec((W, W), lambda k, b: (0, 0))],
        out_specs=pl.BlockSpec((1, t2n, DFT_K1_PER_STEP * W), lambda k, b: (b, 0, k)),
        out_shape=jax.ShapeDtypeStruct((B, t2n, t1n * W), BF16),
        compiler_params=_cparams("arbitrary", "arbitrary"),
        name="dft_stage2",
    )(y5, tab, cc, sc)
    return out.reshape(B, T, W)


def _dft_short_kernel(z_ref, ct_ref, st_ref, cc_ref, sc_ref, o_ref, *, scale):
    z = z_ref[0]
    zc = jnp.dot(z, cc_ref[...], preferred_element_type=F32).astype(BF16)
    zs = jnp.dot(z, sc_ref[...], preferred_element_type=F32).astype(BF16)
    o = (jnp.dot(ct_ref[...], zc, preferred_element_type=F32)
         - jnp.dot(st_ref[...], zs, preferred_element_type=F32))
    o_ref[0] = (o * scale).astype(BF16)


def _fourier_short(uf, cc, sc):
    B, T, W = uf.shape
    t = np.arange(T, dtype=np.int64)
    ang = 2.0 * np.pi * ((t[:, None] * t[None, :]) % T) / T
    ct, st = jnp.asarray(np.cos(ang), BF16), jnp.asarray(np.sin(ang), BF16)
    cc, sc = jnp.asarray(cc, BF16), jnp.asarray(sc, BF16)
    scale = 1.0 / math.sqrt(T * (W // F_GROUPS))
    return pl.pallas_call(
        functools.partial(_dft_short_kernel, scale=scale),
        grid=(B,),
        in_specs=[pl.BlockSpec((1, T, W), lambda b: (b, 0, 0)),
                  pl.BlockSpec((T, T), lambda b: (0, 0)), pl.BlockSpec((T, T), lambda b: (0, 0)),
                  pl.BlockSpec((W, W), lambda b: (0, 0)), pl.BlockSpec((W, W), lambda b: (0, 0))],
        out_specs=pl.BlockSpec((1, T, W), lambda b: (b, 0, 0)),
        out_shape=jax.ShapeDtypeStruct((B, T, W), BF16),
        compiler_params=_cparams("arbitrary"),
        name="dft_short",
    )(uf, ct, st, cc, sc)


def _affine_scan(a, b, reverse):
    T = a.shape[0]
    row = lax.broadcasted_iota(jnp.int32, a.shape, 0)
    k = 1
    while k < T:
        if reverse:
            a_s, b_s, valid = pltpu.roll(a, T - k, 0), pltpu.roll(b, T - k, 0), row < T - k
        else:
            a_s, b_s, valid = pltpu.roll(a, k, 0), pltpu.roll(b, k, 0), row >= k
        b = a * jnp.where(valid, b_s, 0.0) + b
        a = a * jnp.where(valid, a_s, 1.0)
        k *= 2
    return a, b


def _lru_kernel(*refs, reverse, combine, nt):
    if combine:
        (ur_ref, prev_ref, next_ref, h0_ref, cw_ref, cb_ref, wa_ref, ba_ref, wx_ref, bx_ref,
         nsp_ref, hf_ref, uy_ref, out_ref, hlast_ref, carry_ref) = refs
    else:
        (ur_ref, prev_ref, next_ref, h0_ref, cw_ref, cb_ref, wa_ref, ba_ref, wx_ref, bx_ref,
         nsp_ref, out_ref, hlast_ref, carry_ref) = refs
    i = pl.program_id(1)
    ci = nt - 1 - i if reverse else i

    @pl.when(i == 0)
    def _():
        carry_ref[...] = h0_ref[0]

    u = ur_ref[0]
    tc = u.shape[0]
    prev = jnp.where(ci == 0, 0.0, prev_ref[0])
    nxt = jnp.where(ci == nt - 1, 0.0, next_ref[0])
    ext = jnp.concatenate([prev, u, nxt], axis=0)
    n_ext = ext.shape[0]
    xr = cb_ref[...] + u * cw_ref[CONV_LEFT:CONV_LEFT + 1, :]
    for k in range(CONV_W):
        d = k - CONV_LEFT
        if d != 0:
            shifted = pltpu.roll(ext, (-d) % n_ext, 0)[SUBLANES:SUBLANES + tc]
            xr = xr + shifted * cw_ref[k:k + 1, :]
    xb = xr.astype(BF16)
    r = jax.nn.sigmoid(jnp.dot(xb, wa_ref[...], preferred_element_type=F32) + ba_ref[...])
    g = jax.nn.sigmoid(jnp.dot(xb, wx_ref[...], preferred_element_type=F32) + bx_ref[...])
    log_a = r * nsp_ref[...]
    a = jnp.exp(log_a)
    bt = jnp.sqrt(-jnp.tanh(log_a) * (a * a + 1.0)) * (g * xr)
    a_cum, b_cum = _affine_scan(a, bt, reverse)
    h = a_cum * carry_ref[...] + b_cum
    last = h[0:1] if reverse else h[tc - 1:tc]
    carry_ref[...] = last
    hlast_ref[0] = last
    if combine:
        out_ref[0] = (jax.nn.gelu(uy_ref[0]) * (hf_ref[0] + h)).astype(BF16)
    else:
        out_ref[0] = h


def _lru_scan(ur, h0, p, d, *, reverse, hf=None, uy=None, tc):
    B, T, W = ur.shape
    nt = T // tc
    hb = tc // SUBLANES
    nh = T // SUBLANES
    combine = hf is not None
    cidx = (lambda i: nt - 1 - i) if reverse else (lambda i: i)
    tile = lambda b, i: (b, cidx(i), 0)
    const = lambda b, i: (0, 0)
    in_specs = [pl.BlockSpec((1, tc, W), tile),
                pl.BlockSpec((1, SUBLANES, W), lambda b, i: (b, jnp.maximum(cidx(i) * hb - 1, 0), 0)),
                pl.BlockSpec((1, SUBLANES, W), lambda b, i: (b, jnp.minimum((cidx(i) + 1) * hb, nh - 1), 0)),
                pl.BlockSpec((1, 1, W), lambda b, i: (b, 0, 0)),
                pl.BlockSpec((CONV_W, W), const), pl.BlockSpec((1, W), const),
                pl.BlockSpec((W, W), const), pl.BlockSpec((1, W), const),
                pl.BlockSpec((W, W), const), pl.BlockSpec((1, W), const),
                pl.BlockSpec((1, W), const)]
    args = [ur, ur, ur, h0.reshape(B, 1, W), p['conv_w'], p['conv_b'],
            p['wa'][d], p['ba'][d], p['wx'][d], p['bx'][d], p['nsp'][d]]
    if combine:
        in_specs += [pl.BlockSpec((1, tc, W), tile), pl.BlockSpec((1, tc, W), tile)]
        args += [hf, uy]
    out, hlast = pl.pallas_call(
        functools.partial(_lru_kernel, reverse=reverse, combine=combine, nt=nt),
        grid=(B, nt),
        in_specs=in_specs,
        out_specs=[pl.BlockSpec((1, tc, W), tile), pl.BlockSpec((1, 1, W), lambda b, i: (b, 0, 0))],
        out_shape=[jax.ShapeDtypeStruct((B, T, W), BF16 if combine else F32),
                   jax.ShapeDtypeStruct((B, 1, W), F32)],
        scratch_shapes=[pltpu.VMEM((1, W), F32)],
        compiler_params=_cparams("arbitrary", "arbitrary"),
        name=("lru_bwd" if reverse else "lru_fwd"),
    )(*args)
    return out, hlast.reshape(B, W)


def _store_token_tiles(ref, val, lead=()):
    rows, width = val.shape
    nsub = width // LANES
    for j in range(nsub):
        ref[lead + (pl.ds(j, rows, stride=nsub), slice(None))] = val[:, j * LANES:(j + 1) * LANES]


def _load_token_tiles(ref, rows, nsub, lead=()):
    return jnp.concatenate([ref[lead + (pl.ds(j, rows, stride=nsub), slice(None))] for j in range(nsub)], axis=1)


def _outproj_kernel(f_ref, a_ref, r_ref, x_ref, g1_ref, sh_ref, sc_ref, ng_ref, w_ref, wr_ref, br_ref,
                    x1_ref, h2_ref, lg_ref, *, fw, aw):
    y = (jnp.dot(f_ref[0], w_ref[0:fw, :], preferred_element_type=F32)
         + jnp.dot(a_ref[0], w_ref[fw:fw + aw, :], preferred_element_type=F32)
         + jnp.dot(r_ref[0], w_ref[fw + aw:, :], preferred_element_type=F32))
    x1 = x_ref[0] + g1_ref[0] * y
    x1_ref[0] = x1
    h2 = _rms_mod(x1, ng_ref[...], sc_ref[0], sh_ref[0])
    _store_token_tiles(h2_ref, h2, lead=(0,))
    lg_ref[0] = jnp.dot(h2.astype(BF16), wr_ref[...], preferred_element_type=F32) + br_ref[...]


def _outproj(four, att, rec, x, g1, sh2, sc2, ng, w_out, w_rt, b_rt, *, tm):
    B, T, D = x.shape
    fw, aw = four.shape[2], att.shape[2]
    nsub = D // LANES
    tile = lambda b, i: (b, i, 0)
    per_b = lambda b, i: (b, 0, 0)
    const = lambda b, i: (0, 0)
    x1, h2, logits = pl.pallas_call(
        functools.partial(_outproj_kernel, fw=fw, aw=aw),
        grid=(B, T // tm),
        in_specs=[pl.BlockSpec((1, tm, fw), tile), pl.BlockSpec((1, tm, aw), tile),
                  pl.BlockSpec((1, tm, rec.shape[2]), tile), pl.BlockSpec((1, tm, D), tile),
                  pl.BlockSpec((1, 1, D), per_b), pl.BlockSpec((1, 1, D), per_b),
                  pl.BlockSpec((1, 1, D), per_b), pl.BlockSpec((1, D), const),
                  pl.BlockSpec((D, D), const), pl.BlockSpec((D, ROUTE_PAD), const),
                  pl.BlockSpec((1, ROUTE_PAD), const)],
        out_specs=[pl.BlockSpec((1, tm, D), tile), pl.BlockSpec((1, tm * nsub, LANES), tile),
                   pl.BlockSpec((1, tm, ROUTE_PAD), tile)],
        out_shape=[jax.ShapeDtypeStruct((B, T, D), F32),
                   jax.ShapeDtypeStruct((B, T * nsub, LANES), F32),
                   jax.ShapeDtypeStruct((B, T, ROUTE_PAD), F32)],
        compiler_params=_cparams("arbitrary", "arbitrary"),
        name="outproj",
    )(four, att, rec, x, g1, sh2, sc2, ng, w_out, w_rt, b_rt)
    return x1, h2.reshape(B * T * nsub, LANES), logits.reshape(B * T, ROUTE_PAD)


def _dispatch_kernel(pe_ref, nu_ref, pos_ref, *refs, part_tiles, nsub, n_blocks):
    n_parts = len(part_tiles)
    h_refs = refs[:n_parts]
    xs_ref, zbuf, sem, zsem = refs[n_parts:]
    i = pl.program_id(0)
    blk = MOE_BLOCK * nsub

    def zero_copy(row0):
        return pltpu.make_async_copy(zbuf, xs_ref.at[pl.ds(pl.multiple_of(row0, blk), blk)], zsem)

    @pl.when(i == 0)
    def _():
        zbuf[...] = jnp.zeros_like(zbuf)

        def expert_block(e, start, wait):
            pend = pe_ref[e]

            @pl.when(pend > start)
            def _():
                if wait:
                    zero_copy(0).wait()
                else:
                    zero_copy((pend - MOE_BLOCK) * nsub).start()
            return pend

        def tail_block(b, c, wait):
            if wait:
                zero_copy(0).wait()
            else:
                zero_copy(b * blk).start()
            return c

        for wait in (False, True):
            lax.fori_loop(0, N_EXPERTS, functools.partial(expert_block, wait=wait), 0)
            lax.fori_loop(nu_ref[0], n_blocks, functools.partial(tail_block, wait=wait), 0)

    first = 0
    for h_ref, nt in zip(h_refs, part_tiles):
        tm = h_ref.shape[0] // nsub

        @pl.when((i >= first) & (i < first + nt))
        def _(h_ref=h_ref, tm=tm):
            def issue(t, c):
                src = h_ref.at[pl.ds(pl.multiple_of(t * nsub, nsub), nsub)]
                for k in range(TOP_K):
                    dst = pl.multiple_of(pos_ref[0, 0, TOP_K * t + k] * nsub, nsub)
                    pltpu.make_async_copy(src, xs_ref.at[pl.ds(dst, nsub)], sem).start(priority=k % 2)
                return c

            lax.fori_loop(0, tm, issue, 0, unroll=4)
            for k in range(TOP_K):
                pltpu.make_async_copy(h_ref, xs_ref.at[pl.ds(0, tm * nsub)], sem).wait()
        first += nt


def _dispatch(h2_parts, pos, pends, n_used, *, n_slots, tm, nsub):
    part_tiles = tuple(h.shape[0] // (nsub * tm) for h in h2_parts)
    firsts = [sum(part_tiles[:p]) for p in range(len(part_tiles))]
    in_specs = [pl.BlockSpec((1, 1, TOP_K * tm), lambda i, pe, nu: (i, 0, 0), memory_space=pltpu.SMEM)]
    for first, nt in zip(firsts, part_tiles):
        in_specs.append(pl.BlockSpec(
            (tm * nsub, LANES), lambda i, pe, nu, first=first, nt=nt: (jnp.clip(i - first, 0, nt - 1), 0)))
    return pl.pallas_call(
        functools.partial(_dispatch_kernel, part_tiles=part_tiles, nsub=nsub, n_blocks=n_slots // MOE_BLOCK),
        grid_spec=pltpu.PrefetchScalarGridSpec(
            num_scalar_prefetch=2,
            grid=(sum(part_tiles),),
            in_specs=in_specs,
            out_specs=pl.BlockSpec(memory_space=pl.ANY),
            scratch_shapes=[pltpu.VMEM((MOE_BLOCK * nsub, LANES), F32),
                            pltpu.SemaphoreType.DMA(()), pltpu.SemaphoreType.DMA(())]),
        out_shape=jax.ShapeDtypeStruct((n_slots * nsub, LANES), F32),
        compiler_params=_cparams("arbitrary"),
        name="moe_dispatch",
    )(pends, n_used, pos.reshape(-1, 1, TOP_K * tm), *h2_parts)


def _expert_kernel(be_ref, nu_ref, x_ref, w13_ref, w2_ref, o_ref):
    i = pl.program_id(0)
    de, nsub = w2_ref.shape[1], w2_ref.shape[2] // LANES

    @pl.when(i < nu_ref[0])
    def _():
        x = _load_token_tiles(x_ref, MOE_BLOCK, nsub)
        h = jnp.dot(x.astype(BF16), w13_ref[0], preferred_element_type=F32)
        hb = jax.nn.silu(h[:, :de]) * h[:, de:]
        _store_token_tiles(o_ref, jnp.dot(hb.astype(BF16), w2_ref[0], preferred_element_type=F32))

    @pl.when(i >= nu_ref[0])
    def _():
        o_ref[...] = jnp.zeros_like(o_ref)


def _experts(xs, block_exp, n_used, w13, w2):
    de, D = w2.shape[1], w2.shape[2]
    nsub = D // LANES
    R = MOE_BLOCK
    P = xs.shape[0] // nsub
    clamp = lambda i, be, nu: (jnp.minimum(i, nu[0] - 1), 0)
    return pl.pallas_call(
        _expert_kernel,
        grid_spec=pltpu.PrefetchScalarGridSpec(
            num_scalar_prefetch=2,
            grid=(P // R,),
            in_specs=[pl.BlockSpec((R * nsub, LANES), clamp),
                      pl.BlockSpec((1, D, 2 * de), lambda i, be, nu: (be[i], 0, 0)),
                      pl.BlockSpec((1, de, D), lambda i, be, nu: (be[i], 0, 0))],
            out_specs=pl.BlockSpec((R * nsub, LANES), lambda i, be, nu: (i, 0))),
        out_shape=jax.ShapeDtypeStruct((P * nsub, LANES), F32),
        compiler_params=_cparams("arbitrary"),
        name="moe_experts",
    )(block_exp, n_used, xs, w13, w2)


def _combine_kernel(pos_ref, x_ref, g_ref, w_ref, ys_ref, o_ref, buf, sem):
    tm = x_ref.shape[1]
    nsub = x_ref.shape[2] // LANES

    def issue(t, c):
        for k in range(TOP_K):
            src = pl.multiple_of(pos_ref[0, 0, TOP_K * t + k] * nsub, nsub)
            pltpu.make_async_copy(ys_ref.at[pl.ds(src, nsub)],
                                  buf.at[k, pl.ds(pl.multiple_of(t * nsub, nsub), nsub)], sem).start(priority=k % 2)
        return c

    lax.fori_loop(0, tm, issue, 0, unroll=4)
    for k in range(TOP_K):
        pltpu.make_async_copy(ys_ref.at[pl.ds(0, tm * nsub)], buf.at[k], sem).wait()
    w = w_ref[0]
    y = w[:, 0:1] * _load_token_tiles(buf, tm, nsub, lead=(0,))
    for k in range(1, TOP_K):
        y = y + w[:, k:k + 1] * _load_token_tiles(buf, tm, nsub, lead=(k,))
    o_ref[0] = x_ref[0] + g_ref[0] * y


def _combine(x1, g2, ys, pos, wts, *, row_off, tm):
    B, T, D = x1.shape
    nt = T // tm
    off = row_off // tm
    pos3 = pos.reshape(-1, 1, TOP_K * tm)
    wts3 = wts.reshape(-1, tm, TOP_K)
    return pl.pallas_call(
        _combine_kernel,
        grid=(B, nt),
        in_specs=[pl.BlockSpec((1, 1, TOP_K * tm), lambda b, i: (b * nt + i + off, 0, 0),
                               memory_space=pltpu.SMEM),
                  pl.BlockSpec((1, tm, D), lambda b, i: (b, i, 0)),
                  pl.BlockSpec((1, 1, D), lambda b, i: (b, 0, 0)),
                  pl.BlockSpec((1, tm, TOP_K), lambda b, i: (b * nt + i + off, 0, 0)),
                  pl.BlockSpec(memory_space=pl.ANY)],
        out_specs=pl.BlockSpec((1, tm, D), lambda b, i: (b, i, 0)),
        out_shape=jax.ShapeDtypeStruct((B, T, D), F32),
        scratch_shapes=[pltpu.VMEM((TOP_K, tm * (D // LANES), LANES), F32), pltpu.SemaphoreType.DMA(())],
        compiler_params=_cparams("arbitrary", "arbitrary"),
        name="moe_combine",
    )(pos3, x1, g2, wts3, ys)


def _route_kernel(lg_ref, tri_ref, o_ref, cnt_ref, base_ref):
    @pl.when(pl.program_id(0) == 0)
    def _():
        base_ref[...] = jnp.zeros_like(base_ref)

    lg = lg_ref[...]
    lane = lax.broadcasted_iota(jnp.int32, lg.shape, 1)
    lane_f = lane.astype(F32)
    ninf = jnp.float32(-jnp.inf)
    far = jnp.float32(ROUTE_PAD)
    is_g = lane < N_GROUPS
    gl = jnp.where(is_g, lg, ninf)
    gmax = jnp.max(gl, axis=-1, keepdims=True)
    gsum = jnp.sum(jnp.where(is_g, jnp.exp(gl - gmax), 0.0), axis=-1, keepdims=True)
    p_g = 1.0 / gsum
    g_idx = jnp.min(jnp.where(gl == gmax, lane_f, far), axis=-1, keepdims=True)
    e_lane = lane - N_GROUPS
    e_lane_f = e_lane.astype(F32)
    shift = EXPERTS_PER_GROUP.bit_length() - 1
    in_grp = ((e_lane >= 0) & (e_lane < N_EXPERTS)
              & (jnp.right_shift(e_lane, shift).astype(F32) == g_idx))
    el = jnp.where(in_grp, lg, ninf)
    v1 = jnp.max(el, axis=-1, keepdims=True)
    i1 = jnp.min(jnp.where(el == v1, e_lane_f, far), axis=-1, keepdims=True)
    el2 = jnp.where(e_lane_f == i1, ninf, el)
    v2 = jnp.max(el2, axis=-1, keepdims=True)
    i2 = jnp.min(jnp.where(el2 == v2, e_lane_f, far), axis=-1, keepdims=True)
    w1 = p_g / (1.0 + jnp.exp(v2 - v1))
    w2 = p_g - w1
    hit1, hit2 = e_lane_f == i1, e_lane_f == i2
    oh1, oh2 = hit1.astype(F32), hit2.astype(F32)
    tri = tri_ref[...]
    pre1 = jnp.dot(tri, oh1.astype(BF16), preferred_element_type=F32)
    pre2 = jnp.dot(tri, oh2.astype(BF16), preferred_element_type=F32)
    tot1 = jnp.sum(oh1, axis=0, keepdims=True)
    tot2 = jnp.sum(oh2, axis=0, keepdims=True)
    base = base_ref[...]
    r1 = jnp.sum(jnp.where(hit1, pre1 + base, 0.0), axis=-1, keepdims=True)
    r2 = jnp.sum(jnp.where(hit2, pre2 + (base + tot1), 0.0), axis=-1, keepdims=True)
    base = base + tot1 + tot2
    base_ref[...] = base
    cnt_ref[...] = base
    cols = (i1, i2, r1, r2, w1, w2)
    out = jnp.zeros(lg.shape, F32)
    for j, col in enumerate(cols):
        out = jnp.where(lane == j, col, out)
    o_ref[...] = out


def _route(logits):
    N = logits.shape[0]
    tm = math.gcd(ROUTE_TOKENS, N)
    tri = jnp.asarray(np.tril(np.ones((tm, tm), np.float32), -1), BF16)
    out, cnt = pl.pallas_call(
        _route_kernel,
        grid=(N // tm,),
        in_specs=[pl.BlockSpec((tm, ROUTE_PAD), lambda i: (i, 0)),
                  pl.BlockSpec((tm, tm), lambda i: (0, 0))],
        out_specs=[pl.BlockSpec((tm, ROUTE_PAD), lambda i: (i, 0)),
                   pl.BlockSpec((1, ROUTE_PAD), lambda i: (0, 0))],
        out_shape=[jax.ShapeDtypeStruct((N, ROUTE_PAD), F32),
                   jax.ShapeDtypeStruct((1, ROUTE_PAD), F32)],
        scratch_shapes=[pltpu.VMEM((1, ROUTE_PAD), F32)],
        compiler_params=_cparams("arbitrary"),
        name="moe_route",
    )(logits, tri)
    eid = out[:, 0:TOP_K].astype(jnp.int32)
    rank = out[:, TOP_K:2 * TOP_K].astype(jnp.int32)
    wts = out[:, 2 * TOP_K:3 * TOP_K]
    counts = cnt[0, N_GROUPS:N_GROUPS + N_EXPERTS].astype(jnp.int32)
    return eid, rank, wts, counts


def _dispatch_plan(eid, rank, counts):
    N = eid.shape[0]
    A = N * TOP_K
    padded = (counts + MOE_BLOCK - 1) // MOE_BLOCK * MOE_BLOCK
    pends = jnp.cumsum(padded)
    pstarts = pends - padded
    dest = pstarts[eid] + rank
    n_blocks = -(-A // MOE_BLOCK) + N_EXPERTS
    block_exp = jnp.minimum(
        jnp.searchsorted(pends, jnp.arange(n_blocks, dtype=jnp.int32) * MOE_BLOCK, side='right'),
        N_EXPERTS - 1).astype(jnp.int32)
    n_used = (pends[-1] // MOE_BLOCK).astype(jnp.int32).reshape(1)
    return block_exp, n_used, dest, pends.astype(jnp.int32), n_blocks * MOE_BLOCK


def _moe(h2_parts, logits, w13, w2):
    eid, rank, wts, counts = _route(logits)
    block_exp, n_used, pos, pends, n_slots = _dispatch_plan(eid, rank, counts)
    nsub = w2.shape[2] // LANES
    tm = functools.reduce(math.gcd, [h.shape[0] // nsub for h in h2_parts], MOE_COPY_TOKENS)
    xs = _dispatch(h2_parts, pos, pends, n_used, n_slots=n_slots, tm=tm, nsub=nsub)
    ys = _experts(xs, block_exp, n_used, w13, w2)
    return ys, pos, wts


def _blockdiag(w):
    G, n, _ = w.shape
    eye = jnp.eye(G, dtype=w.dtype)
    return (eye[:, None, :, None] * w[:, :, None, :]).reshape(G * n, G * n)


@functools.lru_cache(maxsize=None)
def _rope_tables(S, qk_dim):
    half = qk_dim // 2
    nf = half // 2
    rows_n = S // GRID_W
    row = np.repeat(np.arange(rows_n, dtype=np.float32), GRID_W)
    col = np.tile(np.arange(GRID_W, dtype=np.float32), rows_n)
    freqs = (np.float32(ROPE_BASE) ** (-np.arange(nf, dtype=np.float32) / np.float32(nf))).astype(np.float32)
    ang_r = (row[:, None] * freqs).astype(np.float64)
    ang_c = (col[:, None] * freqs).astype(np.float64)
    cos = np.concatenate([np.cos(ang_r)] * 2 + [np.cos(ang_c)] * 2, axis=1)
    sin = np.concatenate([-np.sin(ang_r), np.sin(ang_r), -np.sin(ang_c), np.sin(ang_c)], axis=1)
    reps = LANES // qk_dim
    return np.tile(cos, (1, reps)).astype(np.float32), np.tile(sin, (1, reps)).astype(np.float32)


def kernel(x, c, ctx, c_ctx, w_mod, b_mod, norm1_g, norm2_g, w_in, q_norm_g, k_norm_g, lambda_q1, lambda_k1, lambda_q2, lambda_k2, subln_g, conv_w, conv_b, gate_a_w, gate_a_b, gate_x_w, gate_x_b, lru_lambda, w_out, w_group, b_group, w_router, b_router, w1, w3, w2):
    B, S, D = x.shape
    C = ctx.shape[1]
    L = w_mod.shape[0]
    qk_dim = q_norm_g.shape[1]
    fw = lw = D // 4
    aw = D // 2
    dims = (fw, aw, lw, qk_dim)
    tm_x, tm_c = min(512, S), min(512, C)
    n_ctx, n_lat = B * C, B * S

    n_rows = -(-(B + 1) // SUBLANES) * SUBLANES
    c_all = jnp.concatenate([c, c_ctx[None, :], jnp.zeros((n_rows - B - 1, D), F32)], axis=0)
    mod = _modulation(c_all, w_mod, b_mod)

    cos_t, sin_t = (jnp.asarray(t) for t in _rope_tables(S, qk_dim))
    dummy_tab = jnp.zeros((C, LANES), F32)
    four_tabs = _fourier_tables(S, fw)
    gmat = _blockdiag(jnp.ones((256 // qk_dim, qk_dim, qk_dim), F32)).astype(BF16)

    xc = ctx
    for l in range(L):
        last = l == L - 1
        lam_init = 0.8 - 0.6 * math.exp(-0.3 * l)
        m = [mod[l, :, i * D:(i + 1) * D] for i in range(N_MOD)]
        mx = [a[:B, None, :] for a in m]
        mc = [jnp.broadcast_to(a[B][None, None, :], (B, 1, D)) for a in m]
        w_in_b = w_in[l].astype(BF16)
        w_out_b = w_out[l].astype(BF16)
        gqk = jnp.concatenate([jnp.tile(q_norm_g[l], aw // qk_dim),
                               jnp.tile(k_norm_g[l], aw // qk_dim)])[None, :]
        lam = (jnp.exp(jnp.sum(lambda_q1[l] * lambda_k1[l])) - jnp.exp(jnp.sum(lambda_q2[l] * lambda_k2[l]))
               + lam_init).astype(F32)
        s_bound = (ATT_BOUND_MARGIN * qk_dim ** 0.5 * LOG2E
                   * jnp.max(jnp.abs(q_norm_g[l])) * jnp.max(jnp.abs(k_norm_g[l]))).astype(F32)
        att_sc = jnp.stack([lam, s_bound])
        sub_g = subln_g[l][None, :]
        n1 = norm1_g[l][None, :]
        n2 = norm2_g[l][None, :]
        lru_p = {
            'conv_w': conv_w[l], 'conv_b': conv_b[l][None, :],
            'wa': [_blockdiag(gate_a_w[l, d]).astype(BF16) for d in range(2)],
            'wx': [_blockdiag(gate_x_w[l, d]).astype(BF16) for d in range(2)],
            'ba': [gate_a_b[l, d][None, :] for d in range(2)],
            'bx': [gate_x_b[l, d][None, :] for d in range(2)],
            'nsp': [(-LRU_C * jax.nn.softplus(-lru_lambda[l, d]))[None, :] for d in range(2)],
        }
        pad = ROUTE_PAD - N_GROUPS - N_EXPERTS
        w_rt = jnp.concatenate([w_group[l], w_router[l], jnp.zeros((D, pad), F32)], axis=1).astype(BF16)
        b_rt = jnp.concatenate([b_group[l], b_router[l], jnp.zeros((pad,), F32)])[None, :]
        w13 = jnp.concatenate([w1[l], w3[l]], axis=-1).astype(BF16)
        w2_b = w2[l].astype(BF16)

        ufc, qc, ktc, vc, uyc, urc = _inproj(xc, mc[0], mc[1], n1, w_in_b, gqk, gmat, dummy_tab, dummy_tab,
                                             dims=dims, use_rope=False, tm=tm_c, name="inproj_ctx")
        ufx, qx, ktx, vx, uyx, urx = _inproj(x, mx[0], mx[1], n1, w_in_b, gqk, gmat, cos_t, sin_t,
                                             dims=dims, use_rope=True, tm=tm_x, name="inproj_lat")
        grp_x = min(ATT_BLOCKS_PER_BODY, ktx.shape[1])
        att_srcs = [(ktc, vc, ktc.shape[1]), (ktx, vx, grp_x)]
        att_fn = functools.partial(_attention, att_sc, qx, att_srcs, sub_g, out_scale=1.0 - lam_init,
                                   tq=min(ATT_TQ, S))
        att_x = lax.cond(2.0 * s_bound <= ATT_BOUND_MAX_SPAN,
                         lambda: att_fn(online_max=False, name="attn_lat_bound"),
                         lambda: att_fn(online_max=True, name="attn_lat_online"))

        zeros_h = jnp.zeros((B, lw), F32)
        tc_c, tc_x = min(512, C), min(512, S)
        hc_f, hc_f_last = _lru_scan(urc, zeros_h, lru_p, 0, reverse=False, tc=tc_c)
        hx_f, _ = _lru_scan(urx, hc_f_last, lru_p, 0, reverse=False, tc=tc_x)
        rec_c, hc_b_first = _lru_scan(urc, zeros_h, lru_p, 1, reverse=True, hf=hc_f, uy=uyc, tc=tc_c)
        rec_x, _ = _lru_scan(urx, hc_b_first, lru_p, 1, reverse=True, hf=hx_f, uy=uyx, tc=tc_x)

        four_x = _fourier_long(ufx, four_tabs)

        if last:
            x1, h2, logits = _outproj(four_x, att_x, rec_x, x, mx[2], mx[3], mx[4], n2, w_out_b, w_rt, b_rt, tm=tm_x)
            ys, pos, wts = _moe([h2], logits, w13, w2_b)
            x = _combine(x1, mx[5], ys, pos, wts, row_off=0, tm=min(MOE_COPY_TOKENS, S))
        else:
            att_c = _attention(att_sc, qc, [(ktc, vc, ktc.shape[1])], sub_g, out_scale=1.0 - lam_init,
                               tq=min(256, C), online_max=True, name="attn_ctx")
            four_c = _fourier_short(ufc, four_tabs[2], four_tabs[3])
            xc1, h2c, lgc = _outproj(four_c, att_c, rec_c, xc, mc[2], mc[3], mc[4], n2, w_out_b, w_rt, b_rt, tm=tm_c)
            x1, h2x, lgx = _outproj(four_x, att_x, rec_x, x, mx[2], mx[3], mx[4], n2, w_out_b, w_rt, b_rt, tm=tm_x)
            ys, pos, wts = _moe([h2c, h2x], jnp.concatenate([lgc, lgx], axis=0), w13, w2_b)
            tmc = min(MOE_COPY_TOKENS, S, C)
            xc = _combine(xc1, mc[5], ys, pos, wts, row_off=0, tm=tmc)
            x = _combine(x1, mx[5], ys, pos, wts, row_off=n_ctx, tm=tmc)
    return x
```

```python
import functools
import math

import jax
import jax.numpy as jnp
import numpy as np
from jax import lax
from jax.experimental import pallas as pl
from jax.experimental.pallas import tpu as pltpu

F32 = jnp.float32
BF16 = jnp.bfloat16

GRID_W = 64
F_GROUPS = 4
ATT_HEADS = 4
LRU_BLOCKS = 4
LRU_C = 8.0
CONV_W = 4
CONV_LEFT = (CONV_W - 1) // 2
N_GROUPS = 4
EXPERTS_PER_GROUP = 8
N_EXPERTS = N_GROUPS * EXPERTS_PER_GROUP
TOP_K = 2
MOE_BLOCK = 512
MOE_COPY_TOKENS = 256
ROUTE_TOKENS = 512
ROUTE_OUT = 8
N_MOD = 6
EPS = 1e-6
ROPE_BASE = 10000.0
LOG2E = math.log2(math.e)
ATT_BOUND_MARGIN = 1.02
ATT_BOUND_MAX_SPAN = 100.0

LANES = 128
SUBLANES = 8
VMEM_LIMIT = 48 * 1024 * 1024
DFT_T1 = 64
DFT_K1_PER_STEP = 8
ROUTE_PAD = 128
ATT_TQ = 512
ATT_BLOCKS_PER_BODY = 16


def _cparams(*sem):
    return pltpu.CompilerParams(dimension_semantics=sem, vmem_limit_bytes=VMEM_LIMIT)


def _mod_kernel(c_ref, w_ref, b_ref, o_ref):
    c = c_ref[...]
    s = c * jax.nn.sigmoid(c)
    o_ref[0] = jnp.dot(s, w_ref[0], preferred_element_type=F32,
                       precision=lax.Precision.HIGHEST) + b_ref[0]


def _modulation(c_all, w_mod, b_mod):
    L, D, n6 = w_mod.shape
    R = c_all.shape[0]
    tn = n6 // 4
    return pl.pallas_call(
        _mod_kernel,
        grid=(L, n6 // tn),
        in_specs=[pl.BlockSpec((R, D), lambda l, j: (0, 0)),
                  pl.BlockSpec((1, D, tn), lambda l, j: (l, 0, j)),
                  pl.BlockSpec((1, 1, tn), lambda l, j: (l, 0, j))],
        out_specs=pl.BlockSpec((1, R, tn), lambda l, j: (l, 0, j)),
        out_shape=jax.ShapeDtypeStruct((L, R, n6), F32),
        compiler_params=_cparams("arbitrary", "arbitrary"),
        name="modulation",
    )(c_all, w_mod, b_mod.reshape(L, 1, n6))


def _rms_mod(x, g, sc, sh):
    ms = jnp.mean(x * x, axis=-1, keepdims=True)
    return (x * lax.rsqrt(ms + EPS)) * g * (1.0 + sc) + sh


def _inproj_kernel(x_ref, sh_ref, sc_ref, g_ref, w_ref, gqk_ref, gmat_ref, cos_ref, sin_ref,
                   uf_ref, q_ref, kt_ref, v_ref, uy_ref, ur_ref, *, dims, use_rope):
    fw, aw, lw, qk_dim = dims
    q_off, k_off, v_off = fw, fw + aw, fw + 2 * aw
    y_off, r_off = v_off + aw, v_off + aw + lw
    h = _rms_mod(x_ref[0], g_ref[...], sc_ref[0], sh_ref[0])
    u = jnp.dot(h.astype(BF16), w_ref[...], preferred_element_type=F32)
    uf_ref[0] = u[:, :fw].astype(BF16)
    qk = u[:, q_off:v_off]
    sq = qk * qk
    hi = sq.astype(BF16)
    lo = (sq - hi.astype(F32)).astype(BF16)
    gm = gmat_ref[...]
    gw = gm.shape[0]
    parts = []
    for s in range(2 * aw // gw):
        sl = slice(s * gw, (s + 1) * gw)
        parts.append(jnp.dot(hi[:, sl], gm, preferred_element_type=F32)
                     + jnp.dot(lo[:, sl], gm, preferred_element_type=F32))
    msq = jnp.concatenate(parts, axis=1) * (1.0 / qk_dim)
    n = qk * lax.rsqrt(msq + EPS) * gqk_ref[...]
    if use_rope:
        reps = 2 * aw // LANES
        cos = jnp.concatenate([cos_ref[...]] * reps, axis=1)
        sin = jnp.concatenate([sin_ref[...]] * reps, axis=1)
        width = n.shape[1]
        half = qk_dim // 4
        lane = lax.broadcasted_iota(jnp.int32, n.shape, 1)
        swapped = jnp.where((lane % (2 * half)) < half,
                            pltpu.roll(n, width - half, 1), pltpu.roll(n, half, 1))
        n = n * cos + swapped * sin
    q_ref[0] = (n[:, :aw] * (qk_dim ** -0.5 * LOG2E)).astype(BF16)
    kt_ref[0, 0] = n[:, aw:].T.astype(BF16)
    v_ref[0] = u[:, v_off:y_off].astype(BF16)
    uy_ref[0] = u[:, y_off:r_off]
    ur_ref[0] = u[:, r_off:]


def _inproj(x, sh, sc, g, w_in, gqk, gmat, cos, sin, *, dims, use_rope, tm, name):
    B, T, D = x.shape
    fw, aw, lw, _ = dims
    n_in = w_in.shape[1]
    per_b = lambda b, i: (b, 0, 0)
    const = lambda b, i: (0, 0)
    tile = lambda b, i: (b, i, 0)
    return pl.pallas_call(
        functools.partial(_inproj_kernel, dims=dims, use_rope=use_rope),
        grid=(B, T // tm),
        in_specs=[pl.BlockSpec((1, tm, D), tile),
                  pl.BlockSpec((1, 1, D), per_b), pl.BlockSpec((1, 1, D), per_b),
                  pl.BlockSpec((1, D), const),
                  pl.BlockSpec((D, n_in), const),
                  pl.BlockSpec((1, 2 * aw), const),
                  pl.BlockSpec(gmat.shape, const),
                  pl.BlockSpec((tm, LANES), lambda b, i: (i, 0)),
                  pl.BlockSpec((tm, LANES), lambda b, i: (i, 0))],
        out_specs=[pl.BlockSpec((1, tm, fw), tile),
                   pl.BlockSpec((1, tm, aw), tile),
                   pl.BlockSpec((1, 1, aw, tm), lambda b, i: (b, i, 0, 0)),
                   pl.BlockSpec((1, tm, aw), tile),
                   pl.BlockSpec((1, tm, lw), tile),
                   pl.BlockSpec((1, tm, lw), tile)],
        out_shape=[jax.ShapeDtypeStruct((B, T, fw), BF16),
                   jax.ShapeDtypeStruct((B, T, aw), BF16),
                   jax.ShapeDtypeStruct((B, T // tm, aw, tm), BF16),
                   jax.ShapeDtypeStruct((B, T, aw), BF16),
                   jax.ShapeDtypeStruct((B, T, lw), F32),
                   jax.ShapeDtypeStruct((B, T, lw), F32)],
        compiler_params=_cparams("arbitrary", "arbitrary"),
        name=name,
    )(x, sh, sc, g, w_in, gqk, gmat, cos, sin)


def _attn_kernel(*refs, n_src, groups, out_scale, online_max):
    sc_ref, q_ref = refs[0], refs[1]
    kv_refs = refs[2:2 + 2 * n_src]
    g_ref, o_ref = refs[2 + 2 * n_src], refs[3 + 2 * n_src]
    q = q_ref[0].astype(F32)
    tq, w = q.shape
    lane = lax.broadcasted_iota(jnp.int32, q.shape, 1)
    qq = jnp.concatenate([jnp.where(lane < w // 2, q, 0.0),
                          jnp.where(lane >= w // 2, q, 0.0)], axis=0).astype(BF16)
    vd = kv_refs[1].shape[-1]

    if online_max:
        def step(kt, v, carry):
            m, l, acc = carry
            s = jnp.dot(qq, kt, preferred_element_type=F32)
            m_new = jnp.maximum(m, jnp.max(s, axis=-1, keepdims=True))
            alpha = jnp.exp2(m - m_new)
            p = jnp.exp2(s - m_new)
            l = alpha * l + jnp.sum(p, axis=-1, keepdims=True)
            acc = alpha * acc + jnp.dot(p.astype(BF16), v, preferred_element_type=F32)
            return m_new, l, acc

        carry = (jnp.full((2 * tq, 1), -jnp.inf, F32), jnp.zeros((2 * tq, 1), F32),
                 jnp.zeros((2 * tq, vd), F32))
    else:
        bound = sc_ref[1]

        def step(kt, v, carry):
            l_part, acc = carry
            s = jnp.dot(qq, kt, preferred_element_type=F32)
            p = jnp.exp2(s - bound)
            for c in range(s.shape[1] // LANES):
                l_part = l_part + p[:, c * LANES:(c + 1) * LANES]
            acc = acc + jnp.dot(p.astype(BF16), v, preferred_element_type=F32)
            return l_part, acc

        carry = (jnp.zeros((2 * tq, LANES), F32), jnp.zeros((2 * tq, vd), F32))

    for n in range(n_src):
        kt_ref, v_ref, grp = kv_refs[2 * n], kv_refs[2 * n + 1], groups[n]
        nblk, kb = kt_ref.shape[1], kt_ref.shape[3]
        if nblk == grp:
            for g in range(grp):
                carry = step(kt_ref[0, g], v_ref[0, g * kb:(g + 1) * kb, :], carry)
        else:
            def body(j, carry, kt_ref=kt_ref, v_ref=v_ref, grp=grp, kb=kb):
                for g in range(grp):
                    blk = j * grp + g
                    carry = step(kt_ref[0, blk], v_ref[0, pl.ds(pl.multiple_of(blk * kb, kb), kb), :], carry)
                return carry
            carry = lax.fori_loop(0, nblk // grp, body, carry)
    if online_max:
        _, l, acc = carry
    else:
        l_part, acc = carry
        l = jnp.sum(l_part, axis=-1, keepdims=True)
    o = acc / l
    d = o[:tq] - sc_ref[0] * o[tq:]
    ms = jnp.mean(d * d, axis=-1, keepdims=True)
    o_ref[0] = (d * lax.rsqrt(ms + EPS) * g_ref[...] * out_scale).astype(BF16)


def _attention(scalars, q, srcs, subln_g, *, out_scale, tq, online_max, name):
    B, S, aw = q.shape
    hd = aw // ATT_HEADS
    in_specs = [pl.BlockSpec(memory_space=pltpu.SMEM),
                pl.BlockSpec((1, tq, hd), lambda b, h, i: (b, i, h))]
    args = [scalars, q]
    for kt, v, _ in srcs:
        in_specs += [pl.BlockSpec((1, kt.shape[1], hd, kt.shape[3]), lambda b, h, i: (b, 0, h, 0)),
                     pl.BlockSpec((1, v.shape[1], hd), lambda b, h, i: (b, 0, h))]
        args += [kt, v]
    in_specs.append(pl.BlockSpec((1, hd), lambda b, h, i: (0, 0)))
    args.append(subln_g)
    return pl.pallas_call(
        functools.partial(_attn_kernel, n_src=len(srcs), groups=tuple(g for _, _, g in srcs),
                          out_scale=out_scale, online_max=online_max),
        grid=(B, ATT_HEADS, S // tq),
        in_specs=in_specs,
        out_specs=pl.BlockSpec((1, tq, hd), lambda b, h, i: (b, i, h)),
        out_shape=jax.ShapeDtypeStruct((B, S, aw), BF16),
        compiler_params=_cparams("arbitrary", "arbitrary", "arbitrary"),
        name=name,
    )(*args)


def _dft1_kernel(m_ref, z_ref, y_ref):
    y_ref[0] = jnp.dot(m_ref[...], z_ref[0], preferred_element_type=F32).astype(BF16)


def _dft2_kernel(y_ref, tab_ref, cc_ref, sc_ref, o_ref, *, scale):
    w = cc_ref.shape[0]
    for i in range(tab_ref.shape[0]):
        y = jnp.concatenate([y_ref[0, 0, i], y_ref[0, 1, i]], axis=0)
        zr = jnp.dot(tab_ref[i, 0], y, preferred_element_type=F32)
        zi = jnp.dot(tab_ref[i, 1], y, preferred_element_type=F32)
        o = (jnp.dot(zr.astype(BF16), cc_ref[...], preferred_element_type=F32)
             + jnp.dot(zi.astype(BF16), sc_ref[...], preferred_element_type=F32))
        o_ref[0, :, i * w:(i + 1) * w] = (o * scale).astype(BF16)


@functools.lru_cache(maxsize=None)
def _fourier_tables(T, fw):
    t1n, t2n = DFT_T1, T // DFT_T1
    gd = fw // F_GROUPS
    two_pi = 2.0 * np.pi
    k1 = np.arange(t1n, dtype=np.int64)
    a1 = two_pi * ((k1[:, None] * k1[None, :]) % t1n) / t1n
    m1 = np.concatenate([np.cos(a1), -np.sin(a1)], axis=0)
    k2 = np.arange(t2n, dtype=np.int64)
    kk = k1[:, None, None] + t1n * k2[None, :, None]
    ph = two_pi * ((kk * k2[None, None, :]) % T) / T
    cp, sp = np.cos(ph), np.sin(ph)
    tab = np.stack([np.concatenate([cp, sp], axis=-1),
                    np.concatenate([-sp, cp], axis=-1)], axis=1)
    c = np.arange(gd, dtype=np.int64)
    ac = two_pi * ((c[:, None] * c[None, :]) % gd) / gd
    eye = np.eye(F_GROUPS)
    cc, sc = np.kron(eye, np.cos(ac)), np.kron(eye, np.sin(ac))
    return tuple(np.asarray(t, np.float32) for t in (m1, tab, cc, sc))


def _fourier_long(uf, tables):
    B, T, W = uf.shape
    m1, tab, cc, sc = (jnp.asarray(t, BF16) for t in tables)
    t1n, t2n = DFT_T1, T // DFT_T1
    ncol = t2n * W
    tn = min(ncol, 4096)
    y = pl.pallas_call(
        _dft1_kernel,
        grid=(B, ncol // tn),
        in_specs=[pl.BlockSpec((2 * t1n, t1n), lambda b, j: (0, 0)),
                  pl.BlockSpec((1, t1n, tn), lambda b, j: (b, 0, j))],
        out_specs=pl.BlockSpec((1, 2 * t1n, tn), lambda b, j: (b, 0, j)),
        out_shape=jax.ShapeDtypeStruct((B, 2 * t1n, ncol), BF16),
        compiler_params=_cparams("arbitrary", "arbitrary"),
        name="dft_stage1",
    )(m1, uf.reshape(B, t1n, ncol))
    y5 = y.reshape(B, 2, t1n, t2n, W)
    scale = 1.0 / math.sqrt(T * (W // F_GROUPS))
    out = pl.pallas_call(
        functools.partial(_dft2_kernel, scale=scale),
        grid=(t1n // DFT_K1_PER_STEP, B),
        in_specs=[pl.BlockSpec((1, 2, DFT_K1_PER_STEP, t2n, W), lambda k, b: (b, 0, k, 0, 0)),
                  pl.BlockSpec((DFT_K1_PER_STEP, 2, t2n, 2 * t2n), lambda k, b: (k, 0, 0, 0)),
                  pl.BlockSpec((W, W), lambda k, b: (0, 0)),
                  pl.BlockSpec((W, W), lambda k, b: (0, 0))],
        out_specs=pl.BlockSpec((1, t2n, DFT_K1_PER_STEP * W), lambda k, b: (b, 0, k)),
        out_shape=jax.ShapeDtypeStruct((B, t2n, t1n * W), BF16),
        compiler_params=_cparams("arbitrary", "arbitrary"),
        name="dft_stage2",
    )(y5, tab, cc, sc)
    return out.reshape(B, T, W)


def _dft_short_kernel(z_ref, ct_ref, st_ref, cc_ref, sc_ref, o_ref, *, scale):
    z = z_ref[0]
    zc = jnp.dot(z, cc_ref[...], preferred_element_type=F32).astype(BF16)
    zs = jnp.dot(z, sc_ref[...], preferred_element_type=F32).astype(BF16)
    o = (jnp.dot(ct_ref[...], zc, preferred_element_type=F32)
         - jnp.dot(st_ref[...], zs, preferred_element_type=F32))
    o_ref[0] = (o * scale).astype(BF16)


def _fourier_short(uf, cc, sc):
    B, T, W = uf.shape
    t = np.arange(T, dtype=np.int64)
    ang = 2.0 * np.pi * ((t[:, None] * t[None, :]) % T) / T
    ct, st = jnp.asarray(np.cos(ang), BF16), jnp.asarray(np.sin(ang), BF16)
    cc, sc = jnp.asarray(cc, BF16), jnp.asarray(sc, BF16)
    scale = 1.0 / math.sqrt(T * (W // F_GROUPS))
    return pl.pallas_call(
        functools.partial(_dft_short_kernel, scale=scale),
        grid=(B,),
        in_specs=[pl.BlockSpec((1, T, W), lambda b: (b, 0, 0)),
                  pl.BlockSpec((T, T), lambda b: (0, 0)), pl.BlockSpec((T, T), lambda b: (0, 0)),
                  pl.BlockSpec((W, W), lambda b: (0, 0)), pl.BlockSpec((W, W), lambda b: (0, 0))],
        out_specs=pl.BlockSpec((1, T, W), lambda b: (b, 0, 0)),
        out_shape=jax.ShapeDtypeStruct((B, T, W), BF16),
        compiler_params=_cparams("arbitrary"),
        name="dft_short",
    )(uf, ct, st, cc, sc)


def _affine_scan(a, b, reverse):
    T = a.shape[0]
    row = lax.broadcasted_iota(jnp.int32, a.shape, 0)
    k = 1
    while k < T:
        if k % SUBLANES == 0:
            one, zero = jnp.ones((k,) + a.shape[1:], a.dtype), jnp.zeros((k,) + a.shape[1:], a.dtype)
            if reverse:
                a_s, b_s = jnp.concatenate([a[k:], one], axis=0), jnp.concatenate([b[k:], zero], axis=0)
            else:
                a_s, b_s = jnp.concatenate([one, a[:T - k]], axis=0), jnp.concatenate([zero, b[:T - k]], axis=0)
        else:
            if reverse:
                a_s, b_s, valid = pltpu.roll(a, T - k, 0), pltpu.roll(b, T - k, 0), row < T - k
            else:
                a_s, b_s, valid = pltpu.roll(a, k, 0), pltpu.roll(b, k, 0), row >= k
            a_s, b_s = jnp.where(valid, a_s, 1.0), jnp.where(valid, b_s, 0.0)
        b = a * b_s + b
        a = a * a_s
        k *= 2
    return a, b


def _lru_kernel(*refs, reverse, combine, nt):
    if combine:
        (ur_ref, prev_ref, next_ref, h0_ref, cw_ref, cb_ref, wa_ref, ba_ref, wx_ref, bx_ref,
         nsp_ref, hf_ref, uy_ref, out_ref, hlast_ref, carry_ref) = refs
    else:
        (ur_ref, prev_ref, next_ref, h0_ref, cw_ref, cb_ref, wa_ref, ba_ref, wx_ref, bx_ref,
         nsp_ref, out_ref, hlast_ref, carry_ref) = refs
    i = pl.program_id(1)
    ci = nt - 1 - i if reverse else i

    @pl.when(i == 0)
    def _():
        carry_ref[...] = h0_ref[0]

    u = ur_ref[0]
    tc = u.shape[0]
    prev = jnp.where(ci == 0, 0.0, prev_ref[0])
    nxt = jnp.where(ci == nt - 1, 0.0, next_ref[0])
    ext = jnp.concatenate([prev, u, nxt], axis=0)
    n_ext = ext.shape[0]
    xr = cb_ref[...] + u * cw_ref[CONV_LEFT:CONV_LEFT + 1, :]
    for k in range(CONV_W):
        d = k - CONV_LEFT
        if d != 0:
            shifted = pltpu.roll(ext, (-d) % n_ext, 0)[SUBLANES:SUBLANES + tc]
            xr = xr + shifted * cw_ref[k:k + 1, :]
    xb = xr.astype(BF16)
    r = jax.nn.sigmoid(jnp.dot(xb, wa_ref[...], preferred_element_type=F32) + ba_ref[...])
    g = jax.nn.sigmoid(jnp.dot(xb, wx_ref[...], preferred_element_type=F32) + bx_ref[...])
    log_a = r * nsp_ref[...]
    a = jnp.exp(log_a)
    bt = jnp.sqrt(-jnp.tanh(log_a) * (a * a + 1.0)) * (g * xr)
    a_cum, b_cum = _affine_scan(a, bt, reverse)
    h = a_cum * carry_ref[...] + b_cum
    last = h[0:1] if reverse else h[tc - 1:tc]
    carry_ref[...] = last
    hlast_ref[0] = last
    if combine:
        out_ref[0] = (jax.nn.gelu(uy_ref[0]) * (hf_ref[0] + h)).astype(BF16)
    else:
        out_ref[0] = h


def _lru_scan(ur, h0, p, d, *, reverse, hf=None, uy=None, tc):
    B, T, W = ur.shape
    nt = T // tc
    hb = tc // SUBLANES
    nh = T // SUBLANES
    combine = hf is not None
    cidx = (lambda i: nt - 1 - i) if reverse else (lambda i: i)
    tile = lambda b, i: (b, cidx(i), 0)
    const = lambda b, i: (0, 0)
    in_specs = [pl.BlockSpec((1, tc, W), tile),
                pl.BlockSpec((1, SUBLANES, W), lambda b, i: (b, jnp.maximum(cidx(i) * hb - 1, 0), 0)),
                pl.BlockSpec((1, SUBLANES, W), lambda b, i: (b, jnp.minimum((cidx(i) + 1) * hb, nh - 1), 0)),
                pl.BlockSpec((1, 1, W), lambda b, i: (b, 0, 0)),
                pl.BlockSpec((CONV_W, W), const), pl.BlockSpec((1, W), const),
                pl.BlockSpec((W, W), const), pl.BlockSpec((1, W), const),
                pl.BlockSpec((W, W), const), pl.BlockSpec((1, W), const),
                pl.BlockSpec((1, W), const)]
    args = [ur, ur, ur, h0.reshape(B, 1, W), p['conv_w'], p['conv_b'],
            p['wa'][d], p['ba'][d], p['wx'][d], p['bx'][d], p['nsp'][d]]
    if combine:
        in_specs += [pl.BlockSpec((1, tc, W), tile), pl.BlockSpec((1, tc, W), tile)]
        args += [hf, uy]
    out, hlast = pl.pallas_call(
        functools.partial(_lru_kernel, reverse=reverse, combine=combine, nt=nt),
        grid=(B, nt),
        in_specs=in_specs,
        out_specs=[pl.BlockSpec((1, tc, W), tile), pl.BlockSpec((1, 1, W), lambda b, i: (b, 0, 0))],
        out_shape=[jax.ShapeDtypeStruct((B, T, W), BF16 if combine else F32),
                   jax.ShapeDtypeStruct((B, 1, W), F32)],
        scratch_shapes=[pltpu.VMEM((1, W), F32)],
        compiler_params=_cparams("arbitrary", "arbitrary"),
        name=("lru_bwd" if reverse else "lru_fwd"),
    )(*args)
    return out, hlast.reshape(B, W)


def _store_token_tiles(ref, val, lead=()):
    rows, width = val.shape
    nsub = width // LANES
    for j in range(nsub):
        ref[lead + (pl.ds(j, rows, stride=nsub), slice(None))] = val[:, j * LANES:(j + 1) * LANES]


def _load_token_tiles(ref, rows, nsub, lead=()):
    return jnp.concatenate([ref[lead + (pl.ds(j, rows, stride=nsub), slice(None))] for j in range(nsub)], axis=1)


def _outproj_kernel(f_ref, a_ref, r_ref, x_ref, g1_ref, sh_ref, sc_ref, ng_ref, w_ref, wr_ref, br_ref,
                    x1_ref, h2_ref, lg_ref, *, fw, aw):
    y = (jnp.dot(f_ref[0], w_ref[0:fw, :], preferred_element_type=F32)
         + jnp.dot(a_ref[0], w_ref[fw:fw + aw, :], preferred_element_type=F32)
         + jnp.dot(r_ref[0], w_ref[fw + aw:, :], preferred_element_type=F32))
    x1 = x_ref[0] + g1_ref[0] * y
    x1_ref[0] = x1
    h2 = _rms_mod(x1, ng_ref[...], sc_ref[0], sh_ref[0])
    _store_token_tiles(h2_ref, h2, lead=(0,))
    lg_ref[0] = jnp.dot(h2.astype(BF16), wr_ref[...], preferred_element_type=F32) + br_ref[...]


def _outproj(four, att, rec, x, g1, sh2, sc2, ng, w_out, w_rt, b_rt, *, tm):
    B, T, D = x.shape
    fw, aw = four.shape[2], att.shape[2]
    nsub = D // LANES
    tile = lambda b, i: (b, i, 0)
    per_b = lambda b, i: (b, 0, 0)
    const = lambda b, i: (0, 0)
    x1, h2, logits = pl.pallas_call(
        functools.partial(_outproj_kernel, fw=fw, aw=aw),
        grid=(B, T // tm),
        in_specs=[pl.BlockSpec((1, tm, fw), tile), pl.BlockSpec((1, tm, aw), tile),
                  pl.BlockSpec((1, tm, rec.shape[2]), tile), pl.BlockSpec((1, tm, D), tile),
                  pl.BlockSpec((1, 1, D), per_b), pl.BlockSpec((1, 1, D), per_b),
                  pl.BlockSpec((1, 1, D), per_b), pl.BlockSpec((1, D), const),
                  pl.BlockSpec((D, D), const), pl.BlockSpec((D, ROUTE_PAD), const),
                  pl.BlockSpec((1, ROUTE_PAD), const)],
        out_specs=[pl.BlockSpec((1, tm, D), tile), pl.BlockSpec((1, tm * nsub, LANES), tile),
                   pl.BlockSpec((1, tm, ROUTE_PAD), tile)],
        out_shape=[jax.ShapeDtypeStruct((B, T, D), F32),
                   jax.ShapeDtypeStruct((B, T * nsub, LANES), F32),
                   jax.ShapeDtypeStruct((B, T, ROUTE_PAD), F32)],
        compiler_params=_cparams("arbitrary", "arbitrary"),
        name="outproj",
    )(four, att, rec, x, g1, sh2, sc2, ng, w_out, w_rt, b_rt)
    return x1, h2.reshape(B * T * nsub, LANES), logits.reshape(B * T, ROUTE_PAD)


def _dispatch_kernel(pe_ref, nu_ref, pos_ref, *refs, part_tiles, nsub, n_blocks):
    n_parts = len(part_tiles)
    h_refs = refs[:n_parts]
    xs_ref, zbuf, ring, sem_rows, sem_load, zsem = refs[n_parts:]
    i = pl.program_id(0)
    n_steps = pl.num_programs(0)
    blk = MOE_BLOCK * nsub
    rows = ring.shape[1]
    tm = rows // nsub

    def zero_copy(row0):
        return pltpu.make_async_copy(zbuf, xs_ref.at[pl.ds(pl.multiple_of(row0, blk), blk)], zsem)

    def load(step, slot, start):
        first = 0
        for h_ref, nt in zip(h_refs, part_tiles):
            @pl.when((step >= first) & (step < first + nt))
            def _(h_ref=h_ref, first=first):
                src = h_ref.at[pl.ds(pl.multiple_of((step - first) * rows, rows), rows)]
                cp = pltpu.make_async_copy(src, ring.at[slot], sem_load.at[slot])
                if start:
                    cp.start()
                else:
                    cp.wait()
            first += nt

    def drain_rows(slot):
        for k in range(TOP_K):
            pltpu.make_async_copy(ring.at[slot], xs_ref.at[pl.ds(0, rows)], sem_rows.at[slot]).wait()

    @pl.when(i == 0)
    def _():
        load(0, 0, True)

        @pl.when(n_steps > 1)
        def _():
            load(1, 1, True)
        zbuf[...] = jnp.zeros_like(zbuf)

        def expert_block(e, start, wait):
            pend = pe_ref[e]

            @pl.when(pend > start)
            def _():
                if wait:
                    zero_copy(0).wait()
                else:
                    zero_copy((pend - MOE_BLOCK) * nsub).start()
            return pend

        def tail_block(b, c, wait):
            if wait:
                zero_copy(0).wait()
            else:
                zero_copy(b * blk).start()
            return c

        for wait in (False, True):
            lax.fori_loop(0, N_EXPERTS, functools.partial(expert_block, wait=wait), 0)
            lax.fori_loop(nu_ref[0], n_blocks, functools.partial(tail_block, wait=wait), 0)

    slot = i % 3

    @pl.when(i >= 1)
    def _():
        drain_rows((i + 2) % 3)

    @pl.when(i + 2 < n_steps)
    def _():
        load(i + 2, (i + 2) % 3, True)

    load(i, slot, False)

    def issue(t, c):
        src = ring.at[slot, pl.ds(pl.multiple_of(t * nsub, nsub), nsub)]
        for k in range(TOP_K):
            dst = pl.multiple_of(pos_ref[0, 0, TOP_K * t + k] * nsub, nsub)
            pltpu.make_async_copy(src, xs_ref.at[pl.ds(dst, nsub)], sem_rows.at[slot]).start(priority=k % 2)
        return c

    lax.fori_loop(0, tm, issue, 0, unroll=4)

    @pl.when(i == n_steps - 1)
    def _():
        drain_rows(slot)


def _dispatch(h2_parts, pos, pends, n_used, *, n_slots, tm, nsub):
    part_tiles = tuple(h.shape[0] // (nsub * tm) for h in h2_parts)
    in_specs = [pl.BlockSpec((1, 1, TOP_K * tm), lambda i, pe, nu: (i, 0, 0), memory_space=pltpu.SMEM)]
    in_specs += [pl.BlockSpec(memory_space=pl.ANY)] * len(h2_parts)
    return pl.pallas_call(
        functools.partial(_dispatch_kernel, part_tiles=part_tiles, nsub=nsub, n_blocks=n_slots // MOE_BLOCK),
        grid_spec=pltpu.PrefetchScalarGridSpec(
            num_scalar_prefetch=2,
            grid=(sum(part_tiles),),
            in_specs=in_specs,
            out_specs=pl.BlockSpec(memory_space=pl.ANY),
            scratch_shapes=[pltpu.VMEM((MOE_BLOCK * nsub, LANES), F32),
                            pltpu.VMEM((3, tm * nsub, LANES), F32),
                            pltpu.SemaphoreType.DMA((3,)), pltpu.SemaphoreType.DMA((3,)),
                            pltpu.SemaphoreType.DMA(())]),
        out_shape=jax.ShapeDtypeStruct((n_slots * nsub, LANES), F32),
        compiler_params=_cparams("arbitrary"),
        name="moe_dispatch",
    )(pends, n_used, pos.reshape(-1, 1, TOP_K * tm), *h2_parts)


def _expert_kernel(be_ref, nu_ref, x_ref, w13_ref, w2_ref, o_ref):
    i = pl.program_id(0)
    de, nsub = w2_ref.shape[1], w2_ref.shape[2] // LANES

    @pl.when(i < nu_ref[0])
    def _():
        x = _load_token_tiles(x_ref, MOE_BLOCK, nsub)
        h = jnp.dot(x.astype(BF16), w13_ref[0], preferred_element_type=F32)
        hb = jax.nn.silu(h[:, :de]) * h[:, de:]
        _store_token_tiles(o_ref, jnp.dot(hb.astype(BF16), w2_ref[0], preferred_element_type=F32))

    @pl.when(i >= nu_ref[0])
    def _():
        o_ref[...] = jnp.zeros_like(o_ref)


def _experts(xs, block_exp, n_used, w13, w2):
    de, D = w2.shape[1], w2.shape[2]
    nsub = D // LANES
    R = MOE_BLOCK
    P = xs.shape[0] // nsub
    clamp = lambda i, be, nu: (jnp.minimum(i, nu[0] - 1), 0)
    return pl.pallas_call(
        _expert_kernel,
        grid_spec=pltpu.PrefetchScalarGridSpec(
            num_scalar_prefetch=2,
            grid=(P // R,),
            in_specs=[pl.BlockSpec((R * nsub, LANES), clamp),
                      pl.BlockSpec((1, D, 2 * de), lambda i, be, nu: (be[i], 0, 0)),
                      pl.BlockSpec((1, de, D), lambda i, be, nu: (be[i], 0, 0))],
            out_specs=pl.BlockSpec((R * nsub, LANES), lambda i, be, nu: (i, 0))),
        out_shape=jax.ShapeDtypeStruct((P * nsub, LANES), F32),
        compiler_params=_cparams("arbitrary"),
        name="moe_experts",
    )(block_exp, n_used, xs, w13, w2)


def _combine_kernel(pos_ref, nxt_ref, x_ref, g_ref, w_ref, ys_ref, o_ref, buf, sems):
    tm = x_ref.shape[1]
    nsub = x_ref.shape[2] // LANES
    step = pl.program_id(0) * pl.num_programs(1) + pl.program_id(1)
    n_steps = pl.num_programs(0) * pl.num_programs(1)
    slot = step % 2

    def issue(idx_ref, into):
        def body(t, c):
            for k in range(TOP_K):
                src = pl.multiple_of(idx_ref[0, 0, TOP_K * t + k] * nsub, nsub)
                pltpu.make_async_copy(ys_ref.at[pl.ds(src, nsub)],
                                      buf.at[into, k, pl.ds(pl.multiple_of(t * nsub, nsub), nsub)],
                                      sems.at[into]).start(priority=k % 2)
            return c
        lax.fori_loop(0, tm, body, 0, unroll=4)

    @pl.when(step == 0)
    def _():
        issue(pos_ref, 0)

    @pl.when(step + 1 < n_steps)
    def _():
        issue(nxt_ref, 1 - slot)

    for k in range(TOP_K):
        pltpu.make_async_copy(ys_ref.at[pl.ds(0, tm * nsub)], buf.at[slot, k], sems.at[slot]).wait()
    w = w_ref[0]
    y = w[:, 0:1] * _load_token_tiles(buf, tm, nsub, lead=(slot, 0))
    for k in range(1, TOP_K):
        y = y + w[:, k:k + 1] * _load_token_tiles(buf, tm, nsub, lead=(slot, k))
    o_ref[0] = x_ref[0] + g_ref[0] * y


def _combine(x1, g2, ys, pos, wts, *, row_off, tm):
    B, T, D = x1.shape
    nt = T // tm
    off = row_off // tm
    last = off + B * nt - 1
    pos3 = pos.reshape(-1, 1, TOP_K * tm)
    wts3 = wts.reshape(-1, tm, TOP_K)
    return pl.pallas_call(
        _combine_kernel,
        grid=(B, nt),
        in_specs=[pl.BlockSpec((1, 1, TOP_K * tm), lambda b, i: (b * nt + i + off, 0, 0),
                               memory_space=pltpu.SMEM),
                  pl.BlockSpec((1, 1, TOP_K * tm), lambda b, i: (jnp.minimum(b * nt + i + off + 1, last), 0, 0),
                               memory_space=pltpu.SMEM),
                  pl.BlockSpec((1, tm, D), lambda b, i: (b, i, 0)),
                  pl.BlockSpec((1, 1, D), lambda b, i: (b, 0, 0)),
                  pl.BlockSpec((1, tm, TOP_K), lambda b, i: (b * nt + i + off, 0, 0)),
                  pl.BlockSpec(memory_space=pl.ANY)],
        out_specs=pl.BlockSpec((1, tm, D), lambda b, i: (b, i, 0)),
        out_shape=jax.ShapeDtypeStruct((B, T, D), F32),
        scratch_shapes=[pltpu.VMEM((2, TOP_K, tm * (D // LANES), LANES), F32), pltpu.SemaphoreType.DMA((2,))],
        compiler_params=_cparams("arbitrary", "arbitrary"),
        name="moe_combine",
    )(pos3, pos3, x1, g2, wts3, ys)


def _route_kernel(lg_ref, tri_ref, o_ref, cnt_ref, base_ref):
    @pl.when(pl.program_id(0) == 0)
    def _():
        base_ref[...] = jnp.zeros_like(base_ref)

    lg = lg_ref[...]
    lane = lax.broadcasted_iota(jnp.int32, lg.shape, 1)
    lane_f = lane.astype(F32)
    ninf = jnp.float32(-jnp.inf)
    far = jnp.float32(ROUTE_PAD)
    is_g = lane < N_GROUPS
    gl = jnp.where(is_g, lg, ninf)
    gmax = jnp.max(gl, axis=-1, keepdims=True)
    gsum = jnp.sum(jnp.where(is_g, jnp.exp(gl - gmax), 0.0), axis=-1, keepdims=True)
    p_g = 1.0 / gsum
    g_idx = jnp.min(jnp.where(gl == gmax, lane_f, far), axis=-1, keepdims=True)
    e_lane = lane - N_GROUPS
    e_lane_f = e_lane.astype(F32)
    shift = EXPERTS_PER_GROUP.bit_length() - 1
    in_grp = ((e_lane >= 0) & (e_lane < N_EXPERTS)
              & (jnp.right_shift(e_lane, shift).astype(F32) == g_idx))
    el = jnp.where(in_grp, lg, ninf)
    v1 = jnp.max(el, axis=-1, keepdims=True)
    i1 = jnp.min(jnp.where(el == v1, e_lane_f, far), axis=-1, keepdims=True)
    el2 = jnp.where(e_lane_f == i1, ninf, el)
    v2 = jnp.max(el2, axis=-1, keepdims=True)
    i2 = jnp.min(jnp.where(el2 == v2, e_lane_f, far), axis=-1, keepdims=True)
    w1 = p_g / (1.0 + jnp.exp(v2 - v1))
    w2 = p_g - w1
    hit1, hit2 = e_lane_f == i1, e_lane_f == i2
    oh1, oh2 = hit1.astype(F32), hit2.astype(F32)
    tri = tri_ref[...]
    pre1 = jnp.dot(tri, oh1.astype(BF16), preferred_element_type=F32)
    pre2 = jnp.dot(tri, oh2.astype(BF16), preferred_element_type=F32)
    tot1 = jnp.sum(oh1, axis=0, keepdims=True)
    tot2 = jnp.sum(oh2, axis=0, keepdims=True)
    base = base_ref[...]
    r1 = jnp.sum(jnp.where(hit1, pre1 + base, 0.0), axis=-1, keepdims=True)
    r2 = jnp.sum(jnp.where(hit2, pre2 + (base + tot1), 0.0), axis=-1, keepdims=True)
    base = base + tot1 + tot2
    base_ref[...] = base
    cnt_ref[...] = base
    cols = (i1, i2, r1, r2, w1, w2)
    out = jnp.zeros(lg.shape, F32)
    for j, col in enumerate(cols):
        out = jnp.where(lane == j, col, out)
    o_ref[...] = out[:, :o_ref.shape[1]]


def _route(logits):
    N = logits.shape[0]
    tm = math.gcd(ROUTE_TOKENS, N)
    tri = jnp.asarray(np.tril(np.ones((tm, tm), np.float32), -1), BF16)
    out, cnt = pl.pallas_call(
        _route_kernel,
        grid=(N // tm,),
        in_specs=[pl.BlockSpec((tm, ROUTE_PAD), lambda i: (i, 0)),
                  pl.BlockSpec((tm, tm), lambda i: (0, 0))],
        out_specs=[pl.BlockSpec((tm, ROUTE_OUT), lambda i: (i, 0)),
                   pl.BlockSpec((1, ROUTE_PAD), lambda i: (0, 0))],
        out_shape=[jax.ShapeDtypeStruct((N, ROUTE_OUT), F32),
                   jax.ShapeDtypeStruct((1, ROUTE_PAD), F32)],
        scratch_shapes=[pltpu.VMEM((1, ROUTE_PAD), F32)],
        compiler_params=_cparams("arbitrary"),
        name="moe_route",
    )(logits, tri)
    eid = out[:, 0:TOP_K].astype(jnp.int32)
    rank = out[:, TOP_K:2 * TOP_K].astype(jnp.int32)
    wts = out[:, 2 * TOP_K:3 * TOP_K]
    counts = cnt[0, N_GROUPS:N_GROUPS + N_EXPERTS].astype(jnp.int32)
    return eid, rank, wts, counts


def _dispatch_plan(eid, rank, counts):
    N = eid.shape[0]
    A = N * TOP_K
    padded = (counts + MOE_BLOCK - 1) // MOE_BLOCK * MOE_BLOCK
    pends = jnp.cumsum(padded)
    pstarts = pends - padded
    dest = pstarts[eid] + rank
    n_blocks = -(-A // MOE_BLOCK) + N_EXPERTS
    starts = jnp.arange(n_blocks, dtype=jnp.int32) * MOE_BLOCK
    block_exp = jnp.minimum(jnp.sum((pends[None, :] <= starts[:, None]).astype(jnp.int32), axis=1),
                            N_EXPERTS - 1).astype(jnp.int32)
    n_used = (pends[-1] // MOE_BLOCK).astype(jnp.int32).reshape(1)
    return block_exp, n_used, dest, pends.astype(jnp.int32), n_blocks * MOE_BLOCK


def _moe(h2_parts, logits, w13, w2):
    eid, rank, wts, counts = _route(logits)
    block_exp, n_used, pos, pends, n_slots = _dispatch_plan(eid, rank, counts)
    nsub = w2.shape[2] // LANES
    tm = functools.reduce(math.gcd, [h.shape[0] // nsub for h in h2_parts], MOE_COPY_TOKENS)
    xs = _dispatch(h2_parts, pos, pends, n_used, n_slots=n_slots, tm=tm, nsub=nsub)
    ys = _experts(xs, block_exp, n_used, w13, w2)
    return ys, pos, wts


def _blockdiag(w):
    G, n, _ = w.shape
    eye = jnp.eye(G, dtype=w.dtype)
    return (eye[:, None, :, None] * w[:, :, None, :]).reshape(G * n, G * n)


@functools.lru_cache(maxsize=None)
def _rope_tables(S, qk_dim):
    half = qk_dim // 2
    nf = half // 2
    rows_n = S // GRID_W
    row = np.repeat(np.arange(rows_n, dtype=np.float32), GRID_W)
    col = np.tile(np.arange(GRID_W, dtype=np.float32), rows_n)
    freqs = (np.float32(ROPE_BASE) ** (-np.arange(nf, dtype=np.float32) / np.float32(nf))).astype(np.float32)
    ang_r = (row[:, None] * freqs).astype(np.float64)
    ang_c = (col[:, None] * freqs).astype(np.float64)
    cos = np.concatenate([np.cos(ang_r)] * 2 + [np.cos(ang_c)] * 2, axis=1)
    sin = np.concatenate([-np.sin(ang_r), np.sin(ang_r), -np.sin(ang_c), np.sin(ang_c)], axis=1)
    reps = LANES // qk_dim
    return np.tile(cos, (1, reps)).astype(np.float32), np.tile(sin, (1, reps)).astype(np.float32)


def kernel(x, c, ctx, c_ctx, w_mod, b_mod, norm1_g, norm2_g, w_in, q_norm_g, k_norm_g, lambda_q1, lambda_k1, lambda_q2, lambda_k2, subln_g, conv_w, conv_b, gate_a_w, gate_a_b, gate_x_w, gate_x_b, lru_lambda, w_out, w_group, b_group, w_router, b_router, w1, w3, w2):
    B, S, D = x.shape
    C = ctx.shape[1]
    L = w_mod.shape[0]
    qk_dim = q_norm_g.shape[1]
    fw = lw = D // 4
    aw = D // 2
    dims = (fw, aw, lw, qk_dim)
    tm_x, tm_c = min(512, S), min(512, C)
    n_ctx, n_lat = B * C, B * S

    n_rows = -(-(B + 1) // SUBLANES) * SUBLANES
    c_all = jnp.concatenate([c, c_ctx[None, :], jnp.zeros((n_rows - B - 1, D), F32)], axis=0)
    mod = _modulation(c_all, w_mod, b_mod)

    cos_t, sin_t = (jnp.asarray(t) for t in _rope_tables(S, qk_dim))
    dummy_tab = jnp.zeros((C, LANES), F32)
    four_tabs = _fourier_tables(S, fw)
    gmat = _blockdiag(jnp.ones((256 // qk_dim, qk_dim, qk_dim), F32)).astype(BF16)

    xc = ctx
    for l in range(L):
        last = l == L - 1
        lam_init = 0.8 - 0.6 * math.exp(-0.3 * l)
        m = [mod[l, :, i * D:(i + 1) * D] for i in range(N_MOD)]
        mx = [a[:B, None, :] for a in m]
        mc = [jnp.broadcast_to(a[B][None, None, :], (B, 1, D)) for a in m]
        w_in_b = w_in[l].astype(BF16)
        w_out_b = w_out[l].astype(BF16)
        gqk = jnp.concatenate([jnp.tile(q_norm_g[l], aw // qk_dim),
                               jnp.tile(k_norm_g[l], aw // qk_dim)])[None, :]
        lam = (jnp.exp(jnp.sum(lambda_q1[l] * lambda_k1[l])) - jnp.exp(jnp.sum(lambda_q2[l] * lambda_k2[l]))
               + lam_init).astype(F32)
        s_bound = (ATT_BOUND_MARGIN * qk_dim ** 0.5 * LOG2E
                   * jnp.max(jnp.abs(q_norm_g[l])) * jnp.max(jnp.abs(k_norm_g[l]))).astype(F32)
        att_sc = jnp.stack([lam, s_bound])
        sub_g = subln_g[l][None, :]
        n1 = norm1_g[l][None, :]
        n2 = norm2_g[l][None, :]
        lru_p = {
            'conv_w': conv_w[l], 'conv_b': conv_b[l][None, :],
            'wa': [_blockdiag(gate_a_w[l, d]).astype(BF16) for d in range(2)],
            'wx': [_blockdiag(gate_x_w[l, d]).astype(BF16) for d in range(2)],
            'ba': [gate_a_b[l, d][None, :] for d in range(2)],
            'bx': [gate_x_b[l, d][None, :] for d in range(2)],
            'nsp': [(-LRU_C * jax.nn.softplus(-lru_lambda[l, d]))[None, :] for d in range(2)],
        }
        pad = ROUTE_PAD - N_GROUPS - N_EXPERTS
        w_rt = jnp.concatenate([w_group[l], w_router[l], jnp.zeros((D, pad), F32)], axis=1).astype(BF16)
        b_rt = jnp.concatenate([b_group[l], b_router[l], jnp.zeros((pad,), F32)])[None, :]
        w13 = jnp.concatenate([w1[l], w3[l]], axis=-1).astype(BF16)
        w2_b = w2[l].astype(BF16)

        ufc, qc, ktc, vc, uyc, urc = _inproj(xc, mc[0], mc[1], n1, w_in_b, gqk, gmat, dummy_tab, dummy_tab,
                                             dims=dims, use_rope=False, tm=tm_c, name="inproj_ctx")
        ufx, qx, ktx, vx, uyx, urx = _inproj(x, mx[0], mx[1], n1, w_in_b, gqk, gmat, cos_t, sin_t,
                                             dims=dims, use_rope=True, tm=tm_x, name="inproj_lat")
        grp_x = min(ATT_BLOCKS_PER_BODY, ktx.shape[1])
        att_srcs = [(ktc, vc, ktc.shape[1]), (ktx, vx, grp_x)]
        att_fn = functools.partial(_attention, att_sc, qx, att_srcs, sub_g, out_scale=1.0 - lam_init,
                                   tq=min(ATT_TQ, S))
        att_x = lax.cond(2.0 * s_bound <= ATT_BOUND_MAX_SPAN,
                         lambda: att_fn(online_max=False, name="attn_lat_bound"),
                         lambda: att_fn(online_max=True, name="attn_lat_online"))

        zeros_h = jnp.zeros((B, lw), F32)
        tc_c, tc_x = min(512, C), min(512, S)
        hc_f, hc_f_last = _lru_scan(urc, zeros_h, lru_p, 0, reverse=False, tc=tc_c)
        hx_f, _ = _lru_scan(urx, hc_f_last, lru_p, 0, reverse=False, tc=tc_x)
        rec_c, hc_b_first = _lru_scan(urc, zeros_h, lru_p, 1, reverse=True, hf=hc_f, uy=uyc, tc=tc_c)
        rec_x, _ = _lru_scan(urx, hc_b_first, lru_p, 1, reverse=True, hf=hx_f, uy=uyx, tc=tc_x)

        four_x = _fourier_long(ufx, four_tabs)

        if last:
            x1, h2, logits = _outproj(four_x, att_x, rec_x, x, mx[2], mx[3], mx[4], n2, w_out_b, w_rt, b_rt, tm=tm_x)
            ys, pos, wts = _moe([h2], logits, w13, w2_b)
            x = _combine(x1, mx[5], ys, pos, wts, row_off=0, tm=min(MOE_COPY_TOKENS, S))
        else:
            att_c = _attention(att_sc, qc, [(ktc, vc, ktc.shape[1])], sub_g, out_scale=1.0 - lam_init,
                               tq=min(256, C), online_max=True, name="attn_ctx")
            four_c = _fourier_short(ufc, four_tabs[2], four_tabs[3])
            xc1, h2c, lgc = _outproj(four_c, att_c, rec_c, xc, mc[2], mc[3], mc[4], n2, w_out_b, w_rt, b_rt, tm=tm_c)
            x1, h2x, lgx = _outproj(four_x, att_x, rec_x, x, mx[2], mx[3], mx[4], n2, w_out_b, w_rt, b_rt, tm=tm_x)
            ys, pos, wts = _moe([h2c, h2x], jnp.concatenate([lgc, lgx], axis=0), w13, w2_b)
            tmc = min(MOE_COPY_TOKENS, S, C)
            xc = _combine(xc1, mc[5], ys, pos, wts, row_off=0, tm=tmc)
            x = _combine(x1, mx[5], ys, pos, wts, row_off=n_ctx, tm=tmc)
    return x
```

```python
import functools
import math

import jax
import jax.numpy as jnp
import numpy as np
from jax import lax
from jax.experimental import pallas as pl
from jax.experimental.pallas import tpu as pltpu

F32 = jnp.float32
BF16 = jnp.bfloat16

GRID_W = 64
F_GROUPS = 4
ATT_HEADS = 4
LRU_BLOCKS = 4
LRU_C = 8.0
CONV_W = 4
CONV_LEFT = (CONV_W - 1) // 2
N_GROUPS = 4
EXPERTS_PER_GROUP = 8
N_EXPERTS = N_GROUPS * EXPERTS_PER_GROUP
TOP_K = 2
MOE_BLOCK = 512
MOE_COPY_TOKENS = 256
ROUTE_TOKENS = 512
ROUTE_OUT = 8
N_MOD = 6
EPS = 1e-6
ROPE_BASE = 10000.0
LOG2E = math.log2(math.e)
ATT_BOUND_MARGIN = 1.02
ATT_BOUND_MAX_SPAN = 100.0

LANES = 128
SUBLANES = 8
VMEM_LIMIT = 48 * 1024 * 1024
DFT_T1 = 64
DFT_K1_PER_STEP = 8
ROUTE_PAD = 128
ROW_TILE = 512
LRU_SUBCHUNK = 32
ATT_TQ = 512
ATT_BLOCKS_PER_BODY = 16


def _cparams(*sem):
    return pltpu.CompilerParams(dimension_semantics=sem, vmem_limit_bytes=VMEM_LIMIT)


def _mod_kernel(c_ref, w_ref, b_ref, o_ref):
    c = c_ref[...]
    s = c * jax.nn.sigmoid(c)
    o_ref[0] = jnp.dot(s, w_ref[0], preferred_element_type=F32,
                       precision=lax.Precision.HIGHEST) + b_ref[0]


def _modulation(c_all, w_mod, b_mod):
    L, D, n6 = w_mod.shape
    R = c_all.shape[0]
    tn = n6 // 4
    return pl.pallas_call(
        _mod_kernel,
        grid=(L, n6 // tn),
        in_specs=[pl.BlockSpec((R, D), lambda l, j: (0, 0)),
                  pl.BlockSpec((1, D, tn), lambda l, j: (l, 0, j)),
                  pl.BlockSpec((1, 1, tn), lambda l, j: (l, 0, j))],
        out_specs=pl.BlockSpec((1, R, tn), lambda l, j: (l, 0, j)),
        out_shape=jax.ShapeDtypeStruct((L, R, n6), F32),
        compiler_params=_cparams("arbitrary", "arbitrary"),
        name="modulation",
    )(c_all, w_mod, b_mod.reshape(L, 1, n6))


def _rms_mod(x, g, sc, sh):
    ms = jnp.mean(x * x, axis=-1, keepdims=True)
    return (x * lax.rsqrt(ms + EPS)) * g * (1.0 + sc) + sh


def _inproj_kernel(x_ref, sh_ref, sc_ref, g_ref, w_ref, gqk_ref, gmat_ref, cos_ref, sin_ref,
                   uf_ref, q_ref, kt_ref, v_ref, uy_ref, ur_ref, *, dims, use_rope):
    fw, aw, lw, qk_dim = dims
    q_off, k_off, v_off = fw, fw + aw, fw + 2 * aw
    y_off, r_off = v_off + aw, v_off + aw + lw
    h = _rms_mod(x_ref[0], g_ref[...], sc_ref[0], sh_ref[0])
    u = jnp.dot(h.astype(BF16), w_ref[...], preferred_element_type=F32)
    uf_ref[0] = u[:, :fw].astype(BF16)
    qk = u[:, q_off:v_off]
    sq = qk * qk
    hi = sq.astype(BF16)
    lo = (sq - hi.astype(F32)).astype(BF16)
    gm = gmat_ref[...]
    gw = gm.shape[0]
    parts = []
    for s in range(2 * aw // gw):
        sl = slice(s * gw, (s + 1) * gw)
        parts.append(jnp.dot(hi[:, sl], gm, preferred_element_type=F32)
                     + jnp.dot(lo[:, sl], gm, preferred_element_type=F32))
    msq = jnp.concatenate(parts, axis=1) * (1.0 / qk_dim)
    n = qk * lax.rsqrt(msq + EPS) * gqk_ref[...]
    if use_rope:
        reps = 2 * aw // LANES
        cos = jnp.concatenate([cos_ref[...]] * reps, axis=1)
        sin = jnp.concatenate([sin_ref[...]] * reps, axis=1)
        width = n.shape[1]
        half = qk_dim // 4
        lane = lax.broadcasted_iota(jnp.int32, n.shape, 1)
        swapped = jnp.where((lane % (2 * half)) < half,
                            pltpu.roll(n, width - half, 1), pltpu.roll(n, half, 1))
        n = n * cos + swapped * sin
    q_ref[0] = (n[:, :aw] * (qk_dim ** -0.5 * LOG2E)).astype(BF16)
    kt_ref[0, 0] = n[:, aw:].T.astype(BF16)
    v_ref[0] = u[:, v_off:y_off].astype(BF16)
    uy_ref[0] = u[:, y_off:r_off]
    ur_ref[0] = u[:, r_off:]


def _inproj(x, sh, sc, g, w_in, gqk, gmat, cos, sin, *, dims, use_rope, tm, name):
    B, T, D = x.shape
    fw, aw, lw, _ = dims
    n_in = w_in.shape[1]
    per_b = lambda b, i: (b, 0, 0)
    const = lambda b, i: (0, 0)
    tile = lambda b, i: (b, i, 0)
    return pl.pallas_call(
        functools.partial(_inproj_kernel, dims=dims, use_rope=use_rope),
        grid=(B, T // tm),
        in_specs=[pl.BlockSpec((1, tm, D), tile),
                  pl.BlockSpec((1, 1, D), per_b), pl.BlockSpec((1, 1, D), per_b),
                  pl.BlockSpec((1, D), const),
                  pl.BlockSpec((D, n_in), const),
                  pl.BlockSpec((1, 2 * aw), const),
                  pl.BlockSpec(gmat.shape, const),
                  pl.BlockSpec((tm, LANES), lambda b, i: (i, 0)),
                  pl.BlockSpec((tm, LANES), lambda b, i: (i, 0))],
        out_specs=[pl.BlockSpec((1, tm, fw), tile),
                   pl.BlockSpec((1, tm, aw), tile),
                   pl.BlockSpec((1, 1, aw, tm), lambda b, i: (b, i, 0, 0)),
                   pl.BlockSpec((1, tm, aw), tile),
                   pl.BlockSpec((1, tm, lw), tile),
                   pl.BlockSpec((1, tm, lw), tile)],
        out_shape=[jax.ShapeDtypeStruct((B, T, fw), BF16),
                   jax.ShapeDtypeStruct((B, T, aw), BF16),
                   jax.ShapeDtypeStruct((B, T // tm, aw, tm), BF16),
                   jax.ShapeDtypeStruct((B, T, aw), BF16),
                   jax.ShapeDtypeStruct((B, T, lw), F32),
                   jax.ShapeDtypeStruct((B, T, lw), F32)],
        compiler_params=_cparams("arbitrary", "arbitrary"),
        name=name,
    )(x, sh, sc, g, w_in, gqk, gmat, cos, sin)


def _attn_kernel(*refs, n_src, groups, out_scale, online_max):
    sc_ref, q_ref = refs[0], refs[1]
    kv_refs = refs[2:2 + 2 * n_src]
    g_ref, o_ref = refs[2 + 2 * n_src], refs[3 + 2 * n_src]
    q = q_ref[0].astype(F32)
    tq, w = q.shape
    lane = lax.broadcasted_iota(jnp.int32, q.shape, 1)
    qq = jnp.concatenate([jnp.where(lane < w // 2, q, 0.0),
                          jnp.where(lane >= w // 2, q, 0.0)], axis=0).astype(BF16)
    vd = kv_refs[1].shape[-1]

    if online_max:
        def step(kt, v, carry):
            m, l, acc = carry
            s = jnp.dot(qq, kt, preferred_element_type=F32)
            m_new = jnp.maximum(m, jnp.max(s, axis=-1, keepdims=True))
            alpha = jnp.exp2(m - m_new)
            p = jnp.exp2(s - m_new)
            l = alpha * l + jnp.sum(p, axis=-1, keepdims=True)
            acc = alpha * acc + jnp.dot(p.astype(BF16), v, preferred_element_type=F32)
            return m_new, l, acc

        carry = (jnp.full((2 * tq, 1), -jnp.inf, F32), jnp.zeros((2 * tq, 1), F32),
                 jnp.zeros((2 * tq, vd), F32))
    else:
        bound = sc_ref[1]

        def step(kt, v, carry):
            l_part, acc = carry
            s = jnp.dot(qq, kt, preferred_element_type=F32)
            p = jnp.exp2(s - bound)
            for c in range(s.shape[1] // LANES):
                l_part = l_part + p[:, c * LANES:(c + 1) * LANES]
            acc = acc + jnp.dot(p.astype(BF16), v, preferred_element_type=F32)
            return l_part, acc

        carry = (jnp.zeros((2 * tq, LANES), F32), jnp.zeros((2 * tq, vd), F32))

    for n in range(n_src):
        kt_ref, v_ref, grp = kv_refs[2 * n], kv_refs[2 * n + 1], groups[n]
        nblk, kb = kt_ref.shape[1], kt_ref.shape[3]
        if nblk == grp:
            for g in range(grp):
                carry = step(kt_ref[0, g], v_ref[0, g * kb:(g + 1) * kb, :], carry)
        else:
            def body(j, carry, kt_ref=kt_ref, v_ref=v_ref, grp=grp, kb=kb):
                for g in range(grp):
                    blk = j * grp + g
                    carry = step(kt_ref[0, blk], v_ref[0, pl.ds(pl.multiple_of(blk * kb, kb), kb), :], carry)
                return carry
            carry = lax.fori_loop(0, nblk // grp, body, carry)
    if online_max:
        _, l, acc = carry
    else:
        l_part, acc = carry
        l = jnp.sum(l_part, axis=-1, keepdims=True)
    o = acc / l
    d = o[:tq] - sc_ref[0] * o[tq:]
    ms = jnp.mean(d * d, axis=-1, keepdims=True)
    o_ref[0] = (d * lax.rsqrt(ms + EPS) * g_ref[...] * out_scale).astype(BF16)


def _attention(scalars, q, srcs, subln_g, *, out_scale, tq, online_max, name):
    B, S, aw = q.shape
    hd = aw // ATT_HEADS
    in_specs = [pl.BlockSpec(memory_space=pltpu.SMEM),
                pl.BlockSpec((1, tq, hd), lambda b, h, i: (b, i, h))]
    args = [scalars, q]
    for kt, v, _ in srcs:
        in_specs += [pl.BlockSpec((1, kt.shape[1], hd, kt.shape[3]), lambda b, h, i: (b, 0, h, 0)),
                     pl.BlockSpec((1, v.shape[1], hd), lambda b, h, i: (b, 0, h))]
        args += [kt, v]
    in_specs.append(pl.BlockSpec((1, hd), lambda b, h, i: (0, 0)))
    args.append(subln_g)
    return pl.pallas_call(
        functools.partial(_attn_kernel, n_src=len(srcs), groups=tuple(g for _, _, g in srcs),
                          out_scale=out_scale, online_max=online_max),
        grid=(B, ATT_HEADS, S // tq),
        in_specs=in_specs,
        out_specs=pl.BlockSpec((1, tq, hd), lambda b, h, i: (b, i, h)),
        out_shape=jax.ShapeDtypeStruct((B, S, aw), BF16),
        compiler_params=_cparams("arbitrary", "arbitrary", "arbitrary"),
        name=name,
    )(*args)


def _dft1_kernel(m_ref, z_ref, y_ref):
    y_ref[0] = jnp.dot(m_ref[...], z_ref[0], preferred_element_type=F32).astype(BF16)


def _dft2_kernel(y_ref, tab_ref, cc_ref, sc_ref, o_ref, *, scale):
    w = cc_ref.shape[0]
    for i in range(tab_ref.shape[0]):
        y = jnp.concatenate([y_ref[0, 0, i], y_ref[0, 1, i]], axis=0)
        zr = jnp.dot(tab_ref[i, 0], y, preferred_element_type=F32)
        zi = jnp.dot(tab_ref[i, 1], y, preferred_element_type=F32)
        o = (jnp.dot(zr.astype(BF16), cc_ref[...], preferred_element_type=F32)
             + jnp.dot(zi.astype(BF16), sc_ref[...], preferred_element_type=F32))
        o_ref[0, :, i * w:(i + 1) * w] = (o * scale).astype(BF16)


@functools.lru_cache(maxsize=None)
def _fourier_tables(T, fw):
    t1n, t2n = DFT_T1, T // DFT_T1
    gd = fw // F_GROUPS
    two_pi = 2.0 * np.pi
    k1 = np.arange(t1n, dtype=np.int64)
    a1 = two_pi * ((k1[:, None] * k1[None, :]) % t1n) / t1n
    m1 = np.concatenate([np.cos(a1), -np.sin(a1)], axis=0)
    k2 = np.arange(t2n, dtype=np.int64)
    kk = k1[:, None, None] + t1n * k2[None, :, None]
    ph = two_pi * ((kk * k2[None, None, :]) % T) / T
    cp, sp = np.cos(ph), np.sin(ph)
    tab = np.stack([np.concatenate([cp, sp], axis=-1),
                    np.concatenate([-sp, cp], axis=-1)], axis=1)
    c = np.arange(gd, dtype=np.int64)
    ac = two_pi * ((c[:, None] * c[None, :]) % gd) / gd
    eye = np.eye(F_GROUPS)
    cc, sc = np.kron(eye, np.cos(ac)), np.kron(eye, np.sin(ac))
    return tuple(np.asarray(t, np.float32) for t in (m1, tab, cc, sc))


def _fourier_long(uf, tables):
    B, T, W = uf.shape
    m1, tab, cc, sc = (jnp.asarray(t, BF16) for t in tables)
    t1n, t2n = DFT_T1, T // DFT_T1
    ncol = t2n * W
    tn = min(ncol, 4096)
    y = pl.pallas_call(
        _dft1_kernel,
        grid=(B, ncol // tn),
        in_specs=[pl.BlockSpec((2 * t1n, t1n), lambda b, j: (0, 0)),
                  pl.BlockSpec((1, t1n, tn), lambda b, j: (b, 0, j))],
        out_specs=pl.BlockSpec((1, 2 * t1n, tn), lambda b, j: (b, 0, j)),
        out_shape=jax.ShapeDtypeStruct((B, 2 * t1n, ncol), BF16),
        compiler_params=_cparams("arbitrary", "arbitrary"),
        name="dft_stage1",
    )(m1, uf.reshape(B, t1n, ncol))
    y5 = y.reshape(B, 2, t1n, t2n, W)
    scale = 1.0 / math.sqrt(T * (W // F_GROUPS))
    out = pl.pallas_call(
        functools.partial(_dft2_kernel, scale=scale),
        grid=(t1n // DFT_K1_PER_STEP, B),
        in_specs=[pl.BlockSpec((1, 2, DFT_K1_PER_STEP, t2n, W), lambda k, b: (b, 0, k, 0, 0)),
                  pl.BlockSpec((DFT_K1_PER_STEP, 2, t2n, 2 * t2n), lambda k, b: (k, 0, 0, 0)),
                  pl.BlockSpec((W, W), lambda k, b: (0, 0)),
                  pl.BlockSpec((W, W), lambda k, b: (0, 0))],
        out_specs=pl.BlockSpec((1, t2n, DFT_K1_PER_STEP * W), lambda k, b: (b, 0, k)),
        out_shape=jax.ShapeDtypeStruct((B, t2n, t1n * W), BF16),
        compiler_params=_cparams("arbitrary", "arbitrary"),
        name="dft_stage2",
    )(y5, tab, cc, sc)
    return out.reshape(B, T, W)


def _dft_short_kernel(z_ref, ct_ref, st_ref, cc_ref, sc_ref, o_ref, *, scale):
    z = z_ref[0]
    zc = jnp.dot(z, cc_ref[...], preferred_element_type=F32).astype(BF16)
    zs = jnp.dot(z, sc_ref[...], preferred_element_type=F32).astype(BF16)
    o = (jnp.dot(ct_ref[...], zc, preferred_element_type=F32)
         - jnp.dot(st_ref[...], zs, preferred_element_type=F32))
    o_ref[0] = (o * scale).astype(BF16)


def _fourier_short(uf, cc, sc):
    B, T, W = uf.shape
    t = np.arange(T, dtype=np.int64)
    ang = 2.0 * np.pi * ((t[:, None] * t[None, :]) % T) / T
    ct, st = jnp.asarray(np.cos(ang), BF16), jnp.asarray(np.sin(ang), BF16)
    cc, sc = jnp.asarray(cc, BF16), jnp.asarray(sc, BF16)
    scale = 1.0 / math.sqrt(T * (W // F_GROUPS))
    return pl.pallas_call(
        functools.partial(_dft_short_kernel, scale=scale),
        grid=(B,),
        in_specs=[pl.BlockSpec((1, T, W), lambda b: (b, 0, 0)),
                  pl.BlockSpec((T, T), lambda b: (0, 0)), pl.BlockSpec((T, T), lambda b: (0, 0)),
                  pl.BlockSpec((W, W), lambda b: (0, 0)), pl.BlockSpec((W, W), lambda b: (0, 0))],
        out_specs=pl.BlockSpec((1, T, W), lambda b: (b, 0, 0)),
        out_shape=jax.ShapeDtypeStruct((B, T, W), BF16),
        compiler_params=_cparams("arbitrary"),
        name="dft_short",
    )(uf, ct, st, cc, sc)


def _affine_scan(a, b, reverse):
    T = a.shape[0]
    row = lax.broadcasted_iota(jnp.int32, a.shape, 0)
    k = 1
    while k < T:
        if k % SUBLANES == 0:
            one, zero = jnp.ones((k,) + a.shape[1:], a.dtype), jnp.zeros((k,) + a.shape[1:], a.dtype)
            if reverse:
                a_s, b_s = jnp.concatenate([a[k:], one], axis=0), jnp.concatenate([b[k:], zero], axis=0)
            else:
                a_s, b_s = jnp.concatenate([one, a[:T - k]], axis=0), jnp.concatenate([zero, b[:T - k]], axis=0)
        else:
            if reverse:
                a_s, b_s, valid = pltpu.roll(a, T - k, 0), pltpu.roll(b, T - k, 0), row < T - k
            else:
                a_s, b_s, valid = pltpu.roll(a, k, 0), pltpu.roll(b, k, 0), row >= k
            a_s, b_s = jnp.where(valid, a_s, 1.0), jnp.where(valid, b_s, 0.0)
        b = a * b_s + b
        a = a * a_s
        k *= 2
    return a, b


def _lru_kernel(*refs, reverse, combine, nt):
    if combine:
        (ur_ref, prev_ref, next_ref, h0_ref, cw_ref, cb_ref, wa_ref, ba_ref, wx_ref, bx_ref,
         nsp_ref, hf_ref, uy_ref, out_ref, hlast_ref, carry_ref) = refs
    else:
        (ur_ref, prev_ref, next_ref, h0_ref, cw_ref, cb_ref, wa_ref, ba_ref, wx_ref, bx_ref,
         nsp_ref, out_ref, hlast_ref, carry_ref) = refs
    i = pl.program_id(1)
    ci = nt - 1 - i if reverse else i

    @pl.when(i == 0)
    def _():
        carry_ref[...] = h0_ref[0]

    u = ur_ref[0]
    tc = u.shape[0]
    prev = jnp.where(ci == 0, 0.0, prev_ref[0])
    nxt = jnp.where(ci == nt - 1, 0.0, next_ref[0])
    ext = jnp.concatenate([prev, u, nxt], axis=0)
    n_ext = ext.shape[0]
    xr = cb_ref[...] + u * cw_ref[CONV_LEFT:CONV_LEFT + 1, :]
    for k in range(CONV_W):
        d = k - CONV_LEFT
        if d != 0:
            shifted = pltpu.roll(ext, (-d) % n_ext, 0)[SUBLANES:SUBLANES + tc]
            xr = xr + shifted * cw_ref[k:k + 1, :]
    xb = xr.astype(BF16)
    r = jax.nn.sigmoid(jnp.dot(xb, wa_ref[...], preferred_element_type=F32) + ba_ref[...])
    g = jax.nn.sigmoid(jnp.dot(xb, wx_ref[...], preferred_element_type=F32) + bx_ref[...])
    log_a = r * nsp_ref[...]
    a = jnp.exp(log_a)
    bt = jnp.sqrt(-jnp.tanh(log_a) * (a * a + 1.0)) * (g * xr)
    sub = math.gcd(LRU_SUBCHUNK, tc)
    n_sub = tc // sub
    parts = [None] * n_sub
    last = carry_ref[...]
    for j in (reversed(range(n_sub)) if reverse else range(n_sub)):
        a_cum, b_cum = _affine_scan(a[j * sub:(j + 1) * sub], bt[j * sub:(j + 1) * sub], reverse)
        hj = a_cum * last + b_cum
        last = hj[0:1] if reverse else hj[sub - 1:sub]
        parts[j] = hj
    h = jnp.concatenate(parts, axis=0) if n_sub > 1 else parts[0]
    carry_ref[...] = last
    hlast_ref[0] = last
    if combine:
        out_ref[0] = (jax.nn.gelu(uy_ref[0]) * (hf_ref[0] + h)).astype(BF16)
    else:
        out_ref[0] = h


def _lru_scan(ur, h0, p, d, *, reverse, hf=None, uy=None, tc):
    B, T, W = ur.shape
    nt = T // tc
    hb = tc // SUBLANES
    nh = T // SUBLANES
    combine = hf is not None
    cidx = (lambda i: nt - 1 - i) if reverse else (lambda i: i)
    tile = lambda b, i: (b, cidx(i), 0)
    const = lambda b, i: (0, 0)
    in_specs = [pl.BlockSpec((1, tc, W), tile),
                pl.BlockSpec((1, SUBLANES, W), lambda b, i: (b, jnp.maximum(cidx(i) * hb - 1, 0), 0)),
                pl.BlockSpec((1, SUBLANES, W), lambda b, i: (b, jnp.minimum((cidx(i) + 1) * hb, nh - 1), 0)),
                pl.BlockSpec((1, 1, W), lambda b, i: (b, 0, 0)),
                pl.BlockSpec((CONV_W, W), const), pl.BlockSpec((1, W), const),
                pl.BlockSpec((W, W), const), pl.BlockSpec((1, W), const),
                pl.BlockSpec((W, W), const), pl.BlockSpec((1, W), const),
                pl.BlockSpec((1, W), const)]
    args = [ur, ur, ur, h0.reshape(B, 1, W), p['conv_w'], p['conv_b'],
            p['wa'][d], p['ba'][d], p['wx'][d], p['bx'][d], p['nsp'][d]]
    if combine:
        in_specs += [pl.BlockSpec((1, tc, W), tile), pl.BlockSpec((1, tc, W), tile)]
        args += [hf, uy]
    out, hlast = pl.pallas_call(
        functools.partial(_lru_kernel, reverse=reverse, combine=combine, nt=nt),
        grid=(B, nt),
        in_specs=in_specs,
        out_specs=[pl.BlockSpec((1, tc, W), tile), pl.BlockSpec((1, 1, W), lambda b, i: (b, 0, 0))],
        out_shape=[jax.ShapeDtypeStruct((B, T, W), BF16 if combine else F32),
                   jax.ShapeDtypeStruct((B, 1, W), F32)],
        scratch_shapes=[pltpu.VMEM((1, W), F32)],
        compiler_params=_cparams("arbitrary", "arbitrary"),
        name=("lru_bwd" if reverse else "lru_fwd"),
    )(*args)
    return out, hlast.reshape(B, W)


def _store_token_tiles(ref, val, lead=()):
    rows, width = val.shape
    nsub = width // LANES
    for j in range(nsub):
        ref[lead + (pl.ds(j, rows, stride=nsub), slice(None))] = val[:, j * LANES:(j + 1) * LANES]


def _load_token_tiles(ref, rows, nsub, lead=()):
    return jnp.concatenate([ref[lead + (pl.ds(j, rows, stride=nsub), slice(None))] for j in range(nsub)], axis=1)


def _outproj_kernel(f_ref, a_ref, r_ref, x_ref, g1_ref, sh_ref, sc_ref, ng_ref, w_ref, wr_ref, br_ref,
                    x1_ref, h2_ref, lg_ref, *, fw, aw):
    y = (jnp.dot(f_ref[0], w_ref[0:fw, :], preferred_element_type=F32)
         + jnp.dot(a_ref[0], w_ref[fw:fw + aw, :], preferred_element_type=F32)
         + jnp.dot(r_ref[0], w_ref[fw + aw:, :], preferred_element_type=F32))
    x1 = x_ref[0] + g1_ref[0] * y
    x1_ref[0] = x1
    h2 = _rms_mod(x1, ng_ref[...], sc_ref[0], sh_ref[0])
    _store_token_tiles(h2_ref, h2, lead=(0,))
    lg_ref[0] = jnp.dot(h2.astype(BF16), wr_ref[...], preferred_element_type=F32) + br_ref[...]


def _outproj(four, att, rec, x, g1, sh2, sc2, ng, w_out, w_rt, b_rt, *, tm):
    B, T, D = x.shape
    fw, aw = four.shape[2], att.shape[2]
    nsub = D // LANES
    tile = lambda b, i: (b, i, 0)
    per_b = lambda b, i: (b, 0, 0)
    const = lambda b, i: (0, 0)
    x1, h2, logits = pl.pallas_call(
        functools.partial(_outproj_kernel, fw=fw, aw=aw),
        grid=(B, T // tm),
        in_specs=[pl.BlockSpec((1, tm, fw), tile), pl.BlockSpec((1, tm, aw), tile),
                  pl.BlockSpec((1, tm, rec.shape[2]), tile), pl.BlockSpec((1, tm, D), tile),
                  pl.BlockSpec((1, 1, D), per_b), pl.BlockSpec((1, 1, D), per_b),
                  pl.BlockSpec((1, 1, D), per_b), pl.BlockSpec((1, D), const),
                  pl.BlockSpec((D, D), const), pl.BlockSpec((D, ROUTE_PAD), const),
                  pl.BlockSpec((1, ROUTE_PAD), const)],
        out_specs=[pl.BlockSpec((1, tm, D), tile), pl.BlockSpec((1, tm * nsub, LANES), tile),
                   pl.BlockSpec((1, tm, ROUTE_PAD), tile)],
        out_shape=[jax.ShapeDtypeStruct((B, T, D), F32),
                   jax.ShapeDtypeStruct((B, T * nsub, LANES), F32),
                   jax.ShapeDtypeStruct((B, T, ROUTE_PAD), F32)],
        compiler_params=_cparams("arbitrary", "arbitrary"),
        name="outproj",
    )(four, att, rec, x, g1, sh2, sc2, ng, w_out, w_rt, b_rt)
    return x1, h2.reshape(B * T * nsub, LANES), logits.reshape(B * T, ROUTE_PAD)


def _dispatch_kernel(pe_ref, nu_ref, pos_ref, *refs, part_tiles, nsub, n_blocks):
    n_parts = len(part_tiles)
    h_refs = refs[:n_parts]
    xs_ref, zbuf, sem, zsem = refs[n_parts:]
    i = pl.program_id(0)
    blk = MOE_BLOCK * nsub

    def zero_copy(row0):
        return pltpu.make_async_copy(zbuf, xs_ref.at[pl.ds(pl.multiple_of(row0, blk), blk)], zsem)

    @pl.when(i == 0)
    def _():
        zbuf[...] = jnp.zeros_like(zbuf)

        def expert_block(e, start, wait):
            pend = pe_ref[e]

            @pl.when(pend > start)
            def _():
                if wait:
                    zero_copy(0).wait()
                else:
                    zero_copy((pend - MOE_BLOCK) * nsub).start()
            return pend

        def tail_block(b, c, wait):
            if wait:
                zero_copy(0).wait()
            else:
                zero_copy(b * blk).start()
            return c

        for wait in (False, True):
            lax.fori_loop(0, N_EXPERTS, functools.partial(expert_block, wait=wait), 0)
            lax.fori_loop(nu_ref[0], n_blocks, functools.partial(tail_block, wait=wait), 0)

    first = 0
    for h_ref, nt in zip(h_refs, part_tiles):
        tm = h_ref.shape[0] // nsub

        @pl.when((i >= first) & (i < first + nt))
        def _(h_ref=h_ref, tm=tm):
            def issue(t, c):
                src = h_ref.at[pl.ds(pl.multiple_of(t * nsub, nsub), nsub)]
                for k in range(TOP_K):
                    dst = pl.multiple_of(pos_ref[0, 0, TOP_K * t + k] * nsub, nsub)
                    pltpu.make_async_copy(src, xs_ref.at[pl.ds(dst, nsub)], sem).start(priority=k % 2)
                return c

            lax.fori_loop(0, tm, issue, 0, unroll=4)
            for k in range(TOP_K):
                pltpu.make_async_copy(h_ref, xs_ref.at[pl.ds(0, tm * nsub)], sem).wait()
        first += nt


def _dispatch(h2_parts, pos, pends, n_used, *, n_slots, tm, nsub):
    part_tiles = tuple(h.shape[0] // (nsub * tm) for h in h2_parts)
    firsts = [sum(part_tiles[:p]) for p in range(len(part_tiles))]
    in_specs = [pl.BlockSpec((1, 1, TOP_K * tm), lambda i, pe, nu: (i, 0, 0), memory_space=pltpu.SMEM)]
    for first, nt in zip(firsts, part_tiles):
        in_specs.append(pl.BlockSpec(
            (tm * nsub, LANES), lambda i, pe, nu, first=first, nt=nt: (jnp.clip(i - first, 0, nt - 1), 0)))
    return pl.pallas_call(
        functools.partial(_dispatch_kernel, part_tiles=part_tiles, nsub=nsub, n_blocks=n_slots // MOE_BLOCK),
        grid_spec=pltpu.PrefetchScalarGridSpec(
            num_scalar_prefetch=2,
            grid=(sum(part_tiles),),
            in_specs=in_specs,
            out_specs=pl.BlockSpec(memory_space=pl.ANY),
            scratch_shapes=[pltpu.VMEM((MOE_BLOCK * nsub, LANES), F32),
                            pltpu.SemaphoreType.DMA(()), pltpu.SemaphoreType.DMA(())]),
        out_shape=jax.ShapeDtypeStruct((n_slots * nsub, LANES), F32),
        compiler_params=_cparams("arbitrary"),
        name="moe_dispatch",
    )(pends, n_used, pos.reshape(-1, 1, TOP_K * tm), *h2_parts)


def _expert_kernel(be_ref, nu_ref, x_ref, w1_ref, w3_ref, w2_ref, o_ref):
    i = pl.program_id(0)
    nsub = w2_ref.shape[2] // LANES

    @pl.when(i < nu_ref[0])
    def _():
        x = _load_token_tiles(x_ref, MOE_BLOCK, nsub)
        xb = x.astype(BF16)
        h1 = jnp.dot(xb, w1_ref[0].astype(BF16), preferred_element_type=F32)
        h3 = jnp.dot(xb, w3_ref[0].astype(BF16), preferred_element_type=F32)
        hb = (jax.nn.silu(h1) * h3).astype(BF16)
        _store_token_tiles(o_ref, jnp.dot(hb, w2_ref[0].astype(BF16), preferred_element_type=F32))

    @pl.when(i >= nu_ref[0])
    def _():
        o_ref[...] = jnp.zeros_like(o_ref)


def _experts(xs, block_exp, n_used, w1, w3, w2):
    de, D = w2.shape[1], w2.shape[2]
    nsub = D // LANES
    R = MOE_BLOCK
    P = xs.shape[0] // nsub
    clamp = lambda i, be, nu: (jnp.minimum(i, nu[0] - 1), 0)
    return pl.pallas_call(
        _expert_kernel,
        grid_spec=pltpu.PrefetchScalarGridSpec(
            num_scalar_prefetch=2,
            grid=(P // R,),
            in_specs=[pl.BlockSpec((R * nsub, LANES), clamp),
                      pl.BlockSpec((1, D, de), lambda i, be, nu: (be[i], 0, 0)),
                      pl.BlockSpec((1, D, de), lambda i, be, nu: (be[i], 0, 0)),
                      pl.BlockSpec((1, de, D), lambda i, be, nu: (be[i], 0, 0))],
            out_specs=pl.BlockSpec((R * nsub, LANES), lambda i, be, nu: (i, 0))),
        out_shape=jax.ShapeDtypeStruct((P * nsub, LANES), F32),
        compiler_params=_cparams("arbitrary"),
        name="moe_experts",
    )(block_exp, n_used, xs, w1, w3, w2)


def _combine_kernel(pos_ref, nxt_ref, x_ref, g_ref, w_ref, ys_ref, o_ref, buf, sems):
    tm = x_ref.shape[1]
    nsub = x_ref.shape[2] // LANES
    step = pl.program_id(0) * pl.num_programs(1) + pl.program_id(1)
    n_steps = pl.num_programs(0) * pl.num_programs(1)
    slot = step % 2

    def issue(idx_ref, into):
        def body(t, c):
            for k in range(TOP_K):
                src = pl.multiple_of(idx_ref[0, 0, TOP_K * t + k] * nsub, nsub)
                pltpu.make_async_copy(ys_ref.at[pl.ds(src, nsub)],
                                      buf.at[into, k, pl.ds(pl.multiple_of(t * nsub, nsub), nsub)],
                                      sems.at[into]).start(priority=k % 2)
            return c
        lax.fori_loop(0, tm, body, 0, unroll=4)

    @pl.when(step == 0)
    def _():
        issue(pos_ref, 0)

    @pl.when(step + 1 < n_steps)
    def _():
        issue(nxt_ref, 1 - slot)

    for k in range(TOP_K):
        pltpu.make_async_copy(ys_ref.at[pl.ds(0, tm * nsub)], buf.at[slot, k], sems.at[slot]).wait()
    w = w_ref[0]
    y = w[:, 0:1] * _load_token_tiles(buf.at[slot, 0], tm, nsub)
    for k in range(1, TOP_K):
        y = y + w[:, k:k + 1] * _load_token_tiles(buf.at[slot, k], tm, nsub)
    o_ref[0] = x_ref[0] + g_ref[0] * y


def _combine(x1, g2, ys, pos, wts, *, row_off, tm):
    B, T, D = x1.shape
    nt = T // tm
    off = row_off // tm
    last = off + B * nt - 1
    pos3 = pos.reshape(-1, 1, TOP_K * tm)
    wts3 = wts.reshape(-1, tm, TOP_K)
    return pl.pallas_call(
        _combine_kernel,
        grid=(B, nt),
        in_specs=[pl.BlockSpec((1, 1, TOP_K * tm), lambda b, i: (b * nt + i + off, 0, 0),
                               memory_space=pltpu.SMEM),
                  pl.BlockSpec((1, 1, TOP_K * tm), lambda b, i: (jnp.minimum(b * nt + i + off + 1, last), 0, 0),
                               memory_space=pltpu.SMEM),
                  pl.BlockSpec((1, tm, D), lambda b, i: (b, i, 0)),
                  pl.BlockSpec((1, 1, D), lambda b, i: (b, 0, 0)),
                  pl.BlockSpec((1, tm, TOP_K), lambda b, i: (b * nt + i + off, 0, 0)),
                  pl.BlockSpec(memory_space=pl.ANY)],
        out_specs=pl.BlockSpec((1, tm, D), lambda b, i: (b, i, 0)),
        out_shape=jax.ShapeDtypeStruct((B, T, D), F32),
        scratch_shapes=[pltpu.VMEM((2, TOP_K, tm * (D // LANES), LANES), F32), pltpu.SemaphoreType.DMA((2,))],
        compiler_params=_cparams("arbitrary", "arbitrary"),
        name="moe_combine",
    )(pos3, pos3, x1, g2, wts3, ys)


def _route_kernel(lg_ref, tri_ref, o_ref, cnt_ref, base_ref):
    @pl.when(pl.program_id(0) == 0)
    def _():
        base_ref[...] = jnp.zeros_like(base_ref)

    lg = lg_ref[...]
    lane = lax.broadcasted_iota(jnp.int32, lg.shape, 1)
    lane_f = lane.astype(F32)
    ninf = jnp.float32(-jnp.inf)
    far = jnp.float32(ROUTE_PAD)
    is_g = lane < N_GROUPS
    gl = jnp.where(is_g, lg, ninf)
    gmax = jnp.max(gl, axis=-1, keepdims=True)
    gsum = jnp.sum(jnp.where(is_g, jnp.exp(gl - gmax), 0.0), axis=-1, keepdims=True)
    p_g = 1.0 / gsum
    g_idx = jnp.min(jnp.where(gl == gmax, lane_f, far), axis=-1, keepdims=True)
    e_lane = lane - N_GROUPS
    e_lane_f = e_lane.astype(F32)
    shift = EXPERTS_PER_GROUP.bit_length() - 1
    in_grp = ((e_lane >= 0) & (e_lane < N_EXPERTS)
              & (jnp.right_shift(e_lane, shift).astype(F32) == g_idx))
    el = jnp.where(in_grp, lg, ninf)
    v1 = jnp.max(el, axis=-1, keepdims=True)
    i1 = jnp.min(jnp.where(el == v1, e_lane_f, far), axis=-1, keepdims=True)
    el2 = jnp.where(e_lane_f == i1, ninf, el)
    v2 = jnp.max(el2, axis=-1, keepdims=True)
    i2 = jnp.min(jnp.where(el2 == v2, e_lane_f, far), axis=-1, keepdims=True)
    w1 = p_g / (1.0 + jnp.exp(v2 - v1))
    w2 = p_g - w1
    hit1, hit2 = e_lane_f == i1, e_lane_f == i2
    oh1, oh2 = hit1.astype(F32), hit2.astype(F32)
    tri = tri_ref[...]
    pre1 = jnp.dot(tri, oh1.astype(BF16), preferred_element_type=F32)
    pre2 = jnp.dot(tri, oh2.astype(BF16), preferred_element_type=F32)
    tot1 = jnp.sum(oh1, axis=0, keepdims=True)
    tot2 = jnp.sum(oh2, axis=0, keepdims=True)
    base = base_ref[...]
    r1 = jnp.sum(jnp.where(hit1, pre1 + base, 0.0), axis=-1, keepdims=True)
    r2 = jnp.sum(jnp.where(hit2, pre2 + (base + tot1), 0.0), axis=-1, keepdims=True)
    base = base + tot1 + tot2
    base_ref[...] = base
    cnt_ref[...] = base
    cols = (i1, i2, r1, r2, w1, w2)
    out = jnp.zeros(lg.shape, F32)
    for j, col in enumerate(cols):
        out = jnp.where(lane == j, col, out)
    o_ref[...] = out[:, :o_ref.shape[1]]


def _route(logits):
    N = logits.shape[0]
    tm = math.gcd(ROUTE_TOKENS, N)
    tri = jnp.asarray(np.tril(np.ones((tm, tm), np.float32), -1), BF16)
    out, cnt = pl.pallas_call(
        _route_kernel,
        grid=(N // tm,),
        in_specs=[pl.BlockSpec((tm, ROUTE_PAD), lambda i: (i, 0)),
                  pl.BlockSpec((tm, tm), lambda i: (0, 0))],
        out_specs=[pl.BlockSpec((tm, ROUTE_OUT), lambda i: (i, 0)),
                   pl.BlockSpec((1, ROUTE_PAD), lambda i: (0, 0))],
        out_shape=[jax.ShapeDtypeStruct((N, ROUTE_OUT), F32),
                   jax.ShapeDtypeStruct((1, ROUTE_PAD), F32)],
        scratch_shapes=[pltpu.VMEM((1, ROUTE_PAD), F32)],
        compiler_params=_cparams("arbitrary"),
        name="moe_route",
    )(logits, tri)
    eid = out[:, 0:TOP_K].astype(jnp.int32)
    rank = out[:, TOP_K:2 * TOP_K].astype(jnp.int32)
    wts = out[:, 2 * TOP_K:3 * TOP_K]
    counts = cnt[0, N_GROUPS:N_GROUPS + N_EXPERTS].astype(jnp.int32)
    return eid, rank, wts, counts


def _dispatch_plan(eid, rank, counts):
    N = eid.shape[0]
    A = N * TOP_K
    padded = (counts + MOE_BLOCK - 1) // MOE_BLOCK * MOE_BLOCK
    pends = jnp.cumsum(padded)
    pstarts = pends - padded
    hit = eid[..., None] == jnp.arange(N_EXPERTS, dtype=jnp.int32)
    dest = jnp.sum(jnp.where(hit, pstarts, 0), axis=-1) + rank
    n_blocks = -(-A // MOE_BLOCK) + N_EXPERTS
    starts = jnp.arange(n_blocks, dtype=jnp.int32) * MOE_BLOCK
    block_exp = jnp.minimum(jnp.sum((pends[None, :] <= starts[:, None]).astype(jnp.int32), axis=1),
                            N_EXPERTS - 1).astype(jnp.int32)
    n_used = (pends[-1] // MOE_BLOCK).astype(jnp.int32).reshape(1)
    return block_exp, n_used, dest, pends.astype(jnp.int32), n_blocks * MOE_BLOCK


def _moe(h2_parts, logits, w1, w3, w2):
    eid, rank, wts, counts = _route(logits)
    block_exp, n_used, pos, pends, n_slots = _dispatch_plan(eid, rank, counts)
    nsub = w2.shape[2] // LANES
    tm = functools.reduce(math.gcd, [h.shape[0] // nsub for h in h2_parts], MOE_COPY_TOKENS)
    xs = _dispatch(h2_parts, pos, pends, n_used, n_slots=n_slots, tm=tm, nsub=nsub)
    ys = _experts(xs, block_exp, n_used, w1, w3, w2)
    return ys, pos, wts


def _blockdiag(w):
    G, n, _ = w.shape
    eye = jnp.eye(G, dtype=w.dtype)
    return (eye[:, None, :, None] * w[:, :, None, :]).reshape(G * n, G * n)


@functools.lru_cache(maxsize=None)
def _rope_tables(S, qk_dim):
    half = qk_dim // 2
    nf = half // 2
    rows_n = S // GRID_W
    row = np.repeat(np.arange(rows_n, dtype=np.float32), GRID_W)
    col = np.tile(np.arange(GRID_W, dtype=np.float32), rows_n)
    freqs = (np.float32(ROPE_BASE) ** (-np.arange(nf, dtype=np.float32) / np.float32(nf))).astype(np.float32)
    ang_r = (row[:, None] * freqs).astype(np.float64)
    ang_c = (col[:, None] * freqs).astype(np.float64)
    cos = np.concatenate([np.cos(ang_r)] * 2 + [np.cos(ang_c)] * 2, axis=1)
    sin = np.concatenate([-np.sin(ang_r), np.sin(ang_r), -np.sin(ang_c), np.sin(ang_c)], axis=1)
    reps = LANES // qk_dim
    return np.tile(cos, (1, reps)).astype(np.float32), np.tile(sin, (1, reps)).astype(np.float32)


def kernel(x, c, ctx, c_ctx, w_mod, b_mod, norm1_g, norm2_g, w_in, q_norm_g, k_norm_g, lambda_q1, lambda_k1, lambda_q2, lambda_k2, subln_g, conv_w, conv_b, gate_a_w, gate_a_b, gate_x_w, gate_x_b, lru_lambda, w_out, w_group, b_group, w_router, b_router, w1, w3, w2):
    B, S, D = x.shape
    C = ctx.shape[1]
    L = w_mod.shape[0]
    qk_dim = q_norm_g.shape[1]
    fw = lw = D // 4
    aw = D // 2
    dims = (fw, aw, lw, qk_dim)
    tm_x, tm_c = min(ROW_TILE, S), min(ROW_TILE, C)
    n_ctx, n_lat = B * C, B * S

    n_rows = -(-(B + 1) // SUBLANES) * SUBLANES
    c_all = jnp.concatenate([c, c_ctx[None, :], jnp.zeros((n_rows - B - 1, D), F32)], axis=0)
    mod = _modulation(c_all, w_mod, b_mod)

    cos_t, sin_t = (jnp.asarray(t) for t in _rope_tables(S, qk_dim))
    dummy_tab = jnp.zeros((C, LANES), F32)
    four_tabs = _fourier_tables(S, fw)
    gmat = _blockdiag(jnp.ones((256 // qk_dim, qk_dim, qk_dim), F32)).astype(BF16)

    xc = ctx
    for l in range(L):
        last = l == L - 1
        lam_init = 0.8 - 0.6 * math.exp(-0.3 * l)
        m = [mod[l, :, i * D:(i + 1) * D] for i in range(N_MOD)]
        mx = [a[:B, None, :] for a in m]
        mc = [jnp.broadcast_to(a[B][None, None, :], (B, 1, D)) for a in m]
        w_in_b = w_in[l].astype(BF16)
        w_out_b = w_out[l].astype(BF16)
        gqk = jnp.concatenate([jnp.tile(q_norm_g[l], aw // qk_dim),
                               jnp.tile(k_norm_g[l], aw // qk_dim)])[None, :]
        lam = (jnp.exp(jnp.sum(lambda_q1[l] * lambda_k1[l])) - jnp.exp(jnp.sum(lambda_q2[l] * lambda_k2[l]))
               + lam_init).astype(F32)
        s_bound = (ATT_BOUND_MARGIN * qk_dim ** 0.5 * LOG2E
                   * jnp.max(jnp.abs(q_norm_g[l])) * jnp.max(jnp.abs(k_norm_g[l]))).astype(F32)
        att_sc = jnp.stack([lam, s_bound])
        sub_g = subln_g[l][None, :]
        n1 = norm1_g[l][None, :]
        n2 = norm2_g[l][None, :]
        lru_p = {
            'conv_w': conv_w[l], 'conv_b': conv_b[l][None, :],
            'wa': [_blockdiag(gate_a_w[l, d]).astype(BF16) for d in range(2)],
            'wx': [_blockdiag(gate_x_w[l, d]).astype(BF16) for d in range(2)],
            'ba': [gate_a_b[l, d][None, :] for d in range(2)],
            'bx': [gate_x_b[l, d][None, :] for d in range(2)],
            'nsp': [(-LRU_C * jax.nn.softplus(-lru_lambda[l, d]))[None, :] for d in range(2)],
        }
        pad = ROUTE_PAD - N_GROUPS - N_EXPERTS
        w_rt = jnp.concatenate([w_group[l], w_router[l], jnp.zeros((D, pad), F32)], axis=1).astype(BF16)
        b_rt = jnp.concatenate([b_group[l], b_router[l], jnp.zeros((pad,), F32)])[None, :]

        ufc, qc, ktc, vc, uyc, urc = _inproj(xc, mc[0], mc[1], n1, w_in_b, gqk, gmat, dummy_tab, dummy_tab,
                                             dims=dims, use_rope=False, tm=tm_c, name="inproj_ctx")
        ufx, qx, ktx, vx, uyx, urx = _inproj(x, mx[0], mx[1], n1, w_in_b, gqk, gmat, cos_t, sin_t,
                                             dims=dims, use_rope=True, tm=tm_x, name="inproj_lat")
        grp_x = min(ATT_BLOCKS_PER_BODY, ktx.shape[1])
        att_srcs = [(ktc, vc, ktc.shape[1]), (ktx, vx, grp_x)]
        att_fn = functools.partial(_attention, att_sc, qx, att_srcs, sub_g, out_scale=1.0 - lam_init,
                                   tq=min(ATT_TQ, S))
        att_x = lax.cond(2.0 * s_bound <= ATT_BOUND_MAX_SPAN,
                         lambda: att_fn(online_max=False, name="attn_lat_bound"),
                         lambda: att_fn(online_max=True, name="attn_lat_online"))

        zeros_h = jnp.zeros((B, lw), F32)
        tc_c, tc_x = min(ROW_TILE, C), min(ROW_TILE, S)
        hc_f, hc_f_last = _lru_scan(urc, zeros_h, lru_p, 0, reverse=False, tc=tc_c)
        hx_f, _ = _lru_scan(urx, hc_f_last, lru_p, 0, reverse=False, tc=tc_x)
        rec_c, hc_b_first = _lru_scan(urc, zeros_h, lru_p, 1, reverse=True, hf=hc_f, uy=uyc, tc=tc_c)
        rec_x, _ = _lru_scan(urx, hc_b_first, lru_p, 1, reverse=True, hf=hx_f, uy=uyx, tc=tc_x)

        four_x = _fourier_long(ufx, four_tabs)

        if last:
            x1, h2, logits = _outproj(four_x, att_x, rec_x, x, mx[2], mx[3], mx[4], n2, w_out_b, w_rt, b_rt, tm=tm_x)
            ys, pos, wts = _moe([h2], logits, w1[l], w3[l], w2[l])
            x = _combine(x1, mx[5], ys, pos, wts, row_off=0, tm=min(MOE_COPY_TOKENS, S))
        else:
            att_c = _attention(att_sc, qc, [(ktc, vc, ktc.shape[1])], sub_g, out_scale=1.0 - lam_init,
                               tq=min(ATT_TQ, C), online_max=True, name="attn_ctx")
            four_c = _fourier_short(ufc, four_tabs[2], four_tabs[3])
            xc1, h2c, lgc = _outproj(four_c, att_c, rec_c, xc, mc[2], mc[3], mc[4], n2, w_out_b, w_rt, b_rt, tm=tm_c)
            x1, h2x, lgx = _outproj(four_x, att_x, rec_x, x, mx[2], mx[3], mx[4], n2, w_out_b, w_rt, b_rt, tm=tm_x)
            ys, pos, wts = _moe([h2c, h2x], jnp.concatenate([lgc, lgx], axis=0), w1[l], w3[l], w2[l])
            tmc = min(MOE_COPY_TOKENS, S, C)
            xc = _combine(xc1, mc[5], ys, pos, wts, row_off=0, tm=tmc)
            x = _combine(x1, mx[5], ys, pos, wts, row_off=n_ctx, tm=tmc)
    return x
```

```python
import functools
import math

import jax
import jax.numpy as jnp
import numpy as np
from jax import lax
from jax.experimental import pallas as pl
from jax.experimental.pallas import tpu as pltpu

F32 = jnp.float32
BF16 = jnp.bfloat16

GRID_W = 64
F_GROUPS = 4
ATT_HEADS = 4
LRU_BLOCKS = 4
LRU_C = 8.0
CONV_W = 4
CONV_LEFT = (CONV_W - 1) // 2
N_GROUPS = 4
EXPERTS_PER_GROUP = 8
N_EXPERTS = N_GROUPS * EXPERTS_PER_GROUP
TOP_K = 2
MOE_BLOCK = 512
MOE_COPY_TOKENS = 256
ROUTE_TOKENS = 512
ROUTE_OUT = 8
N_MOD = 6
EPS = 1e-6
ROPE_BASE = 10000.0
LOG2E = math.log2(math.e)
ATT_BOUND_MARGIN = 1.02
ATT_BOUND_MAX_SPAN = 100.0

LANES = 128
SUBLANES = 8
MXU_TILE = 256
VMEM_LIMIT = 48 * 1024 * 1024
DFT_T1 = 64
DFT_K1_PER_STEP = 8
DFT_COL_TILE = 4096
ROUTE_PAD = 128
ROW_TILE = 512
LRU_SUBCHUNK = 32
ATT_TQ = 512
ATT_BLOCKS_PER_BODY = 16


def _cparams(*sem):
    return pltpu.CompilerParams(dimension_semantics=sem, vmem_limit_bytes=VMEM_LIMIT)


def _mod_kernel(c_ref, w_ref, b_ref, o_ref):
    c = c_ref[...]
    s = c * jax.nn.sigmoid(c)
    o_ref[0] = jnp.dot(s, w_ref[0], preferred_element_type=F32,
                       precision=lax.Precision.HIGHEST) + b_ref[0]


def _modulation(c_all, w_mod, b_mod):
    L, D, n6 = w_mod.shape
    R = c_all.shape[0]
    tn = n6 // 4
    return pl.pallas_call(
        _mod_kernel,
        grid=(L, n6 // tn),
        in_specs=[pl.BlockSpec((R, D), lambda l, j: (0, 0)),
                  pl.BlockSpec((1, D, tn), lambda l, j: (l, 0, j)),
                  pl.BlockSpec((1, 1, tn), lambda l, j: (l, 0, j))],
        out_specs=pl.BlockSpec((1, R, tn), lambda l, j: (l, 0, j)),
        out_shape=jax.ShapeDtypeStruct((L, R, n6), F32),
        compiler_params=_cparams("arbitrary", "arbitrary"),
        name="modulation",
    )(c_all, w_mod, b_mod.reshape(L, 1, n6))


def _rms_mod(x, g, sc, sh):
    ms = jnp.mean(x * x, axis=-1, keepdims=True)
    return (x * lax.rsqrt(ms + EPS)) * g * (1.0 + sc) + sh


def _inproj_kernel(x_ref, sh_ref, sc_ref, g_ref, w_ref, gqk_ref, gmat_ref, cos_ref, sin_ref,
                   uf_ref, q_ref, kt_ref, v_ref, uy_ref, ur_ref, *, dims, use_rope):
    fw, aw, lw, qk_dim = dims
    q_off, k_off, v_off = fw, fw + aw, fw + 2 * aw
    y_off, r_off = v_off + aw, v_off + aw + lw
    h = _rms_mod(x_ref[0], g_ref[...], sc_ref[0], sh_ref[0])
    u = jnp.dot(h.astype(BF16), w_ref[...], preferred_element_type=F32)
    uf_ref[0] = u[:, :fw].astype(BF16)
    qk = u[:, q_off:v_off]
    sq = qk * qk
    hi = sq.astype(BF16)
    lo = (sq - hi.astype(F32)).astype(BF16)
    gm = gmat_ref[...]
    gw = gm.shape[0]
    parts = []
    for s in range(2 * aw // gw):
        sl = slice(s * gw, (s + 1) * gw)
        parts.append(jnp.dot(hi[:, sl], gm, preferred_element_type=F32)
                     + jnp.dot(lo[:, sl], gm, preferred_element_type=F32))
    msq = jnp.concatenate(parts, axis=1) * (1.0 / qk_dim)
    n = qk * lax.rsqrt(msq + EPS) * gqk_ref[...]
    if use_rope:
        reps = 2 * aw // LANES
        cos = jnp.concatenate([cos_ref[...]] * reps, axis=1)
        sin = jnp.concatenate([sin_ref[...]] * reps, axis=1)
        width = n.shape[1]
        half = qk_dim // 4
        lane = lax.broadcasted_iota(jnp.int32, n.shape, 1)
        swapped = jnp.where((lane % (2 * half)) < half,
                            pltpu.roll(n, width - half, 1), pltpu.roll(n, half, 1))
        n = n * cos + swapped * sin
    q_ref[0] = (n[:, :aw] * (qk_dim ** -0.5 * LOG2E)).astype(BF16)
    kt_ref[0, 0] = n[:, aw:].T.astype(BF16)
    v_ref[0] = u[:, v_off:y_off].astype(BF16)
    uy_ref[0] = u[:, y_off:r_off]
    ur_ref[0] = u[:, r_off:]


def _inproj(x, sh, sc, g, w_in, gqk, gmat, cos, sin, *, dims, use_rope, tm, name):
    B, T, D = x.shape
    fw, aw, lw, _ = dims
    n_in = w_in.shape[1]
    per_b = lambda b, i: (b, 0, 0)
    const = lambda b, i: (0, 0)
    tile = lambda b, i: (b, i, 0)
    return pl.pallas_call(
        functools.partial(_inproj_kernel, dims=dims, use_rope=use_rope),
        grid=(B, T // tm),
        in_specs=[pl.BlockSpec((1, tm, D), tile),
                  pl.BlockSpec((1, 1, D), per_b), pl.BlockSpec((1, 1, D), per_b),
                  pl.BlockSpec((1, D), const),
                  pl.BlockSpec((D, n_in), const),
                  pl.BlockSpec((1, 2 * aw), const),
                  pl.BlockSpec(gmat.shape, const),
                  pl.BlockSpec((tm, LANES), lambda b, i: (i, 0)),
                  pl.BlockSpec((tm, LANES), lambda b, i: (i, 0))],
        out_specs=[pl.BlockSpec((1, tm, fw), tile),
                   pl.BlockSpec((1, tm, aw), tile),
                   pl.BlockSpec((1, 1, aw, tm), lambda b, i: (b, i, 0, 0)),
                   pl.BlockSpec((1, tm, aw), tile),
                   pl.BlockSpec((1, tm, lw), tile),
                   pl.BlockSpec((1, tm, lw), tile)],
        out_shape=[jax.ShapeDtypeStruct((B, T, fw), BF16),
                   jax.ShapeDtypeStruct((B, T, aw), BF16),
                   jax.ShapeDtypeStruct((B, T // tm, aw, tm), BF16),
                   jax.ShapeDtypeStruct((B, T, aw), BF16),
                   jax.ShapeDtypeStruct((B, T, lw), F32),
                   jax.ShapeDtypeStruct((B, T, lw), F32)],
        compiler_params=_cparams("arbitrary", "arbitrary"),
        name=name,
    )(x, sh, sc, g, w_in, gqk, gmat, cos, sin)


def _attn_kernel(*refs, n_src, groups, out_scale, online_max):
    sc_ref, q_ref = refs[0], refs[1]
    kv_refs = refs[2:2 + 2 * n_src]
    g_ref, o_ref = refs[2 + 2 * n_src], refs[3 + 2 * n_src]
    q = q_ref[0].astype(F32)
    tq, w = q.shape
    lane = lax.broadcasted_iota(jnp.int32, q.shape, 1)
    qq = jnp.concatenate([jnp.where(lane < w // 2, q, 0.0),
                          jnp.where(lane >= w // 2, q, 0.0)], axis=0).astype(BF16)
    vd = kv_refs[1].shape[-1]

    if online_max:
        def step(kt, v, carry):
            m, l, acc = carry
            s = jnp.dot(qq, kt, preferred_element_type=F32)
            m_new = jnp.maximum(m, jnp.max(s, axis=-1, keepdims=True))
            alpha = jnp.exp2(m - m_new)
            p = jnp.exp2(s - m_new)
            l = alpha * l + jnp.sum(p, axis=-1, keepdims=True)
            acc = alpha * acc + jnp.dot(p.astype(BF16), v, preferred_element_type=F32)
            return m_new, l, acc

        carry = (jnp.full((2 * tq, 1), -jnp.inf, F32), jnp.zeros((2 * tq, 1), F32),
                 jnp.zeros((2 * tq, vd), F32))
    else:
        bound = sc_ref[1]

        def step(kt, v, carry):
            l_part, acc = carry
            s = jnp.dot(qq, kt, preferred_element_type=F32)
            p = jnp.exp2(s - bound)
            for c in range(s.shape[1] // LANES):
                l_part = l_part + p[:, c * LANES:(c + 1) * LANES]
            acc = acc + jnp.dot(p.astype(BF16), v, preferred_element_type=F32)
            return l_part, acc

        carry = (jnp.zeros((2 * tq, LANES), F32), jnp.zeros((2 * tq, vd), F32))

    for n in range(n_src):
        kt_ref, v_ref, grp = kv_refs[2 * n], kv_refs[2 * n + 1], groups[n]
        nblk, kb = kt_ref.shape[1], kt_ref.shape[3]
        if nblk == grp:
            for g in range(grp):
                carry = step(kt_ref[0, g], v_ref[0, g * kb:(g + 1) * kb, :], carry)
        else:
            def body(j, carry, kt_ref=kt_ref, v_ref=v_ref, grp=grp, kb=kb):
                for g in range(grp):
                    blk = j * grp + g
                    carry = step(kt_ref[0, blk], v_ref[0, pl.ds(pl.multiple_of(blk * kb, kb), kb), :], carry)
                return carry
            carry = lax.fori_loop(0, nblk // grp, body, carry)
    if online_max:
        _, l, acc = carry
    else:
        l_part, acc = carry
        l = jnp.sum(l_part, axis=-1, keepdims=True)
    o = acc / l
    d = o[:tq] - sc_ref[0] * o[tq:]
    ms = jnp.mean(d * d, axis=-1, keepdims=True)
    o_ref[0] = (d * lax.rsqrt(ms + EPS) * g_ref[...] * out_scale).astype(BF16)


def _attention(scalars, q, srcs, subln_g, *, out_scale, tq, online_max, name):
    B, S, aw = q.shape
    hd = aw // ATT_HEADS
    in_specs = [pl.BlockSpec(memory_space=pltpu.SMEM),
                pl.BlockSpec((1, tq, hd), lambda b, h, i: (b, i, h))]
    args = [scalars, q]
    for kt, v, _ in srcs:
        in_specs += [pl.BlockSpec((1, kt.shape[1], hd, kt.shape[3]), lambda b, h, i: (b, 0, h, 0)),
                     pl.BlockSpec((1, v.shape[1], hd), lambda b, h, i: (b, 0, h))]
        args += [kt, v]
    in_specs.append(pl.BlockSpec((1, hd), lambda b, h, i: (0, 0)))
    args.append(subln_g)
    return pl.pallas_call(
        functools.partial(_attn_kernel, n_src=len(srcs), groups=tuple(g for _, _, g in srcs),
                          out_scale=out_scale, online_max=online_max),
        grid=(B, ATT_HEADS, S // tq),
        in_specs=in_specs,
        out_specs=pl.BlockSpec((1, tq, hd), lambda b, h, i: (b, i, h)),
        out_shape=jax.ShapeDtypeStruct((B, S, aw), BF16),
        compiler_params=_cparams("arbitrary", "arbitrary", "arbitrary"),
        name=name,
    )(*args)


def _dft1_kernel(m_ref, z_ref, y_ref):
    y_ref[0] = jnp.dot(m_ref[...], z_ref[0], preferred_element_type=F32).astype(BF16)


def _dft2_kernel(y_ref, tab_ref, cc_ref, sc_ref, o_ref, *, scale):
    w = cc_ref.shape[0]
    for i in range(tab_ref.shape[0]):
        y = jnp.concatenate([y_ref[0, 0, i], y_ref[0, 1, i]], axis=0)
        zr = jnp.dot(tab_ref[i, 0], y, preferred_element_type=F32)
        zi = jnp.dot(tab_ref[i, 1], y, preferred_element_type=F32)
        o = (jnp.dot(zr.astype(BF16), cc_ref[...], preferred_element_type=F32)
             + jnp.dot(zi.astype(BF16), sc_ref[...], preferred_element_type=F32))
        o_ref[0, :, i * w:(i + 1) * w] = (o * scale).astype(BF16)


@functools.lru_cache(maxsize=None)
def _fourier_tables(T, fw):
    t1n, t2n = DFT_T1, T // DFT_T1
    gd = fw // F_GROUPS
    two_pi = 2.0 * np.pi
    k1 = np.arange(t1n, dtype=np.int64)
    a1 = two_pi * ((k1[:, None] * k1[None, :]) % t1n) / t1n
    m1 = np.concatenate([np.cos(a1), -np.sin(a1)], axis=0)
    k2 = np.arange(t2n, dtype=np.int64)
    kk = k1[:, None, None] + t1n * k2[None, :, None]
    ph = two_pi * ((kk * k2[None, None, :]) % T) / T
    cp, sp = np.cos(ph), np.sin(ph)
    tab = np.stack([np.concatenate([cp, sp], axis=-1),
                    np.concatenate([-sp, cp], axis=-1)], axis=1)
    c = np.arange(gd, dtype=np.int64)
    ac = two_pi * ((c[:, None] * c[None, :]) % gd) / gd
    eye = np.eye(F_GROUPS)
    cc, sc = np.kron(eye, np.cos(ac)), np.kron(eye, np.sin(ac))
    return tuple(np.asarray(t, np.float32) for t in (m1, tab, cc, sc))


def _fourier_long(uf, tables):
    B, T, W = uf.shape
    m1, tab, cc, sc = (jnp.asarray(t, BF16) for t in tables)
    t1n, t2n = DFT_T1, T // DFT_T1
    ncol = t2n * W
    tn = min(ncol, DFT_COL_TILE)
    y = pl.pallas_call(
        _dft1_kernel,
        grid=(B, ncol // tn),
        in_specs=[pl.BlockSpec((2 * t1n, t1n), lambda b, j: (0, 0)),
                  pl.BlockSpec((1, t1n, tn), lambda b, j: (b, 0, j))],
        out_specs=pl.BlockSpec((1, 2 * t1n, tn), lambda b, j: (b, 0, j)),
        out_shape=jax.ShapeDtypeStruct((B, 2 * t1n, ncol), BF16),
        compiler_params=_cparams("arbitrary", "arbitrary"),
        name="dft_stage1",
    )(m1, uf.reshape(B, t1n, ncol))
    y5 = y.reshape(B, 2, t1n, t2n, W)
    scale = 1.0 / math.sqrt(T * (W // F_GROUPS))
    out = pl.pallas_call(
        functools.partial(_dft2_kernel, scale=scale),
        grid=(t1n // DFT_K1_PER_STEP, B),
        in_specs=[pl.BlockSpec((1, 2, DFT_K1_PER_STEP, t2n, W), lambda k, b: (b, 0, k, 0, 0)),
                  pl.BlockSpec((DFT_K1_PER_STEP, 2, t2n, 2 * t2n), lambda k, b: (k, 0, 0, 0)),
                  pl.BlockSpec((W, W), lambda k, b: (0, 0)),
                  pl.BlockSpec((W, W), lambda k, b: (0, 0))],
        out_specs=pl.BlockSpec((1, t2n, DFT_K1_PER_STEP * W), lambda k, b: (b, 0, k)),
        out_shape=jax.ShapeDtypeStruct((B, t2n, t1n * W), BF16),
        compiler_params=_cparams("arbitrary", "arbitrary"),
        name="dft_stage2",
    )(y5, tab, cc, sc)
    return out.reshape(B, T, W)


def _dft_short_kernel(z_ref, ct_ref, st_ref, cc_ref, sc_ref, o_ref, *, scale):
    z = z_ref[0]
    zc = jnp.dot(z, cc_ref[...], preferred_element_type=F32).astype(BF16)
    zs = jnp.dot(z, sc_ref[...], preferred_element_type=F32).astype(BF16)
    o = (jnp.dot(ct_ref[...], zc, preferred_element_type=F32)
         - jnp.dot(st_ref[...], zs, preferred_element_type=F32))
    o_ref[0] = (o * scale).astype(BF16)


def _fourier_short(uf, cc, sc):
    B, T, W = uf.shape
    t = np.arange(T, dtype=np.int64)
    ang = 2.0 * np.pi * ((t[:, None] * t[None, :]) % T) / T
    ct, st = jnp.asarray(np.cos(ang), BF16), jnp.asarray(np.sin(ang), BF16)
    cc, sc = jnp.asarray(cc, BF16), jnp.asarray(sc, BF16)
    scale = 1.0 / math.sqrt(T * (W // F_GROUPS))
    return pl.pallas_call(
        functools.partial(_dft_short_kernel, scale=scale),
        grid=(B,),
        in_specs=[pl.BlockSpec((1, T, W), lambda b: (b, 0, 0)),
                  pl.BlockSpec((T, T), lambda b: (0, 0)), pl.BlockSpec((T, T), lambda b: (0, 0)),
                  pl.BlockSpec((W, W), lambda b: (0, 0)), pl.BlockSpec((W, W), lambda b: (0, 0))],
        out_specs=pl.BlockSpec((1, T, W), lambda b: (b, 0, 0)),
        out_shape=jax.ShapeDtypeStruct((B, T, W), BF16),
        compiler_params=_cparams("arbitrary"),
        name="dft_short",
    )(uf, ct, st, cc, sc)


def _affine_scan(a, b, reverse):
    T = a.shape[0]
    row = lax.broadcasted_iota(jnp.int32, a.shape, 0)
    k = 1
    while k < T:
        if k % SUBLANES == 0:
            one, zero = jnp.ones((k,) + a.shape[1:], a.dtype), jnp.zeros((k,) + a.shape[1:], a.dtype)
            if reverse:
                a_s, b_s = jnp.concatenate([a[k:], one], axis=0), jnp.concatenate([b[k:], zero], axis=0)
            else:
                a_s, b_s = jnp.concatenate([one, a[:T - k]], axis=0), jnp.concatenate([zero, b[:T - k]], axis=0)
        else:
            if reverse:
                a_s, b_s, valid = pltpu.roll(a, T - k, 0), pltpu.roll(b, T - k, 0), row < T - k
            else:
                a_s, b_s, valid = pltpu.roll(a, k, 0), pltpu.roll(b, k, 0), row >= k
            a_s, b_s = jnp.where(valid, a_s, 1.0), jnp.where(valid, b_s, 0.0)
        b = a * b_s + b
        a = a * a_s
        k *= 2
    return a, b


def _lru_kernel(*refs, reverse, combine, nt):
    if combine:
        (ur_ref, prev_ref, next_ref, h0_ref, cw_ref, cb_ref, wa_ref, ba_ref, wx_ref, bx_ref,
         nsp_ref, hf_ref, uy_ref, out_ref, hlast_ref, carry_ref) = refs
    else:
        (ur_ref, prev_ref, next_ref, h0_ref, cw_ref, cb_ref, wa_ref, ba_ref, wx_ref, bx_ref,
         nsp_ref, out_ref, hlast_ref, carry_ref) = refs
    i = pl.program_id(1)
    ci = nt - 1 - i if reverse else i

    @pl.when(i == 0)
    def _():
        carry_ref[...] = h0_ref[0]

    u = ur_ref[0]
    tc = u.shape[0]
    prev = jnp.where(ci == 0, 0.0, prev_ref[0])
    nxt = jnp.where(ci == nt - 1, 0.0, next_ref[0])
    ext = jnp.concatenate([prev, u, nxt], axis=0)
    n_ext = ext.shape[0]
    xr = cb_ref[...] + u * cw_ref[CONV_LEFT:CONV_LEFT + 1, :]
    for k in range(CONV_W):
        d = k - CONV_LEFT
        if d != 0:
            shifted = pltpu.roll(ext, (-d) % n_ext, 0)[SUBLANES:SUBLANES + tc]
            xr = xr + shifted * cw_ref[k:k + 1, :]
    xb = xr.astype(BF16)
    r = jax.nn.sigmoid(jnp.dot(xb, wa_ref[...], preferred_element_type=F32) + ba_ref[...])
    g = jax.nn.sigmoid(jnp.dot(xb, wx_ref[...], preferred_element_type=F32) + bx_ref[...])
    log_a = r * nsp_ref[...]
    a = jnp.exp(log_a)
    bt = jnp.sqrt(-jnp.tanh(log_a) * (a * a + 1.0)) * (g * xr)
    sub = math.gcd(LRU_SUBCHUNK, tc)
    n_sub = tc // sub
    parts = [None] * n_sub
    last = carry_ref[...]
    for j in (reversed(range(n_sub)) if reverse else range(n_sub)):
        a_cum, b_cum = _affine_scan(a[j * sub:(j + 1) * sub], bt[j * sub:(j + 1) * sub], reverse)
        hj = a_cum * last + b_cum
        last = hj[0:1] if reverse else hj[sub - 1:sub]
        parts[j] = hj
    h = jnp.concatenate(parts, axis=0) if n_sub > 1 else parts[0]
    carry_ref[...] = last
    hlast_ref[0] = last
    if combine:
        out_ref[0] = (jax.nn.gelu(uy_ref[0]) * (hf_ref[0] + h)).astype(BF16)
    else:
        out_ref[0] = h


def _lru_scan(ur, h0, p, d, *, reverse, hf=None, uy=None, tc):
    B, T, W = ur.shape
    nt = T // tc
    hb = tc // SUBLANES
    nh = T // SUBLANES
    combine = hf is not None
    cidx = (lambda i: nt - 1 - i) if reverse else (lambda i: i)
    tile = lambda b, i: (b, cidx(i), 0)
    const = lambda b, i: (0, 0)
    in_specs = [pl.BlockSpec((1, tc, W), tile),
                pl.BlockSpec((1, SUBLANES, W), lambda b, i: (b, jnp.maximum(cidx(i) * hb - 1, 0), 0)),
                pl.BlockSpec((1, SUBLANES, W), lambda b, i: (b, jnp.minimum((cidx(i) + 1) * hb, nh - 1), 0)),
                pl.BlockSpec((1, 1, W), lambda b, i: (b, 0, 0)),
                pl.BlockSpec((CONV_W, W), const), pl.BlockSpec((1, W), const),
                pl.BlockSpec((W, W), const), pl.BlockSpec((1, W), const),
                pl.BlockSpec((W, W), const), pl.BlockSpec((1, W), const),
                pl.BlockSpec((1, W), const)]
    args = [ur, ur, ur, h0.reshape(B, 1, W), p['conv_w'], p['conv_b'],
            p['wa'][d], p['ba'][d], p['wx'][d], p['bx'][d], p['nsp'][d]]
    if combine:
        in_specs += [pl.BlockSpec((1, tc, W), tile), pl.BlockSpec((1, tc, W), tile)]
        args += [hf, uy]
    out, hlast = pl.pallas_call(
        functools.partial(_lru_kernel, reverse=reverse, combine=combine, nt=nt),
        grid=(B, nt),
        in_specs=in_specs,
        out_specs=[pl.BlockSpec((1, tc, W), tile), pl.BlockSpec((1, 1, W), lambda b, i: (b, 0, 0))],
        out_shape=[jax.ShapeDtypeStruct((B, T, W), BF16 if combine else F32),
                   jax.ShapeDtypeStruct((B, 1, W), F32)],
        scratch_shapes=[pltpu.VMEM((1, W), F32)],
        compiler_params=_cparams("arbitrary", "arbitrary"),
        name=("lru_bwd" if reverse else "lru_fwd"),
    )(*args)
    return out, hlast.reshape(B, W)


def _store_token_tiles(ref, val, lead=()):
    rows, width = val.shape
    nsub = width // LANES
    for j in range(nsub):
        ref[lead + (pl.ds(j, rows, stride=nsub), slice(None))] = val[:, j * LANES:(j + 1) * LANES]


def _load_token_tiles(ref, rows, nsub, lead=()):
    return jnp.concatenate([ref[lead + (pl.ds(j, rows, stride=nsub), slice(None))] for j in range(nsub)], axis=1)


def _outproj_kernel(f_ref, a_ref, r_ref, x_ref, g1_ref, sh_ref, sc_ref, ng_ref, w_ref, wr_ref, br_ref,
                    x1_ref, h2_ref, lg_ref, *, fw, aw):
    y = (jnp.dot(f_ref[0], w_ref[0:fw, :], preferred_element_type=F32)
         + jnp.dot(a_ref[0], w_ref[fw:fw + aw, :], preferred_element_type=F32)
         + jnp.dot(r_ref[0], w_ref[fw + aw:, :], preferred_element_type=F32))
    x1 = x_ref[0] + g1_ref[0] * y
    x1_ref[0] = x1
    h2 = _rms_mod(x1, ng_ref[...], sc_ref[0], sh_ref[0])
    _store_token_tiles(h2_ref, h2, lead=(0,))
    lg_ref[0] = jnp.dot(h2.astype(BF16), wr_ref[...], preferred_element_type=F32) + br_ref[...]


def _outproj(four, att, rec, x, g1, sh2, sc2, ng, w_out, w_rt, b_rt, *, tm):
    B, T, D = x.shape
    fw, aw = four.shape[2], att.shape[2]
    nsub = D // LANES
    tile = lambda b, i: (b, i, 0)
    per_b = lambda b, i: (b, 0, 0)
    const = lambda b, i: (0, 0)
    x1, h2, logits = pl.pallas_call(
        functools.partial(_outproj_kernel, fw=fw, aw=aw),
        grid=(B, T // tm),
        in_specs=[pl.BlockSpec((1, tm, fw), tile), pl.BlockSpec((1, tm, aw), tile),
                  pl.BlockSpec((1, tm, rec.shape[2]), tile), pl.BlockSpec((1, tm, D), tile),
                  pl.BlockSpec((1, 1, D), per_b), pl.BlockSpec((1, 1, D), per_b),
                  pl.BlockSpec((1, 1, D), per_b), pl.BlockSpec((1, D), const),
                  pl.BlockSpec((D, D), const), pl.BlockSpec((D, ROUTE_PAD), const),
                  pl.BlockSpec((1, ROUTE_PAD), const)],
        out_specs=[pl.BlockSpec((1, tm, D), tile), pl.BlockSpec((1, tm * nsub, LANES), tile),
                   pl.BlockSpec((1, tm, ROUTE_PAD), tile)],
        out_shape=[jax.ShapeDtypeStruct((B, T, D), F32),
                   jax.ShapeDtypeStruct((B, T * nsub, LANES), F32),
                   jax.ShapeDtypeStruct((B, T, ROUTE_PAD), F32)],
        compiler_params=_cparams("arbitrary", "arbitrary"),
        name="outproj",
    )(four, att, rec, x, g1, sh2, sc2, ng, w_out, w_rt, b_rt)
    return x1, h2.reshape(B * T * nsub, LANES), logits.reshape(B * T, ROUTE_PAD)


def _dispatch_kernel(pe_ref, nu_ref, pos_ref, *refs, part_tiles, nsub, n_blocks):
    n_parts = len(part_tiles)
    h_refs = refs[:n_parts]
    xs_ref, zbuf, sem, zsem = refs[n_parts:]
    i = pl.program_id(0)
    blk = MOE_BLOCK * nsub

    def zero_copy(row0):
        return pltpu.make_async_copy(zbuf, xs_ref.at[pl.ds(pl.multiple_of(row0, blk), blk)], zsem)

    @pl.when(i == 0)
    def _():
        zbuf[...] = jnp.zeros_like(zbuf)

        def expert_block(e, start, wait):
            pend = pe_ref[e]

            @pl.when(pend > start)
            def _():
                if wait:
                    zero_copy(0).wait()
                else:
                    zero_copy((pend - MOE_BLOCK) * nsub).start()
            return pend

        def tail_block(b, c, wait):
            if wait:
                zero_copy(0).wait()
            else:
                zero_copy(b * blk).start()
            return c

        for wait in (False, True):
            lax.fori_loop(0, N_EXPERTS, functools.partial(expert_block, wait=wait), 0)
            lax.fori_loop(nu_ref[0], n_blocks, functools.partial(tail_block, wait=wait), 0)

    first = 0
    for h_ref, nt in zip(h_refs, part_tiles):
        tm = h_ref.shape[0] // nsub

        @pl.when((i >= first) & (i < first + nt))
        def _(h_ref=h_ref, tm=tm):
            def issue(t, c):
                src = h_ref.at[pl.ds(pl.multiple_of(t * nsub, nsub), nsub)]
                for k in range(TOP_K):
                    dst = pl.multiple_of(pos_ref[0, 0, TOP_K * t + k] * nsub, nsub)
                    pltpu.make_async_copy(src, xs_ref.at[pl.ds(dst, nsub)], sem).start(priority=k % 2)
                return c

            lax.fori_loop(0, tm, issue, 0, unroll=4)
            for k in range(TOP_K):
                pltpu.make_async_copy(h_ref, xs_ref.at[pl.ds(0, tm * nsub)], sem).wait()
        first += nt


def _dispatch(h2_parts, pos, pends, n_used, *, n_slots, tm, nsub):
    part_tiles = tuple(h.shape[0] // (nsub * tm) for h in h2_parts)
    firsts = [sum(part_tiles[:p]) for p in range(len(part_tiles))]
    in_specs = [pl.BlockSpec((1, 1, TOP_K * tm), lambda i, pe, nu: (i, 0, 0), memory_space=pltpu.SMEM)]
    for first, nt in zip(firsts, part_tiles):
        in_specs.append(pl.BlockSpec(
            (tm * nsub, LANES), lambda i, pe, nu, first=first, nt=nt: (jnp.clip(i - first, 0, nt - 1), 0)))
    return pl.pallas_call(
        functools.partial(_dispatch_kernel, part_tiles=part_tiles, nsub=nsub, n_blocks=n_slots // MOE_BLOCK),
        grid_spec=pltpu.PrefetchScalarGridSpec(
            num_scalar_prefetch=2,
            grid=(sum(part_tiles),),
            in_specs=in_specs,
            out_specs=pl.BlockSpec(memory_space=pl.ANY),
            scratch_shapes=[pltpu.VMEM((MOE_BLOCK * nsub, LANES), F32),
                            pltpu.SemaphoreType.DMA(()), pltpu.SemaphoreType.DMA(())]),
        out_shape=jax.ShapeDtypeStruct((n_slots * nsub, LANES), F32),
        compiler_params=_cparams("arbitrary"),
        name="moe_dispatch",
    )(pends, n_used, pos.reshape(-1, 1, TOP_K * tm), *h2_parts)


def _expert_kernel(be_ref, nu_ref, x_ref, w1_ref, w3_ref, w2_ref, o_ref):
    i = pl.program_id(0)
    nsub = w2_ref.shape[3] // LANES

    @pl.when(i < nu_ref[0])
    def _():
        x = _load_token_tiles(x_ref, MOE_BLOCK, nsub)
        xb = x.astype(BF16)
        h1 = jnp.dot(xb, w1_ref[0, 0].astype(BF16), preferred_element_type=F32)
        h3 = jnp.dot(xb, w3_ref[0, 0].astype(BF16), preferred_element_type=F32)
        hb = (jax.nn.silu(h1) * h3).astype(BF16)
        _store_token_tiles(o_ref, jnp.dot(hb, w2_ref[0, 0].astype(BF16), preferred_element_type=F32))

    @pl.when(i >= nu_ref[0])
    def _():
        o_ref[...] = jnp.zeros_like(o_ref)


def _experts(xs, block_exp, n_used, w1, w3, w2, layer):
    de, D = w2.shape[2], w2.shape[3]
    nsub = D // LANES
    R = MOE_BLOCK
    P = xs.shape[0] // nsub
    clamp = lambda i, be, nu: (jnp.minimum(i, nu[0] - 1), 0)
    return pl.pallas_call(
        _expert_kernel,
        grid_spec=pltpu.PrefetchScalarGridSpec(
            num_scalar_prefetch=2,
            grid=(P // R,),
            in_specs=[pl.BlockSpec((R * nsub, LANES), clamp),
                      pl.BlockSpec((1, 1, D, de), lambda i, be, nu: (layer, be[i], 0, 0)),
                      pl.BlockSpec((1, 1, D, de), lambda i, be, nu: (layer, be[i], 0, 0)),
                      pl.BlockSpec((1, 1, de, D), lambda i, be, nu: (layer, be[i], 0, 0))],
            out_specs=pl.BlockSpec((R * nsub, LANES), lambda i, be, nu: (i, 0))),
        out_shape=jax.ShapeDtypeStruct((P * nsub, LANES), F32),
        compiler_params=_cparams("arbitrary"),
        name="moe_experts",
    )(block_exp, n_used, xs, w1, w3, w2)


def _combine_kernel(pos_ref, nxt_ref, x_ref, g_ref, w_ref, ys_ref, o_ref, buf, sems):
    tm = x_ref.shape[1]
    nsub = x_ref.shape[2] // LANES
    step = pl.program_id(0) * pl.num_programs(1) + pl.program_id(1)
    n_steps = pl.num_programs(0) * pl.num_programs(1)
    slot = step % 2

    def issue(idx_ref, into):
        def body(t, c):
            for k in range(TOP_K):
                src = pl.multiple_of(idx_ref[0, 0, TOP_K * t + k] * nsub, nsub)
                pltpu.make_async_copy(ys_ref.at[pl.ds(src, nsub)],
                                      buf.at[into, k, pl.ds(pl.multiple_of(t * nsub, nsub), nsub)],
                                      sems.at[into]).start(priority=k % 2)
            return c
        lax.fori_loop(0, tm, body, 0, unroll=4)

    @pl.when(step == 0)
    def _():
        issue(pos_ref, 0)

    @pl.when(step + 1 < n_steps)
    def _():
        issue(nxt_ref, 1 - slot)

    for k in range(TOP_K):
        pltpu.make_async_copy(ys_ref.at[pl.ds(0, tm * nsub)], buf.at[slot, k], sems.at[slot]).wait()
    w = w_ref[0]
    y = w[:, 0:1] * _load_token_tiles(buf.at[slot, 0], tm, nsub)
    for k in range(1, TOP_K):
        y = y + w[:, k:k + 1] * _load_token_tiles(buf.at[slot, k], tm, nsub)
    o_ref[0] = x_ref[0] + g_ref[0] * y


def _combine(x1, g2, ys, pos, wts, *, row_off, tm):
    B, T, D = x1.shape
    nt = T // tm
    off = row_off // tm
    last = off + B * nt - 1
    pos3 = pos.reshape(-1, 1, TOP_K * tm)
    wts3 = wts.reshape(-1, tm, TOP_K)
    return pl.pallas_call(
        _combine_kernel,
        grid=(B, nt),
        in_specs=[pl.BlockSpec((1, 1, TOP_K * tm), lambda b, i: (b * nt + i + off, 0, 0),
                               memory_space=pltpu.SMEM),
                  pl.BlockSpec((1, 1, TOP_K * tm), lambda b, i: (jnp.minimum(b * nt + i + off + 1, last), 0, 0),
                               memory_space=pltpu.SMEM),
                  pl.BlockSpec((1, tm, D), lambda b, i: (b, i, 0)),
                  pl.BlockSpec((1, 1, D), lambda b, i: (b, 0, 0)),
                  pl.BlockSpec((1, tm, TOP_K), lambda b, i: (b * nt + i + off, 0, 0)),
                  pl.BlockSpec(memory_space=pl.ANY)],
        out_specs=pl.BlockSpec((1, tm, D), lambda b, i: (b, i, 0)),
        out_shape=jax.ShapeDtypeStruct((B, T, D), F32),
        scratch_shapes=[pltpu.VMEM((2, TOP_K, tm * (D // LANES), LANES), F32), pltpu.SemaphoreType.DMA((2,))],
        compiler_params=_cparams("arbitrary", "arbitrary"),
        name="moe_combine",
    )(pos3, pos3, x1, g2, wts3, ys)


def _route_kernel(lg_ref, tri_ref, o_ref, cnt_ref, base_ref):
    @pl.when(pl.program_id(0) == 0)
    def _():
        base_ref[...] = jnp.zeros_like(base_ref)

    lg = lg_ref[...]
    lane = lax.broadcasted_iota(jnp.int32, lg.shape, 1)
    lane_f = lane.astype(F32)
    ninf = jnp.float32(-jnp.inf)
    far = jnp.float32(ROUTE_PAD)
    is_g = lane < N_GROUPS
    gl = jnp.where(is_g, lg, ninf)
    gmax = jnp.max(gl, axis=-1, keepdims=True)
    gsum = jnp.sum(jnp.where(is_g, jnp.exp(gl - gmax), 0.0), axis=-1, keepdims=True)
    p_g = 1.0 / gsum
    g_idx = jnp.min(jnp.where(gl == gmax, lane_f, far), axis=-1, keepdims=True)
    e_lane = lane - N_GROUPS
    e_lane_f = e_lane.astype(F32)
    shift = EXPERTS_PER_GROUP.bit_length() - 1
    in_grp = ((e_lane >= 0) & (e_lane < N_EXPERTS)
              & (jnp.right_shift(e_lane, shift).astype(F32) == g_idx))
    el = jnp.where(in_grp, lg, ninf)
    v1 = jnp.max(el, axis=-1, keepdims=True)
    i1 = jnp.min(jnp.where(el == v1, e_lane_f, far), axis=-1, keepdims=True)
    el2 = jnp.where(e_lane_f == i1, ninf, el)
    v2 = jnp.max(el2, axis=-1, keepdims=True)
    i2 = jnp.min(jnp.where(el2 == v2, e_lane_f, far), axis=-1, keepdims=True)
    w1 = p_g / (1.0 + jnp.exp(v2 - v1))
    w2 = p_g - w1
    hit1, hit2 = e_lane_f == i1, e_lane_f == i2
    oh1, oh2 = hit1.astype(F32), hit2.astype(F32)
    tri = tri_ref[...]
    pre1 = jnp.dot(tri, oh1.astype(BF16), preferred_element_type=F32)
    pre2 = jnp.dot(tri, oh2.astype(BF16), preferred_element_type=F32)
    tot1 = jnp.sum(oh1, axis=0, keepdims=True)
    tot2 = jnp.sum(oh2, axis=0, keepdims=True)
    base = base_ref[...]
    r1 = jnp.sum(jnp.where(hit1, pre1 + base, 0.0), axis=-1, keepdims=True)
    r2 = jnp.sum(jnp.where(hit2, pre2 + (base + tot1), 0.0), axis=-1, keepdims=True)
    base = base + tot1 + tot2
    base_ref[...] = base
    cnt_ref[...] = base
    cols = (i1, i2, r1, r2, w1, w2)
    out = jnp.zeros(lg.shape, F32)
    for j, col in enumerate(cols):
        out = jnp.where(lane == j, col, out)
    o_ref[...] = out[:, :o_ref.shape[1]]


def _route(logits):
    N = logits.shape[0]
    tm = math.gcd(ROUTE_TOKENS, N)
    tri = jnp.asarray(np.tril(np.ones((tm, tm), np.float32), -1), BF16)
    out, cnt = pl.pallas_call(
        _route_kernel,
        grid=(N // tm,),
        in_specs=[pl.BlockSpec((tm, ROUTE_PAD), lambda i: (i, 0)),
                  pl.BlockSpec((tm, tm), lambda i: (0, 0))],
        out_specs=[pl.BlockSpec((tm, ROUTE_OUT), lambda i: (i, 0)),
                   pl.BlockSpec((1, ROUTE_PAD), lambda i: (0, 0))],
        out_shape=[jax.ShapeDtypeStruct((N, ROUTE_OUT), F32),
                   jax.ShapeDtypeStruct((1, ROUTE_PAD), F32)],
        scratch_shapes=[pltpu.VMEM((1, ROUTE_PAD), F32)],
        compiler_params=_cparams("arbitrary"),
        name="moe_route",
    )(logits, tri)
    eid = out[:, 0:TOP_K].astype(jnp.int32)
    rank = out[:, TOP_K:2 * TOP_K].astype(jnp.int32)
    wts = out[:, 2 * TOP_K:3 * TOP_K]
    counts = cnt[0, N_GROUPS:N_GROUPS + N_EXPERTS].astype(jnp.int32)
    return eid, rank, wts, counts


def _dispatch_plan(eid, rank, counts):
    N = eid.shape[0]
    A = N * TOP_K
    padded = (counts + MOE_BLOCK - 1) // MOE_BLOCK * MOE_BLOCK
    pends = jnp.cumsum(padded)
    pstarts = pends - padded
    hit = eid[..., None] == jnp.arange(N_EXPERTS, dtype=jnp.int32)
    dest = jnp.sum(jnp.where(hit, pstarts, 0), axis=-1) + rank
    n_blocks = -(-A // MOE_BLOCK) + N_EXPERTS
    starts = jnp.arange(n_blocks, dtype=jnp.int32) * MOE_BLOCK
    block_exp = jnp.minimum(jnp.sum((pends[None, :] <= starts[:, None]).astype(jnp.int32), axis=1),
                            N_EXPERTS - 1).astype(jnp.int32)
    n_used = (pends[-1] // MOE_BLOCK).astype(jnp.int32).reshape(1)
    return block_exp, n_used, dest, pends.astype(jnp.int32), n_blocks * MOE_BLOCK


def _moe(h2_parts, logits, w1, w3, w2, layer):
    eid, rank, wts, counts = _route(logits)
    block_exp, n_used, pos, pends, n_slots = _dispatch_plan(eid, rank, counts)
    nsub = w2.shape[3] // LANES
    tm = functools.reduce(math.gcd, [h.shape[0] // nsub for h in h2_parts], MOE_COPY_TOKENS)
    xs = _dispatch(h2_parts, pos, pends, n_used, n_slots=n_slots, tm=tm, nsub=nsub)
    ys = _experts(xs, block_exp, n_used, w1, w3, w2, layer)
    return ys, pos, wts


def _blockdiag(w):
    G, n, _ = w.shape
    eye = jnp.eye(G, dtype=w.dtype)
    return (eye[:, None, :, None] * w[:, :, None, :]).reshape(G * n, G * n)


@functools.lru_cache(maxsize=None)
def _rope_tables(S, qk_dim):
    half = qk_dim // 2
    nf = half // 2
    rows_n = S // GRID_W
    row = np.repeat(np.arange(rows_n, dtype=np.float32), GRID_W)
    col = np.tile(np.arange(GRID_W, dtype=np.float32), rows_n)
    freqs = (np.float32(ROPE_BASE) ** (-np.arange(nf, dtype=np.float32) / np.float32(nf))).astype(np.float32)
    ang_r = (row[:, None] * freqs).astype(np.float64)
    ang_c = (col[:, None] * freqs).astype(np.float64)
    cos = np.concatenate([np.cos(ang_r)] * 2 + [np.cos(ang_c)] * 2, axis=1)
    sin = np.concatenate([-np.sin(ang_r), np.sin(ang_r), -np.sin(ang_c), np.sin(ang_c)], axis=1)
    reps = LANES // qk_dim
    return np.tile(cos, (1, reps)).astype(np.float32), np.tile(sin, (1, reps)).astype(np.float32)


def kernel(x, c, ctx, c_ctx, w_mod, b_mod, norm1_g, norm2_g, w_in, q_norm_g, k_norm_g, lambda_q1, lambda_k1, lambda_q2, lambda_k2, subln_g, conv_w, conv_b, gate_a_w, gate_a_b, gate_x_w, gate_x_b, lru_lambda, w_out, w_group, b_group, w_router, b_router, w1, w3, w2):
    B, S, D = x.shape
    C = ctx.shape[1]
    L = w_mod.shape[0]
    qk_dim = q_norm_g.shape[1]
    fw = lw = D // 4
    aw = D // 2
    dims = (fw, aw, lw, qk_dim)
    tm_x, tm_c = min(ROW_TILE, S), min(ROW_TILE, C)
    n_ctx, n_lat = B * C, B * S

    n_rows = -(-(B + 1) // SUBLANES) * SUBLANES
    c_all = jnp.concatenate([c, c_ctx[None, :], jnp.zeros((n_rows - B - 1, D), F32)], axis=0)
    mod = _modulation(c_all, w_mod, b_mod)

    cos_t, sin_t = (jnp.asarray(t) for t in _rope_tables(S, qk_dim))
    dummy_tab = jnp.zeros((C, LANES), F32)
    four_tabs = _fourier_tables(S, fw)
    gmat = _blockdiag(jnp.ones((MXU_TILE // qk_dim, qk_dim, qk_dim), F32)).astype(BF16)

    xc = ctx
    for l in range(L):
        last = l == L - 1
        lam_init = 0.8 - 0.6 * math.exp(-0.3 * l)
        m = [mod[l, :, i * D:(i + 1) * D] for i in range(N_MOD)]
        mx = [a[:B, None, :] for a in m]
        mc = [jnp.broadcast_to(a[B][None, None, :], (B, 1, D)) for a in m]
        w_in_b = w_in[l].astype(BF16)
        w_out_b = w_out[l].astype(BF16)
        gqk = jnp.concatenate([jnp.tile(q_norm_g[l], aw // qk_dim),
                               jnp.tile(k_norm_g[l], aw // qk_dim)])[None, :]
        lam = (jnp.exp(jnp.sum(lambda_q1[l] * lambda_k1[l])) - jnp.exp(jnp.sum(lambda_q2[l] * lambda_k2[l]))
               + lam_init).astype(F32)
        s_bound = (ATT_BOUND_MARGIN * qk_dim ** 0.5 * LOG2E
                   * jnp.max(jnp.abs(q_norm_g[l])) * jnp.max(jnp.abs(k_norm_g[l]))).astype(F32)
        att_sc = jnp.stack([lam, s_bound])
        sub_g = subln_g[l][None, :]
        n1 = norm1_g[l][None, :]
        n2 = norm2_g[l][None, :]
        lru_p = {
            'conv_w': conv_w[l], 'conv_b': conv_b[l][None, :],
            'wa': [_blockdiag(gate_a_w[l, d]).astype(BF16) for d in range(2)],
            'wx': [_blockdiag(gate_x_w[l, d]).astype(BF16) for d in range(2)],
            'ba': [gate_a_b[l, d][None, :] for d in range(2)],
            'bx': [gate_x_b[l, d][None, :] for d in range(2)],
            'nsp': [(-LRU_C * jax.nn.softplus(-lru_lambda[l, d]))[None, :] for d in range(2)],
        }
        pad = ROUTE_PAD - N_GROUPS - N_EXPERTS
        w_rt = jnp.concatenate([w_group[l], w_router[l], jnp.zeros((D, pad), F32)], axis=1).astype(BF16)
        b_rt = jnp.concatenate([b_group[l], b_router[l], jnp.zeros((pad,), F32)])[None, :]

        ufc, qc, ktc, vc, uyc, urc = _inproj(xc, mc[0], mc[1], n1, w_in_b, gqk, gmat, dummy_tab, dummy_tab,
                                             dims=dims, use_rope=False, tm=tm_c, name="inproj_ctx")
        ufx, qx, ktx, vx, uyx, urx = _inproj(x, mx[0], mx[1], n1, w_in_b, gqk, gmat, cos_t, sin_t,
                                             dims=dims, use_rope=True, tm=tm_x, name="inproj_lat")
        grp_x = min(ATT_BLOCKS_PER_BODY, ktx.shape[1])
        att_srcs = [(ktc, vc, ktc.shape[1]), (ktx, vx, grp_x)]
        att_fn = functools.partial(_attention, att_sc, qx, att_srcs, sub_g, out_scale=1.0 - lam_init,
                                   tq=min(ATT_TQ, S))
        att_x = lax.cond(2.0 * s_bound <= ATT_BOUND_MAX_SPAN,
                         lambda: att_fn(online_max=False, name="attn_lat_bound"),
                         lambda: att_fn(online_max=True, name="attn_lat_online"))

        zeros_h = jnp.zeros((B, lw), F32)
        tc_c, tc_x = min(ROW_TILE, C), min(ROW_TILE, S)
        hc_f, hc_f_last = _lru_scan(urc, zeros_h, lru_p, 0, reverse=False, tc=tc_c)
        hx_f, _ = _lru_scan(urx, hc_f_last, lru_p, 0, reverse=False, tc=tc_x)
        rec_c, hc_b_first = _lru_scan(urc, zeros_h, lru_p, 1, reverse=True, hf=hc_f, uy=uyc, tc=tc_c)
        rec_x, _ = _lru_scan(urx, hc_b_first, lru_p, 1, reverse=True, hf=hx_f, uy=uyx, tc=tc_x)

        four_x = _fourier_long(ufx, four_tabs)

        if last:
            x1, h2, logits = _outproj(four_x, att_x, rec_x, x, mx[2], mx[3], mx[4], n2, w_out_b, w_rt, b_rt, tm=tm_x)
            ys, pos, wts = _moe([h2], logits, w1, w3, w2, l)
            x = _combine(x1, mx[5], ys, pos, wts, row_off=0, tm=min(MOE_COPY_TOKENS, S))
        else:
            att_c = _attention(att_sc, qc, [(ktc, vc, ktc.shape[1])], sub_g, out_scale=1.0 - lam_init,
                               tq=min(ATT_TQ, C), online_max=True, name="attn_ctx")
            four_c = _fourier_short(ufc, four_tabs[2], four_tabs[3])
            xc1, h2c, lgc = _outproj(four_c, att_c, rec_c, xc, mc[2], mc[3], mc[4], n2, w_out_b, w_rt, b_rt, tm=tm_c)
            x1, h2x, lgx = _outproj(four_x, att_x, rec_x, x, mx[2], mx[3], mx[4], n2, w_out_b, w_rt, b_rt, tm=tm_x)
            ys, pos, wts = _moe([h2c, h2x], jnp.concatenate([lgc, lgx], axis=0), w1, w3, w2, l)
            tmc = min(MOE_COPY_TOKENS, S, C)
            xc = _combine(xc1, mc[5], ys, pos, wts, row_off=0, tm=tmc)
            x = _combine(x1, mx[5], ys, pos, wts, row_off=n_ctx, tm=tmc)
    return x
```

```python
import functools
import math

import jax
import jax.numpy as jnp
import numpy as np
from jax import lax
from jax.experimental import pallas as pl
from jax.experimental.pallas import tpu as pltpu

F32 = jnp.float32
BF16 = jnp.bfloat16

GRID_W = 64
F_GROUPS = 4
ATT_HEADS = 4
LRU_BLOCKS = 4
LRU_C = 8.0
CONV_W = 4
CONV_LEFT = (CONV_W - 1) // 2
N_GROUPS = 4
EXPERTS_PER_GROUP = 8
N_EXPERTS = N_GROUPS * EXPERTS_PER_GROUP
TOP_K = 2
MOE_BLOCK = 512
MOE_COPY_TOKENS = 256
ROUTE_TOKENS = 512
ROUTE_OUT = 8
N_MOD = 6
EPS = 1e-6
ROPE_BASE = 10000.0
LOG2E = math.log2(math.e)
ATT_BOUND_MARGIN = 1.02
ATT_BOUND_MAX_SPAN = 100.0

LANES = 128
SUBLANES = 8
MXU_TILE = 256
VMEM_LIMIT = 48 * 1024 * 1024
DFT_T1 = 64
DFT_K1_PER_STEP = 8
DFT_COL_TILE = 4096
ROUTE_PAD = 128
ROW_TILE = 512
LRU_SUBCHUNK = 32
ATT_TQ = 512
ATT_BLOCKS_PER_BODY = 16


def _cparams(*sem):
    return pltpu.CompilerParams(dimension_semantics=sem, vmem_limit_bytes=VMEM_LIMIT)


def _mod_kernel(c_ref, w_ref, b_ref, o_ref):
    c = c_ref[...]
    s = c * jax.nn.sigmoid(c)
    o_ref[0] = jnp.dot(s, w_ref[0], preferred_element_type=F32,
                       precision=lax.Precision.HIGHEST) + b_ref[0]


def _modulation(c_all, w_mod, b_mod):
    L, D, n6 = w_mod.shape
    R = c_all.shape[0]
    tn = n6 // 4
    return pl.pallas_call(
        _mod_kernel,
        grid=(L, n6 // tn),
        in_specs=[pl.BlockSpec((R, D), lambda l, j: (0, 0)),
                  pl.BlockSpec((1, D, tn), lambda l, j: (l, 0, j)),
                  pl.BlockSpec((1, 1, tn), lambda l, j: (l, 0, j))],
        out_specs=pl.BlockSpec((1, R, tn), lambda l, j: (l, 0, j)),
        out_shape=jax.ShapeDtypeStruct((L, R, n6), F32),
        compiler_params=_cparams("arbitrary", "arbitrary"),
        name="modulation",
    )(c_all, w_mod, b_mod.reshape(L, 1, n6))


def _rms_mod(x, g, sc, sh):
    ms = jnp.mean(x * x, axis=-1, keepdims=True)
    return (x * lax.rsqrt(ms + EPS)) * g * (1.0 + sc) + sh


def _inproj_kernel(x_ref, sh_ref, sc_ref, g_ref, w_ref, gqk_ref, gmat_ref, cos_ref, sin_ref,
                   uf_ref, q_ref, kt_ref, v_ref, uy_ref, ur_ref, *, dims, use_rope):
    fw, aw, lw, qk_dim = dims
    q_off, k_off, v_off = fw, fw + aw, fw + 2 * aw
    y_off, r_off = v_off + aw, v_off + aw + lw
    h = _rms_mod(x_ref[0], g_ref[...], sc_ref[0], sh_ref[0])
    u = jnp.dot(h.astype(BF16), w_ref[...], preferred_element_type=F32)
    uf_ref[0] = u[:, :fw].astype(BF16)
    qk = u[:, q_off:v_off]
    sq = qk * qk
    hi = sq.astype(BF16)
    lo = (sq - hi.astype(F32)).astype(BF16)
    gm = gmat_ref[...]
    gw = gm.shape[0]
    parts = []
    for s in range(2 * aw // gw):
        sl = slice(s * gw, (s + 1) * gw)
        parts.append(jnp.dot(hi[:, sl], gm, preferred_element_type=F32)
                     + jnp.dot(lo[:, sl], gm, preferred_element_type=F32))
    msq = jnp.concatenate(parts, axis=1) * (1.0 / qk_dim)
    n = qk * lax.rsqrt(msq + EPS) * gqk_ref[...]
    if use_rope:
        reps = 2 * aw // LANES
        cos = jnp.concatenate([cos_ref[...]] * reps, axis=1)
        sin = jnp.concatenate([sin_ref[...]] * reps, axis=1)
        width = n.shape[1]
        half = qk_dim // 4
        lane = lax.broadcasted_iota(jnp.int32, n.shape, 1)
        swapped = jnp.where((lane % (2 * half)) < half,
                            pltpu.roll(n, width - half, 1), pltpu.roll(n, half, 1))
        n = n * cos + swapped * sin
    q_ref[0] = (n[:, :aw] * (qk_dim ** -0.5 * LOG2E)).astype(BF16)
    kt_ref[0, 0] = n[:, aw:].T.astype(BF16)
    v_ref[0] = u[:, v_off:y_off].astype(BF16)
    uy_ref[0] = u[:, y_off:r_off]
    ur_ref[0] = u[:, r_off:]


def _inproj(x, sh, sc, g, w_in, gqk, gmat, cos, sin, *, dims, use_rope, tm, name):
    B, T, D = x.shape
    fw, aw, lw, _ = dims
    n_in = w_in.shape[1]
    per_b = lambda b, i: (b, 0, 0)
    const = lambda b, i: (0, 0)
    tile = lambda b, i: (b, i, 0)
    return pl.pallas_call(
        functools.partial(_inproj_kernel, dims=dims, use_rope=use_rope),
        grid=(B, T // tm),
        in_specs=[pl.BlockSpec((1, tm, D), tile),
                  pl.BlockSpec((1, 1, D), per_b), pl.BlockSpec((1, 1, D), per_b),
                  pl.BlockSpec((1, D), const),
                  pl.BlockSpec((D, n_in), const),
                  pl.BlockSpec((1, 2 * aw), const),
                  pl.BlockSpec(gmat.shape, const),
                  pl.BlockSpec((tm, LANES), lambda b, i: (i, 0)),
                  pl.BlockSpec((tm, LANES), lambda b, i: (i, 0))],
        out_specs=[pl.BlockSpec((1, tm, fw), tile),
                   pl.BlockSpec((1, tm, aw), tile),
                   pl.BlockSpec((1, 1, aw, tm), lambda b, i: (b, i, 0, 0)),
                   pl.BlockSpec((1, tm, aw), tile),
                   pl.BlockSpec((1, tm, lw), tile),
                   pl.BlockSpec((1, tm, lw), tile)],
        out_shape=[jax.ShapeDtypeStruct((B, T, fw), BF16),
                   jax.ShapeDtypeStruct((B, T, aw), BF16),
                   jax.ShapeDtypeStruct((B, T // tm, aw, tm), BF16),
                   jax.ShapeDtypeStruct((B, T, aw), BF16),
                   jax.ShapeDtypeStruct((B, T, lw), F32),
                   jax.ShapeDtypeStruct((B, T, lw), F32)],
        compiler_params=_cparams("arbitrary", "arbitrary"),
        name=name,
    )(x, sh, sc, g, w_in, gqk, gmat, cos, sin)


def _attn_kernel(*refs, n_src, groups, out_scale, online_max):
    sc_ref, q_ref = refs[0], refs[1]
    kv_refs = refs[2:2 + 2 * n_src]
    g_ref, o_ref = refs[2 + 2 * n_src], refs[3 + 2 * n_src]
    q = q_ref[0].astype(F32)
    tq, w = q.shape
    lane = lax.broadcasted_iota(jnp.int32, q.shape, 1)
    qq = jnp.concatenate([jnp.where(lane < w // 2, q, 0.0),
                          jnp.where(lane >= w // 2, q, 0.0)], axis=0).astype(BF16)
    vd = kv_refs[1].shape[-1]

    if online_max:
        def step(kt, v, carry):
            m, l, acc = carry
            s = jnp.dot(qq, kt, preferred_element_type=F32)
            m_new = jnp.maximum(m, jnp.max(s, axis=-1, keepdims=True))
            alpha = jnp.exp2(m - m_new)
            p = jnp.exp2(s - m_new)
            l = alpha * l + jnp.sum(p, axis=-1, keepdims=True)
            acc = alpha * acc + jnp.dot(p.astype(BF16), v, preferred_element_type=F32)
            return m_new, l, acc

        carry = (jnp.full((2 * tq, 1), -jnp.inf, F32), jnp.zeros((2 * tq, 1), F32),
                 jnp.zeros((2 * tq, vd), F32))
    else:
        bound = sc_ref[1]

        def step(kt, v, carry):
            l_part, acc = carry
            s = jnp.dot(qq, kt, preferred_element_type=F32)
            p = jnp.exp2(s - bound)
            for c in range(s.shape[1] // LANES):
                l_part = l_part + p[:, c * LANES:(c + 1) * LANES]
            acc = acc + jnp.dot(p.astype(BF16), v, preferred_element_type=F32)
            return l_part, acc

        carry = (jnp.zeros((2 * tq, LANES), F32), jnp.zeros((2 * tq, vd), F32))

    for n in range(n_src):
        kt_ref, v_ref, grp = kv_refs[2 * n], kv_refs[2 * n + 1], groups[n]
        nblk, kb = kt_ref.shape[1], kt_ref.shape[3]
        if nblk == grp:
            for g in range(grp):
                carry = step(kt_ref[0, g], v_ref[0, g * kb:(g + 1) * kb, :], carry)
        else:
            def body(j, carry, kt_ref=kt_ref, v_ref=v_ref, grp=grp, kb=kb):
                for g in range(grp):
                    blk = j * grp + g
                    carry = step(kt_ref[0, blk], v_ref[0, pl.ds(pl.multiple_of(blk * kb, kb), kb), :], carry)
                return carry
            carry = lax.fori_loop(0, nblk // grp, body, carry)
    if online_max:
        _, l, acc = carry
    else:
        l_part, acc = carry
        l = jnp.sum(l_part, axis=-1, keepdims=True)
    o = acc / l
    d = o[:tq] - sc_ref[0] * o[tq:]
    ms = jnp.mean(d * d, axis=-1, keepdims=True)
    o_ref[0] = (d * lax.rsqrt(ms + EPS) * g_ref[...] * out_scale).astype(BF16)


def _attention(scalars, q, srcs, subln_g, *, out_scale, tq, online_max, name):
    B, S, aw = q.shape
    hd = aw // ATT_HEADS
    in_specs = [pl.BlockSpec(memory_space=pltpu.SMEM),
                pl.BlockSpec((1, tq, hd), lambda b, h, i: (b, i, h))]
    args = [scalars, q]
    for kt, v, _ in srcs:
        in_specs += [pl.BlockSpec((1, kt.shape[1], hd, kt.shape[3]), lambda b, h, i: (b, 0, h, 0)),
                     pl.BlockSpec((1, v.shape[1], hd), lambda b, h, i: (b, 0, h))]
        args += [kt, v]
    in_specs.append(pl.BlockSpec((1, hd), lambda b, h, i: (0, 0)))
    args.append(subln_g)
    return pl.pallas_call(
        functools.partial(_attn_kernel, n_src=len(srcs), groups=tuple(g for _, _, g in srcs),
                          out_scale=out_scale, online_max=online_max),
        grid=(B, ATT_HEADS, S // tq),
        in_specs=in_specs,
        out_specs=pl.BlockSpec((1, tq, hd), lambda b, h, i: (b, i, h)),
        out_shape=jax.ShapeDtypeStruct((B, S, aw), BF16),
        compiler_params=_cparams("arbitrary", "arbitrary", "arbitrary"),
        name=name,
    )(*args)


def _dft1_kernel(m_ref, z_ref, y_ref):
    y_ref[0] = jnp.dot(m_ref[...], z_ref[0], preferred_element_type=F32).astype(BF16)


def _dft2_kernel(y_ref, tab_ref, cc_ref, sc_ref, o_ref, *, scale):
    w = cc_ref.shape[0]
    for i in range(tab_ref.shape[0]):
        y = jnp.concatenate([y_ref[0, 0, i], y_ref[0, 1, i]], axis=0)
        zr = jnp.dot(tab_ref[i, 0], y, preferred_element_type=F32)
        zi = jnp.dot(tab_ref[i, 1], y, preferred_element_type=F32)
        o = (jnp.dot(zr.astype(BF16), cc_ref[...], preferred_element_type=F32)
             + jnp.dot(zi.astype(BF16), sc_ref[...], preferred_element_type=F32))
        o_ref[0, :, i * w:(i + 1) * w] = (o * scale).astype(BF16)


@functools.lru_cache(maxsize=None)
def _fourier_tables(T, fw):
    t1n, t2n = DFT_T1, T // DFT_T1
    gd = fw // F_GROUPS
    two_pi = 2.0 * np.pi
    k1 = np.arange(t1n, dtype=np.int64)
    a1 = two_pi * ((k1[:, None] * k1[None, :]) % t1n) / t1n
    m1 = np.concatenate([np.cos(a1), -np.sin(a1)], axis=0)
    k2 = np.arange(t2n, dtype=np.int64)
    kk = k1[:, None, None] + t1n * k2[None, :, None]
    ph = two_pi * ((kk * k2[None, None, :]) % T) / T
    cp, sp = np.cos(ph), np.sin(ph)
    tab = np.stack([np.concatenate([cp, sp], axis=-1),
                    np.concatenate([-sp, cp], axis=-1)], axis=1)
    c = np.arange(gd, dtype=np.int64)
    ac = two_pi * ((c[:, None] * c[None, :]) % gd) / gd
    eye = np.eye(F_GROUPS)
    cc, sc = np.kron(eye, np.cos(ac)), np.kron(eye, np.sin(ac))
    return tuple(np.asarray(t, np.float32) for t in (m1, tab, cc, sc))


def _fourier_long(uf, tables):
    B, T, W = uf.shape
    m1, tab, cc, sc = (jnp.asarray(t, BF16) for t in tables)
    t1n, t2n = DFT_T1, T // DFT_T1
    ncol = t2n * W
    tn = min(ncol, DFT_COL_TILE)
    y = pl.pallas_call(
        _dft1_kernel,
        grid=(B, ncol // tn),
        in_specs=[pl.BlockSpec((2 * t1n, t1n), lambda b, j: (0, 0)),
                  pl.BlockSpec((1, t1n, tn), lambda b, j: (b, 0, j))],
        out_specs=pl.BlockSpec((1, 2 * t1n, tn), lambda b, j: (b, 0, j)),
        out_shape=jax.ShapeDtypeStruct((B, 2 * t1n, ncol), BF16),
        compiler_params=_cparams("arbitrary", "arbitrary"),
        name="dft_stage1",
    )(m1, uf.reshape(B, t1n, ncol))
    y5 = y.reshape(B, 2, t1n, t2n, W)
    scale = 1.0 / math.sqrt(T * (W // F_GROUPS))
    out = pl.pallas_call(
        functools.partial(_dft2_kernel, scale=scale),
        grid=(t1n // DFT_K1_PER_STEP, B),
        in_specs=[pl.BlockSpec((1, 2, DFT_K1_PER_STEP, t2n, W), lambda k, b: (b, 0, k, 0, 0)),
                  pl.BlockSpec((DFT_K1_PER_STEP, 2, t2n, 2 * t2n), lambda k, b: (k, 0, 0, 0)),
                  pl.BlockSpec((W, W), lambda k, b: (0, 0)),
                  pl.BlockSpec((W, W), lambda k, b: (0, 0))],
        out_specs=pl.BlockSpec((1, t2n, DFT_K1_PER_STEP * W), lambda k, b: (b, 0, k)),
        out_shape=jax.ShapeDtypeStruct((B, t2n, t1n * W), BF16),
        compiler_params=_cparams("arbitrary", "arbitrary"),
        name="dft_stage2",
    )(y5, tab, cc, sc)
    return out.reshape(B, T, W)


def _dft_short_kernel(z_ref, ct_ref, st_ref, cc_ref, sc_ref, o_ref, *, scale):
    z = z_ref[0]
    zc = jnp.dot(z, cc_ref[...], preferred_element_type=F32).astype(BF16)
    zs = jnp.dot(z, sc_ref[...], preferred_element_type=F32).astype(BF16)
    o = (jnp.dot(ct_ref[...], zc, preferred_element_type=F32)
         - jnp.dot(st_ref[...], zs, preferred_element_type=F32))
    o_ref[0] = (o * scale).astype(BF16)


def _fourier_short(uf, cc, sc):
    B, T, W = uf.shape
    t = np.arange(T, dtype=np.int64)
    ang = 2.0 * np.pi * ((t[:, None] * t[None, :]) % T) / T
    ct, st = jnp.asarray(np.cos(ang), BF16), jnp.asarray(np.sin(ang), BF16)
    cc, sc = jnp.asarray(cc, BF16), jnp.asarray(sc, BF16)
    scale = 1.0 / math.sqrt(T * (W // F_GROUPS))
    return pl.pallas_call(
        functools.partial(_dft_short_kernel, scale=scale),
        grid=(B,),
        in_specs=[pl.BlockSpec((1, T, W), lambda b: (b, 0, 0)),
                  pl.BlockSpec((T, T), lambda b: (0, 0)), pl.BlockSpec((T, T), lambda b: (0, 0)),
                  pl.BlockSpec((W, W), lambda b: (0, 0)), pl.BlockSpec((W, W), lambda b: (0, 0))],
        out_specs=pl.BlockSpec((1, T, W), lambda b: (b, 0, 0)),
        out_shape=jax.ShapeDtypeStruct((B, T, W), BF16),
        compiler_params=_cparams("arbitrary"),
        name="dft_short",
    )(uf, ct, st, cc, sc)


def _affine_scan(a, b, reverse):
    T = a.shape[0]
    row = lax.broadcasted_iota(jnp.int32, a.shape, 0)
    k = 1
    while k < T:
        if k % SUBLANES == 0:
            one, zero = jnp.ones((k,) + a.shape[1:], a.dtype), jnp.zeros((k,) + a.shape[1:], a.dtype)
            if reverse:
                a_s, b_s = jnp.concatenate([a[k:], one], axis=0), jnp.concatenate([b[k:], zero], axis=0)
            else:
                a_s, b_s = jnp.concatenate([one, a[:T - k]], axis=0), jnp.concatenate([zero, b[:T - k]], axis=0)
        else:
            if reverse:
                a_s, b_s, valid = pltpu.roll(a, T - k, 0), pltpu.roll(b, T - k, 0), row < T - k
            else:
                a_s, b_s, valid = pltpu.roll(a, k, 0), pltpu.roll(b, k, 0), row >= k
            a_s, b_s = jnp.where(valid, a_s, 1.0), jnp.where(valid, b_s, 0.0)
        b = a * b_s + b
        a = a * a_s
        k *= 2
    return a, b


def _lru_kernel(*refs, reverse, combine, nt):
    if combine:
        (ur_ref, prev_ref, next_ref, h0_ref, cw_ref, cb_ref, wa_ref, ba_ref, wx_ref, bx_ref,
         nsp_ref, hf_ref, uy_ref, out_ref, hlast_ref, carry_ref) = refs
    else:
        (ur_ref, prev_ref, next_ref, h0_ref, cw_ref, cb_ref, wa_ref, ba_ref, wx_ref, bx_ref,
         nsp_ref, out_ref, hlast_ref, carry_ref) = refs
    i = pl.program_id(1)
    ci = nt - 1 - i if reverse else i

    @pl.when(i == 0)
    def _():
        carry_ref[...] = h0_ref[0]

    u = ur_ref[0]
    tc = u.shape[0]
    prev = jnp.where(ci == 0, 0.0, prev_ref[0])
    nxt = jnp.where(ci == nt - 1, 0.0, next_ref[0])
    ext = jnp.concatenate([prev, u, nxt], axis=0)
    n_ext = ext.shape[0]
    xr = cb_ref[...] + u * cw_ref[CONV_LEFT:CONV_LEFT + 1, :]
    for k in range(CONV_W):
        d = k - CONV_LEFT
        if d != 0:
            shifted = pltpu.roll(ext, (-d) % n_ext, 0)[SUBLANES:SUBLANES + tc]
            xr = xr + shifted * cw_ref[k:k + 1, :]
    xb = xr.astype(BF16)
    r = jax.nn.sigmoid(jnp.dot(xb, wa_ref[...], preferred_element_type=F32) + ba_ref[...])
    g = jax.nn.sigmoid(jnp.dot(xb, wx_ref[...], preferred_element_type=F32) + bx_ref[...])
    log_a = r * nsp_ref[...]
    a = jnp.exp(log_a)
    bt = jnp.sqrt(-jnp.tanh(log_a) * (a * a + 1.0)) * (g * xr)
    sub = math.gcd(LRU_SUBCHUNK, tc)
    n_sub = tc // sub
    parts = [None] * n_sub
    last = carry_ref[...]
    for j in (reversed(range(n_sub)) if reverse else range(n_sub)):
        a_cum, b_cum = _affine_scan(a[j * sub:(j + 1) * sub], bt[j * sub:(j + 1) * sub], reverse)
        hj = a_cum * last + b_cum
        last = hj[0:1] if reverse else hj[sub - 1:sub]
        parts[j] = hj
    h = jnp.concatenate(parts, axis=0) if n_sub > 1 else parts[0]
    carry_ref[...] = last
    hlast_ref[0] = last
    if combine:
        out_ref[0] = (jax.nn.gelu(uy_ref[0]) * (hf_ref[0] + h)).astype(BF16)
    else:
        out_ref[0] = h


def _lru_scan(ur, h0, p, d, *, reverse, hf=None, uy=None, tc):
    B, T, W = ur.shape
    nt = T // tc
    hb = tc // SUBLANES
    nh = T // SUBLANES
    combine = hf is not None
    cidx = (lambda i: nt - 1 - i) if reverse else (lambda i: i)
    tile = lambda b, i: (b, cidx(i), 0)
    const = lambda b, i: (0, 0)
    in_specs = [pl.BlockSpec((1, tc, W), tile),
                pl.BlockSpec((1, SUBLANES, W), lambda b, i: (b, jnp.maximum(cidx(i) * hb - 1, 0), 0)),
                pl.BlockSpec((1, SUBLANES, W), lambda b, i: (b, jnp.minimum((cidx(i) + 1) * hb, nh - 1), 0)),
                pl.BlockSpec((1, 1, W), lambda b, i: (b, 0, 0)),
                pl.BlockSpec((CONV_W, W), const), pl.BlockSpec((1, W), const),
                pl.BlockSpec((W, W), const), pl.BlockSpec((1, W), const),
                pl.BlockSpec((W, W), const), pl.BlockSpec((1, W), const),
                pl.BlockSpec((1, W), const)]
    args = [ur, ur, ur, h0.reshape(B, 1, W), p['conv_w'], p['conv_b'],
            p['wa'][d], p['ba'][d], p['wx'][d], p['bx'][d], p['nsp'][d]]
    if combine:
        in_specs += [pl.BlockSpec((1, tc, W), tile), pl.BlockSpec((1, tc, W), tile)]
        args += [hf, uy]
    out, hlast = pl.pallas_call(
        functools.partial(_lru_kernel, reverse=reverse, combine=combine, nt=nt),
        grid=(B, nt),
        in_specs=in_specs,
        out_specs=[pl.BlockSpec((1, tc, W), tile), pl.BlockSpec((1, 1, W), lambda b, i: (b, 0, 0))],
        out_shape=[jax.ShapeDtypeStruct((B, T, W), BF16 if combine else F32),
                   jax.ShapeDtypeStruct((B, 1, W), F32)],
        scratch_shapes=[pltpu.VMEM((1, W), F32)],
        compiler_params=_cparams("arbitrary", "arbitrary"),
        name=("lru_bwd" if reverse else "lru_fwd"),
    )(*args)
    return out, hlast.reshape(B, W)


def _store_token_tiles(ref, val, lead=()):
    rows, width = val.shape
    nsub = width // LANES
    for j in range(nsub):
        ref[lead + (pl.ds(j, rows, stride=nsub), slice(None))] = val[:, j * LANES:(j + 1) * LANES]


def _load_token_tiles(ref, rows, nsub, lead=()):
    return jnp.concatenate([ref[lead + (pl.ds(j, rows, stride=nsub), slice(None))] for j in range(nsub)], axis=1)


def _outproj_kernel(f_ref, a_ref, r_ref, x_ref, g1_ref, sh_ref, sc_ref, ng_ref, w_ref, wr_ref, br_ref,
                    x1_ref, h2_ref, lg_ref, *, fw, aw):
    y = (jnp.dot(f_ref[0], w_ref[0:fw, :], preferred_element_type=F32)
         + jnp.dot(a_ref[0], w_ref[fw:fw + aw, :], preferred_element_type=F32)
         + jnp.dot(r_ref[0], w_ref[fw + aw:, :], preferred_element_type=F32))
    x1 = x_ref[0] + g1_ref[0] * y
    x1_ref[0] = x1
    h2 = _rms_mod(x1, ng_ref[...], sc_ref[0], sh_ref[0])
    _store_token_tiles(h2_ref, h2, lead=(0,))
    lg_ref[0] = jnp.dot(h2.astype(BF16), wr_ref[...], preferred_element_type=F32) + br_ref[...]


def _outproj(four, att, rec, x, g1, sh2, sc2, ng, w_out, w_rt, b_rt, *, tm):
    B, T, D = x.shape
    fw, aw = four.shape[2], att.shape[2]
    nsub = D // LANES
    tile = lambda b, i: (b, i, 0)
    per_b = lambda b, i: (b, 0, 0)
    const = lambda b, i: (0, 0)
    x1, h2, logits = pl.pallas_call(
        functools.partial(_outproj_kernel, fw=fw, aw=aw),
        grid=(B, T // tm),
        in_specs=[pl.BlockSpec((1, tm, fw), tile), pl.BlockSpec((1, tm, aw), tile),
                  pl.BlockSpec((1, tm, rec.shape[2]), tile), pl.BlockSpec((1, tm, D), tile),
                  pl.BlockSpec((1, 1, D), per_b), pl.BlockSpec((1, 1, D), per_b),
                  pl.BlockSpec((1, 1, D), per_b), pl.BlockSpec((1, D), const),
                  pl.BlockSpec((D, D), const), pl.BlockSpec((D, ROUTE_PAD), const),
                  pl.BlockSpec((1, ROUTE_PAD), const)],
        out_specs=[pl.BlockSpec((1, tm, D), tile), pl.BlockSpec((1, tm * nsub, LANES), tile),
                   pl.BlockSpec((1, tm, ROUTE_PAD), tile)],
        out_shape=[jax.ShapeDtypeStruct((B, T, D), F32),
                   jax.ShapeDtypeStruct((B, T * nsub, LANES), F32),
                   jax.ShapeDtypeStruct((B, T, ROUTE_PAD), F32)],
        compiler_params=_cparams("arbitrary", "arbitrary"),
        name="outproj",
    )(four, att, rec, x, g1, sh2, sc2, ng, w_out, w_rt, b_rt)
    return x1, h2.reshape(B * T * nsub, LANES), logits.reshape(B * T, ROUTE_PAD)


def _dispatch_kernel(pe_ref, nu_ref, pos_ref, *refs, part_tiles, nsub, n_blocks):
    n_parts = len(part_tiles)
    h_refs = refs[:n_parts]
    xs_ref, zbuf, sem, zsem = refs[n_parts:]
    i = pl.program_id(0)
    blk = MOE_BLOCK * nsub

    def zero_copy(row0):
        return pltpu.make_async_copy(zbuf, xs_ref.at[pl.ds(pl.multiple_of(row0, blk), blk)], zsem)

    @pl.when(i == 0)
    def _():
        zbuf[...] = jnp.zeros_like(zbuf)

        def expert_block(e, start, wait):
            pend = pe_ref[e]

            @pl.when(pend > start)
            def _():
                if wait:
                    zero_copy(0).wait()
                else:
                    zero_copy((pend - MOE_BLOCK) * nsub).start()
            return pend

        def tail_block(b, c, wait):
            if wait:
                zero_copy(0).wait()
            else:
                zero_copy(b * blk).start()
            return c

        for wait in (False, True):
            lax.fori_loop(0, N_EXPERTS, functools.partial(expert_block, wait=wait), 0)
            lax.fori_loop(nu_ref[0], n_blocks, functools.partial(tail_block, wait=wait), 0)

    first = 0
    for h_ref, nt in zip(h_refs, part_tiles):
        tm = h_ref.shape[0] // nsub

        @pl.when((i >= first) & (i < first + nt))
        def _(h_ref=h_ref, tm=tm):
            def issue(t, c):
                src = h_ref.at[pl.ds(pl.multiple_of(t * nsub, nsub), nsub)]
                for k in range(TOP_K):
                    dst = pl.multiple_of(pos_ref[0, 0, TOP_K * t + k] * nsub, nsub)
                    pltpu.make_async_copy(src, xs_ref.at[pl.ds(dst, nsub)], sem).start(priority=k % 2)
                return c

            lax.fori_loop(0, tm, issue, 0, unroll=4)
            for k in range(TOP_K):
                pltpu.make_async_copy(h_ref, xs_ref.at[pl.ds(0, tm * nsub)], sem).wait()
        first += nt


def _dispatch(h2_parts, pos, pends, n_used, *, n_slots, tm, nsub):
    part_tiles = tuple(h.shape[0] // (nsub * tm) for h in h2_parts)
    firsts = [sum(part_tiles[:p]) for p in range(len(part_tiles))]
    in_specs = [pl.BlockSpec((1, 1, TOP_K * tm), lambda i, pe, nu: (i, 0, 0), memory_space=pltpu.SMEM)]
    for first, nt in zip(firsts, part_tiles):
        in_specs.append(pl.BlockSpec(
            (tm * nsub, LANES), lambda i, pe, nu, first=first, nt=nt: (jnp.clip(i - first, 0, nt - 1), 0)))
    return pl.pallas_call(
        functools.partial(_dispatch_kernel, part_tiles=part_tiles, nsub=nsub, n_blocks=n_slots // MOE_BLOCK),
        grid_spec=pltpu.PrefetchScalarGridSpec(
            num_scalar_prefetch=2,
            grid=(sum(part_tiles),),
            in_specs=in_specs,
            out_specs=pl.BlockSpec(memory_space=pl.ANY),
            scratch_shapes=[pltpu.VMEM((MOE_BLOCK * nsub, LANES), F32),
                            pltpu.SemaphoreType.DMA(()), pltpu.SemaphoreType.DMA(())]),
        out_shape=jax.ShapeDtypeStruct((n_slots * nsub, LANES), F32),
        compiler_params=_cparams("arbitrary"),
        name="moe_dispatch",
    )(pends, n_used, pos.reshape(-1, 1, TOP_K * tm), *h2_parts)


def _expert_kernel(be_ref, nu_ref, first_ref, nxt_ref, has_ref, slot_ref, x_ref, w1_hbm, w3_hbm, w2_hbm, o_ref,
                   w1_buf, w3_buf, w2_buf, sems, *, layer):
    i = pl.program_id(0)
    nsub = w2_buf.shape[2] // LANES

    def copies(e, slot):
        return [pltpu.make_async_copy(w1_hbm.at[layer, e], w1_buf.at[slot], sems.at[slot]),
                pltpu.make_async_copy(w3_hbm.at[layer, e], w3_buf.at[slot], sems.at[slot]),
                pltpu.make_async_copy(w2_hbm.at[layer, e], w2_buf.at[slot], sems.at[slot])]

    @pl.when(i < nu_ref[0])
    def _():
        slot = slot_ref[i]

        @pl.when(i == 0)
        def _():
            for cp in copies(be_ref[0], 0):
                cp.start()

        @pl.when(first_ref[i] == 1)
        def _():
            for cp in copies(be_ref[i], slot):
                cp.wait()

            @pl.when(has_ref[i] == 1)
            def _():
                for cp in copies(nxt_ref[i], 1 - slot):
                    cp.start()

        x = _load_token_tiles(x_ref, MOE_BLOCK, nsub)
        xb = x.astype(BF16)
        h1 = jnp.dot(xb, w1_buf[slot].astype(BF16), preferred_element_type=F32)
        h3 = jnp.dot(xb, w3_buf[slot].astype(BF16), preferred_element_type=F32)
        hb = (jax.nn.silu(h1) * h3).astype(BF16)
        _store_token_tiles(o_ref, jnp.dot(hb, w2_buf[slot].astype(BF16), preferred_element_type=F32))

    @pl.when(i >= nu_ref[0])
    def _():
        o_ref[...] = jnp.zeros_like(o_ref)


def _experts(xs, block_exp, n_used, w1, w3, w2, layer):
    de, D = w2.shape[2], w2.shape[3]
    nsub = D // LANES
    R = MOE_BLOCK
    P = xs.shape[0] // nsub
    n_blocks = P // R
    idx = jnp.arange(n_blocks, dtype=jnp.int32)
    first = jnp.concatenate([jnp.ones((1,), jnp.int32), (block_exp[1:] != block_exp[:-1]).astype(jnp.int32)])
    later = block_exp[None, :] > block_exp[:, None]
    nxt_idx = jnp.argmax(later, axis=1).astype(jnp.int32)
    has = (jnp.any(later, axis=1) & (nxt_idx < n_used[0])).astype(jnp.int32)
    nxt = block_exp[nxt_idx]
    slot = ((jnp.cumsum(first) - 1) % 2).astype(jnp.int32)
    clamp = lambda i, *_: (jnp.minimum(i, _[1][0] - 1), 0)
    return pl.pallas_call(
        functools.partial(_expert_kernel, layer=layer),
        grid_spec=pltpu.PrefetchScalarGridSpec(
            num_scalar_prefetch=6,
            grid=(n_blocks,),
            in_specs=[pl.BlockSpec((R * nsub, LANES), clamp),
                      pl.BlockSpec(memory_space=pl.ANY), pl.BlockSpec(memory_space=pl.ANY),
                      pl.BlockSpec(memory_space=pl.ANY)],
            out_specs=pl.BlockSpec((R * nsub, LANES), lambda i, *_: (i, 0)),
            scratch_shapes=[pltpu.VMEM((2, D, de), F32), pltpu.VMEM((2, D, de), F32), pltpu.VMEM((2, de, D), F32),
                            pltpu.SemaphoreType.DMA((2,))]),
        out_shape=jax.ShapeDtypeStruct((P * nsub, LANES), F32),
        compiler_params=_cparams("arbitrary"),
        name="moe_experts",
    )(block_exp, n_used, first, nxt, has, slot, xs, w1, w3, w2)


def _combine_kernel(pos_ref, nxt_ref, x_ref, g_ref, w_ref, ys_ref, o_ref, buf, sems):
    tm = x_ref.shape[1]
    nsub = x_ref.shape[2] // LANES
    step = pl.program_id(0) * pl.num_programs(1) + pl.program_id(1)
    n_steps = pl.num_programs(0) * pl.num_programs(1)
    slot = step % 2

    def issue(idx_ref, into):
        def body(t, c):
            for k in range(TOP_K):
                src = pl.multiple_of(idx_ref[0, 0, TOP_K * t + k] * nsub, nsub)
                pltpu.make_async_copy(ys_ref.at[pl.ds(src, nsub)],
                                      buf.at[into, k, pl.ds(pl.multiple_of(t * nsub, nsub), nsub)],
                                      sems.at[into]).start(priority=k % 2)
            return c
        lax.fori_loop(0, tm, body, 0, unroll=4)

    @pl.when(step == 0)
    def _():
        issue(pos_ref, 0)

    @pl.when(step + 1 < n_steps)
    def _():
        issue(nxt_ref, 1 - slot)

    for k in range(TOP_K):
        pltpu.make_async_copy(ys_ref.at[pl.ds(0, tm * nsub)], buf.at[slot, k], sems.at[slot]).wait()
    w = w_ref[0]
    y = w[:, 0:1] * _load_token_tiles(buf.at[slot, 0], tm, nsub)
    for k in range(1, TOP_K):
        y = y + w[:, k:k + 1] * _load_token_tiles(buf.at[slot, k], tm, nsub)
    o_ref[0] = x_ref[0] + g_ref[0] * y


def _combine(x1, g2, ys, pos, wts, *, row_off, tm):
    B, T, D = x1.shape
    nt = T // tm
    off = row_off // tm
    last = off + B * nt - 1
    pos3 = pos.reshape(-1, 1, TOP_K * tm)
    wts3 = wts.reshape(-1, tm, TOP_K)
    return pl.pallas_call(
        _combine_kernel,
        grid=(B, nt),
        in_specs=[pl.BlockSpec((1, 1, TOP_K * tm), lambda b, i: (b * nt + i + off, 0, 0),
                               memory_space=pltpu.SMEM),
                  pl.BlockSpec((1, 1, TOP_K * tm), lambda b, i: (jnp.minimum(b * nt + i + off + 1, last), 0, 0),
                               memory_space=pltpu.SMEM),
                  pl.BlockSpec((1, tm, D), lambda b, i: (b, i, 0)),
                  pl.BlockSpec((1, 1, D), lambda b, i: (b, 0, 0)),
                  pl.BlockSpec((1, tm, TOP_K), lambda b, i: (b * nt + i + off, 0, 0)),
                  pl.BlockSpec(memory_space=pl.ANY)],
        out_specs=pl.BlockSpec((1, tm, D), lambda b, i: (b, i, 0)),
        out_shape=jax.ShapeDtypeStruct((B, T, D), F32),
        scratch_shapes=[pltpu.VMEM((2, TOP_K, tm * (D // LANES), LANES), F32), pltpu.SemaphoreType.DMA((2,))],
        compiler_params=_cparams("arbitrary", "arbitrary"),
        name="moe_combine",
    )(pos3, pos3, x1, g2, wts3, ys)


def _route_kernel(lg_ref, tri_ref, o_ref, cnt_ref, base_ref):
    @pl.when(pl.program_id(0) == 0)
    def _():
        base_ref[...] = jnp.zeros_like(base_ref)

    lg = lg_ref[...]
    lane = lax.broadcasted_iota(jnp.int32, lg.shape, 1)
    lane_f = lane.astype(F32)
    ninf = jnp.float32(-jnp.inf)
    far = jnp.float32(ROUTE_PAD)
    is_g = lane < N_GROUPS
    gl = jnp.where(is_g, lg, ninf)
    gmax = jnp.max(gl, axis=-1, keepdims=True)
    gsum = jnp.sum(jnp.where(is_g, jnp.exp(gl - gmax), 0.0), axis=-1, keepdims=True)
    p_g = 1.0 / gsum
    g_idx = jnp.min(jnp.where(gl == gmax, lane_f, far), axis=-1, keepdims=True)
    e_lane = lane - N_GROUPS
    e_lane_f = e_lane.astype(F32)
    shift = EXPERTS_PER_GROUP.bit_length() - 1
    in_grp = ((e_lane >= 0) & (e_lane < N_EXPERTS)
              & (jnp.right_shift(e_lane, shift).astype(F32) == g_idx))
    el = jnp.where(in_grp, lg, ninf)
    v1 = jnp.max(el, axis=-1, keepdims=True)
    i1 = jnp.min(jnp.where(el == v1, e_lane_f, far), axis=-1, keepdims=True)
    el2 = jnp.where(e_lane_f == i1, ninf, el)
    v2 = jnp.max(el2, axis=-1, keepdims=True)
    i2 = jnp.min(jnp.where(el2 == v2, e_lane_f, far), axis=-1, keepdims=True)
    w1 = p_g / (1.0 + jnp.exp(v2 - v1))
    w2 = p_g - w1
    hit1, hit2 = e_lane_f == i1, e_lane_f == i2
    oh1, oh2 = hit1.astype(F32), hit2.astype(F32)
    tri = tri_ref[...]
    pre1 = jnp.dot(tri, oh1.astype(BF16), preferred_element_type=F32)
    pre2 = jnp.dot(tri, oh2.astype(BF16), preferred_element_type=F32)
    tot1 = jnp.sum(oh1, axis=0, keepdims=True)
    tot2 = jnp.sum(oh2, axis=0, keepdims=True)
    base = base_ref[...]
    r1 = jnp.sum(jnp.where(hit1, pre1 + base, 0.0), axis=-1, keepdims=True)
    r2 = jnp.sum(jnp.where(hit2, pre2 + (base + tot1), 0.0), axis=-1, keepdims=True)
    base = base + tot1 + tot2
    base_ref[...] = base
    cnt_ref[...] = base
    cols = (i1, i2, r1, r2, w1, w2)
    out = jnp.zeros(lg.shape, F32)
    for j, col in enumerate(cols):
        out = jnp.where(lane == j, col, out)
    o_ref[...] = out[:, :o_ref.shape[1]]


def _route(logits):
    N = logits.shape[0]
    tm = math.gcd(ROUTE_TOKENS, N)
    tri = jnp.asarray(np.tril(np.ones((tm, tm), np.float32), -1), BF16)
    out, cnt = pl.pallas_call(
        _route_kernel,
        grid=(N // tm,),
        in_specs=[pl.BlockSpec((tm, ROUTE_PAD), lambda i: (i, 0)),
                  pl.BlockSpec((tm, tm), lambda i: (0, 0))],
        out_specs=[pl.BlockSpec((tm, ROUTE_OUT), lambda i: (i, 0)),
                   pl.BlockSpec((1, ROUTE_PAD), lambda i: (0, 0))],
        out_shape=[jax.ShapeDtypeStruct((N, ROUTE_OUT), F32),
                   jax.ShapeDtypeStruct((1, ROUTE_PAD), F32)],
        scratch_shapes=[pltpu.VMEM((1, ROUTE_PAD), F32)],
        compiler_params=_cparams("arbitrary"),
        name="moe_route",
    )(logits, tri)
    eid = out[:, 0:TOP_K].astype(jnp.int32)
    rank = out[:, TOP_K:2 * TOP_K].astype(jnp.int32)
    wts = out[:, 2 * TOP_K:3 * TOP_K]
    counts = cnt[0, N_GROUPS:N_GROUPS + N_EXPERTS].astype(jnp.int32)
    return eid, rank, wts, counts


def _dispatch_plan(eid, rank, counts):
    N = eid.shape[0]
    A = N * TOP_K
    padded = (counts + MOE_BLOCK - 1) // MOE_BLOCK * MOE_BLOCK
    pends = jnp.cumsum(padded)
    pstarts = pends - padded
    hit = eid[..., None] == jnp.arange(N_EXPERTS, dtype=jnp.int32)
    dest = jnp.sum(jnp.where(hit, pstarts, 0), axis=-1) + rank
    n_blocks = -(-A // MOE_BLOCK) + N_EXPERTS
    starts = jnp.arange(n_blocks, dtype=jnp.int32) * MOE_BLOCK
    block_exp = jnp.minimum(jnp.sum((pends[None, :] <= starts[:, None]).astype(jnp.int32), axis=1),
                            N_EXPERTS - 1).astype(jnp.int32)
    n_used = (pends[-1] // MOE_BLOCK).astype(jnp.int32).reshape(1)
    return block_exp, n_used, dest, pends.astype(jnp.int32), n_blocks * MOE_BLOCK


def _moe(h2_parts, logits, w1, w3, w2, layer):
    eid, rank, wts, counts = _route(logits)
    block_exp, n_used, pos, pends, n_slots = _dispatch_plan(eid, rank, counts)
    nsub = w2.shape[3] // LANES
    tm = functools.reduce(math.gcd, [h.shape[0] // nsub for h in h2_parts], MOE_COPY_TOKENS)
    xs = _dispatch(h2_parts, pos, pends, n_used, n_slots=n_slots, tm=tm, nsub=nsub)
    ys = _experts(xs, block_exp, n_used, w1, w3, w2, layer)
    return ys, pos, wts


def _blockdiag(w):
    G, n, _ = w.shape
    eye = jnp.eye(G, dtype=w.dtype)
    return (eye[:, None, :, None] * w[:, :, None, :]).reshape(G * n, G * n)


@functools.lru_cache(maxsize=None)
def _rope_tables(S, qk_dim):
    half = qk_dim // 2
    nf = half // 2
    rows_n = S // GRID_W
    row = np.repeat(np.arange(rows_n, dtype=np.float32), GRID_W)
    col = np.tile(np.arange(GRID_W, dtype=np.float32), rows_n)
    freqs = (np.float32(ROPE_BASE) ** (-np.arange(nf, dtype=np.float32) / np.float32(nf))).astype(np.float32)
    ang_r = (row[:, None] * freqs).astype(np.float64)
    ang_c = (col[:, None] * freqs).astype(np.float64)
    cos = np.concatenate([np.cos(ang_r)] * 2 + [np.cos(ang_c)] * 2, axis=1)
    sin = np.concatenate([-np.sin(ang_r), np.sin(ang_r), -np.sin(ang_c), np.sin(ang_c)], axis=1)
    reps = LANES // qk_dim
    return np.tile(cos, (1, reps)).astype(np.float32), np.tile(sin, (1, reps)).astype(np.float32)


def kernel(x, c, ctx, c_ctx, w_mod, b_mod, norm1_g, norm2_g, w_in, q_norm_g, k_norm_g, lambda_q1, lambda_k1, lambda_q2, lambda_k2, subln_g, conv_w, conv_b, gate_a_w, gate_a_b, gate_x_w, gate_x_b, lru_lambda, w_out, w_group, b_group, w_router, b_router, w1, w3, w2):
    B, S, D = x.shape
    C = ctx.shape[1]
    L = w_mod.shape[0]
    qk_dim = q_norm_g.shape[1]
    fw = lw = D // 4
    aw = D // 2
    dims = (fw, aw, lw, qk_dim)
    tm_x, tm_c = min(ROW_TILE, S), min(ROW_TILE, C)
    n_ctx, n_lat = B * C, B * S

    n_rows = -(-(B + 1) // SUBLANES) * SUBLANES
    c_all = jnp.concatenate([c, c_ctx[None, :], jnp.zeros((n_rows - B - 1, D), F32)], axis=0)
    mod = _modulation(c_all, w_mod, b_mod)

    cos_t, sin_t = (jnp.asarray(t) for t in _rope_tables(S, qk_dim))
    dummy_tab = jnp.zeros((C, LANES), F32)
    four_tabs = _fourier_tables(S, fw)
    gmat = _blockdiag(jnp.ones((MXU_TILE // qk_dim, qk_dim, qk_dim), F32)).astype(BF16)

    xc = ctx
    for l in range(L):
        last = l == L - 1
        lam_init = 0.8 - 0.6 * math.exp(-0.3 * l)
        m = [mod[l, :, i * D:(i + 1) * D] for i in range(N_MOD)]
        mx = [a[:B, None, :] for a in m]
        mc = [jnp.broadcast_to(a[B][None, None, :], (B, 1, D)) for a in m]
        w_in_b = w_in[l].astype(BF16)
        w_out_b = w_out[l].astype(BF16)
        gqk = jnp.concatenate([jnp.tile(q_norm_g[l], aw // qk_dim),
                               jnp.tile(k_norm_g[l], aw // qk_dim)])[None, :]
        lam = (jnp.exp(jnp.sum(lambda_q1[l] * lambda_k1[l])) - jnp.exp(jnp.sum(lambda_q2[l] * lambda_k2[l]))
               + lam_init).astype(F32)
        s_bound = (ATT_BOUND_MARGIN * qk_dim ** 0.5 * LOG2E
                   * jnp.max(jnp.abs(q_norm_g[l])) * jnp.max(jnp.abs(k_norm_g[l]))).astype(F32)
        att_sc = jnp.stack([lam, s_bound])
        sub_g = subln_g[l][None, :]
        n1 = norm1_g[l][None, :]
        n2 = norm2_g[l][None, :]
        lru_p = {
            'conv_w': conv_w[l], 'conv_b': conv_b[l][None, :],
            'wa': [_blockdiag(gate_a_w[l, d]).astype(BF16) for d in range(2)],
            'wx': [_blockdiag(gate_x_w[l, d]).astype(BF16) for d in range(2)],
            'ba': [gate_a_b[l, d][None, :] for d in range(2)],
            'bx': [gate_x_b[l, d][None, :] for d in range(2)],
            'nsp': [(-LRU_C * jax.nn.softplus(-lru_lambda[l, d]))[None, :] for d in range(2)],
        }
        pad = ROUTE_PAD - N_GROUPS - N_EXPERTS
        w_rt = jnp.concatenate([w_group[l], w_router[l], jnp.zeros((D, pad), F32)], axis=1).astype(BF16)
        b_rt = jnp.concatenate([b_group[l], b_router[l], jnp.zeros((pad,), F32)])[None, :]

        ufc, qc, ktc, vc, uyc, urc = _inproj(xc, mc[0], mc[1], n1, w_in_b, gqk, gmat, dummy_tab, dummy_tab,
                                             dims=dims, use_rope=False, tm=tm_c, name="inproj_ctx")
        ufx, qx, ktx, vx, uyx, urx = _inproj(x, mx[0], mx[1], n1, w_in_b, gqk, gmat, cos_t, sin_t,
                                             dims=dims, use_rope=True, tm=tm_x, name="inproj_lat")
        grp_x = min(ATT_BLOCKS_PER_BODY, ktx.shape[1])
        att_srcs = [(ktc, vc, ktc.shape[1]), (ktx, vx, grp_x)]
        att_fn = functools.partial(_attention, att_sc, qx, att_srcs, sub_g, out_scale=1.0 - lam_init,
                                   tq=min(ATT_TQ, S))
        att_x = lax.cond(2.0 * s_bound <= ATT_BOUND_MAX_SPAN,
                         lambda: att_fn(online_max=False, name="attn_lat_bound"),
                         lambda: att_fn(online_max=True, name="attn_lat_online"))

        zeros_h = jnp.zeros((B, lw), F32)
        tc_c, tc_x = min(ROW_TILE, C), min(ROW_TILE, S)
        hc_f, hc_f_last = _lru_scan(urc, zeros_h, lru_p, 0, reverse=False, tc=tc_c)
        hx_f, _ = _lru_scan(urx, hc_f_last, lru_p, 0, reverse=False, tc=tc_x)
        rec_c, hc_b_first = _lru_scan(urc, zeros_h, lru_p, 1, reverse=True, hf=hc_f, uy=uyc, tc=tc_c)
        rec_x, _ = _lru_scan(urx, hc_b_first, lru_p, 1, reverse=True, hf=hx_f, uy=uyx, tc=tc_x)

        four_x = _fourier_long(ufx, four_tabs)

        if last:
            x1, h2, logits = _outproj(four_x, att_x, rec_x, x, mx[2], mx[3], mx[4], n2, w_out_b, w_rt, b_rt, tm=tm_x)
            ys, pos, wts = _moe([h2], logits, w1, w3, w2, l)
            x = _combine(x1, mx[5], ys, pos, wts, row_off=0, tm=min(MOE_COPY_TOKENS, S))
        else:
            att_c = _attention(att_sc, qc, [(ktc, vc, ktc.shape[1])], sub_g, out_scale=1.0 - lam_init,
                               tq=min(ATT_TQ, C), online_max=True, name="attn_ctx")
            four_c = _fourier_short(ufc, four_tabs[2], four_tabs[3])
            xc1, h2c, lgc = _outproj(four_c, att_c, rec_c, xc, mc[2], mc[3], mc[4], n2, w_out_b, w_rt, b_rt, tm=tm_c)
            x1, h2x, lgx = _outproj(four_x, att_x, rec_x, x, mx[2], mx[3], mx[4], n2, w_out_b, w_rt, b_rt, tm=tm_x)
            ys, pos, wts = _moe([h2c, h2x], jnp.concatenate([lgc, lgx], axis=0), w1, w3, w2, l)
            tmc = min(MOE_COPY_TOKENS, S, C)
            xc = _combine(xc1, mc[5], ys, pos, wts, row_off=0, tm=tmc)
            x = _combine(x1, mx[5], ys, pos, wts, row_off=n_ctx, tm=tmc)
    return x
```
